```python
import math
import jax
import jax.numpy as jnp
from jax import lax
import numpy as np

D_MODEL = 1024
BATCH = 16
SEQ = 2048
DEPTH = 2

CTX_LEN = 256
GRID_W = 64
N_EVEN = (DEPTH + 1) // 2
N_ODD = DEPTH // 2
ALPHA = (2.0 * DEPTH) ** 0.25
BETA = (8.0 * DEPTH) ** -0.25
N_MOD = 9
D_FF = 2816
LN_EPS = 1e-6
D_FOURIER = D_MODEL // 2
FOURIER_GROUPS = 8
D_HYENA = D_MODEL // 2
HY_ORDER = 2
HY_SHORT = 3
HY_EMB = 33
HY_HID = 64
HY_FAST_DECAY = 0.3
HY_SLOW_DECAY = 1.5
HY_TARGET = 1e-2
EV_IN = D_FOURIER + (HY_ORDER + 1) * D_HYENA
CHUNK = 64
GLA_HEADS = 4
GLA_DK = D_MODEL // 16
GLA_DV = D_MODEL // 8
GLA_RANK = 16
GLA_TAU = 16.0
D_GLA = GLA_HEADS * GLA_DV
HG_HEADS = 4
HG_D = 128
D_HG = HG_HEADS * HG_D
OD_SIZES = (GLA_HEADS * GLA_DK, GLA_HEADS * GLA_DK, D_GLA, D_GLA, GLA_RANK, GLA_RANK,
            D_HG, D_HG, D_HG, D_HG, D_HG)
OD_IN = sum(OD_SIZES)

kernel_name = 'hybrid_fourier_hyena_gla_hgrn2_prefix_dit'


def _split(p, sizes):
    idx = np.cumsum(np.array(sizes))[:-1].tolist()
    return jnp.split(p, idx, axis=-1)


def _ln(x):
    xf = x.astype(jnp.float32)
    mu = jnp.mean(xf, axis=-1, keepdims=True)
    var = jnp.mean(jnp.square(xf - mu), axis=-1, keepdims=True)
    return (xf - mu) * lax.rsqrt(var + LN_EPS)


def _modulate(x, shift, scale):
    return (_ln(x) * (1.0 + scale) + shift).astype(x.dtype)


def _post_ln(x, y, g, b):
    z = ALPHA * x.astype(jnp.float32) + y.astype(jnp.float32)
    return (_ln(z) * g + b).astype(x.dtype)


def _rms_heads(o, g):
    of = o.astype(jnp.float32)
    return of * lax.rsqrt(jnp.mean(jnp.square(of), axis=-1, keepdims=True) + LN_EPS) * g


def _ffn_sublayer(x, shift, scale, gate, w_in, w_out, g, b):
    a, u = jnp.split(_modulate(x, shift, scale) @ w_in, 2, axis=-1)
    y = (jax.nn.silu(a) * u) @ w_out
    return _post_ln(x, 0.5 * gate * y, g, b)


def _fourier_mix(a):
    B, L, C = a.shape
    ag = a.astype(jnp.float32).reshape(B, L, FOURIER_GROUPS, C // FOURIER_GROUPS)
    y = jnp.fft.fft2(ag, axes=(1, 3), norm='ortho').real
    return y.reshape(B, L, C).astype(a.dtype)


def _short_conv(u, w, b):
    up = jnp.pad(u, ((0, 0), (1, 1), (0, 0)))
    return up[:, :-2] * w[0] + up[:, 1:-1] * w[1] + up[:, 2:] * w[2] + b


def _hyena_filters(L, w1, b1, w2, b2, w3, freq):
    f32 = jnp.float32
    t = jnp.linspace(0.0, 1.0, L, dtype=f32)[:, None]
    bands = (HY_EMB - 1) // 2
    fr = jnp.linspace(1e-4, bands - 1, bands, dtype=f32)[None, :]
    idx = jnp.arange(L, dtype=f32)[:, None]
    w = 2.0 * math.pi * idx * fr / L
    z = jnp.concatenate([t, jnp.cos(w), -jnp.sin(w)], axis=-1)
    hdn = jnp.sin(freq * (z @ w1 + b1))
    hdn = jnp.sin(freq * (hdn @ w2 + b2))
    h = (hdn @ w3).astype(f32).reshape(L, 2, HY_ORDER, D_HYENA)
    deltas = jnp.abs(jnp.linspace(math.log(HY_TARGET) / HY_SLOW_DECAY,
                                  math.log(HY_TARGET) / HY_FAST_DECAY, D_HYENA, dtype=f32))
    h = h * jnp.exp(-t * deltas)[:, None, None, :]
    fwd, bwd = h[:, 0], h[:, 1]
    k = jnp.concatenate([fwd, jnp.zeros((1, HY_ORDER, D_HYENA), f32), jnp.flip(bwd[1:], axis=0)], axis=0)
    return k / (jnp.sum(jnp.abs(k), axis=0, keepdims=True) + 1e-6)


def _fft_conv(z, k):
    L = z.shape[1]
    Z = jnp.fft.rfft(z, n=2 * L, axis=1)
    K = jnp.fft.rfft(k, n=2 * L, axis=0)
    return jnp.fft.irfft(Z * K[None], n=2 * L, axis=1)[:, :L]


def _hyena(u, conv_w, conv_b, w1, b1, w2, b2, w3, freq, skip):
    L = u.shape[1]
    u = _short_conv(u, conv_w, conv_b)
    v, x1, x2 = jnp.split(u, 3, axis=-1)
    k = _hyena_filters(L, w1, b1, w2, b2, w3, freq)
    z = v.astype(jnp.float32)
    z = x1 * (_fft_conv(z, k[:, 0]) + z * skip[0])
    z = x2 * (_fft_conv(z, k[:, 1]) + z * skip[1])
    return z.astype(u.dtype)


def _even_mixer(h, w_in, w_out, conv_w, conv_b, w1, b1, w2, b2, w3, freq, skip):
    p = h @ w_in
    a, u = p[..., :D_FOURIER], p[..., D_FOURIER:]
    y = jnp.concatenate([_fourier_mix(a), _hyena(u, conv_w, conv_b, w1, b1, w2, b2, w3, freq, skip)], axis=-1)
    return y @ w_out


def _chunked_gla(q, k, v, g, s0):
    dt = v.dtype
    B, L, H, dk = q.shape
    dv = v.shape[-1]
    n = L // CHUNK
    blk = lambda a: a.astype(jnp.float32).reshape(B, n, CHUNK, H, a.shape[-1])
    q, k, v, g = blk(q), blk(k), blk(v), blk(g)
    G = jnp.cumsum(g, axis=2)
    G_last = G[:, :, -1:]
    q_in = q * jnp.exp(G)
    k_in = k * jnp.exp(-G)
    k_out = k * jnp.exp(G_last - G)
    att = jnp.einsum('bnihd,bnjhd->bnhij', q_in, k_in)
    upto_self = jnp.tril(jnp.ones((CHUNK, CHUNK), dtype=bool))
    att = jnp.where(upto_self, att, 0.0)
    o_intra = jnp.einsum('bnhij,bnjhv->bnihv', att, v)

    def step(S, inp):
        qc, kc, vc, dc = inp
        o = jnp.einsum('bihd,bhdv->bihv', qc, S)
        S = dc[..., None] * S + jnp.einsum('bjhd,bjhv->bhdv', kc, vc)
        return S, o

    xs = (jnp.moveaxis(q_in, 1, 0), jnp.moveaxis(k_out, 1, 0), jnp.moveaxis(v, 1, 0),
          jnp.moveaxis(jnp.exp(G_last[:, :, 0]), 1, 0))
    S, o_inter = lax.scan(step, s0.astype(jnp.float32), xs)
    o = o_intra + jnp.moveaxis(o_inter, 0, 1)
    return o.reshape(B, L, H, dv).astype(dt), S


def _final_state(k, v, g):
    G = jnp.cumsum(g.astype(jnp.float32), axis=1)
    w = jnp.exp(G[:, -1:] - G)
    return jnp.einsum('blhd,blhv->bhdv', k.astype(jnp.float32) * w, v.astype(jnp.float32))


def _flip(a):
    return jnp.flip(a, axis=1)


def _scan_two_way(lat, ctx, need_ctx_out):
    q, kf, gf, kb, gb, v = lat
    cq, ckf, cgf, ckb, cgb, cv = ctx
    B, _, H, dk = kf.shape
    dv = v.shape[-1]
    if need_ctx_out:
        s0 = jnp.zeros((B, H, dk, dv), jnp.float32)
        co_f, sf = _chunked_gla(cq, ckf, cv, cgf, s0)
        co_b, sb = _chunked_gla(_flip(cq), _flip(ckb), _flip(cv), _flip(cgb), s0)
        co = co_f + _flip(co_b)
    else:
        sf = _final_state(ckf, cv, cgf)
        sb = _final_state(_flip(ckb), _flip(cv), _flip(cgb))
        co = None
    o_f, _ = _chunked_gla(q, kf, v, gf, sf)
    o_b, _ = _chunked_gla(_flip(q), _flip(kb), _flip(v), _flip(gb), sb)
    return o_f + _flip(o_b), co


def _odd_mixer(h, hc, w_in, w_out, a_up, a_b, gla_g, lb, hg_g, need_ctx_out):
    B, L, _ = h.shape
    rows = L // GRID_W

    def to_col(a):
        return a.reshape((B, rows, GRID_W) + a.shape[2:]).swapaxes(1, 2).reshape(a.shape)

    def from_col(a):
        return a.reshape((B, GRID_W, rows) + a.shape[2:]).swapaxes(1, 2).reshape(a.shape)

    def features(u):
        lead = u.shape[:2]
        heads = lambda a, H: a.reshape(lead + (H, -1))
        qc, kc, vc, rc, af, ab, qd, ffl, fbl, idd, gd = _split(u @ w_in, OD_SIZES)
        q = heads(qc, GLA_HEADS) * (GLA_DK ** -0.5)
        k = heads(kc, GLA_HEADS)
        v = heads(vc, GLA_HEADS)
        gf = heads(jax.nn.log_sigmoid((af @ a_up[0] + a_b[0]).astype(jnp.float32)) / GLA_TAU, GLA_HEADS)
        gb = heads(jax.nn.log_sigmoid((ab @ a_up[1] + a_b[1]).astype(jnp.float32)) / GLA_TAU, GLA_HEADS)
        ff = lb + (1.0 - lb) * jax.nn.sigmoid(ffl.astype(jnp.float32))
        fb = lb + (1.0 - lb) * jax.nn.sigmoid(fbl.astype(jnp.float32))
        gla = (q, k, gf, k, gb, v)
        hg = (heads(qd, HG_HEADS), heads(1.0 - ff, HG_HEADS), heads(jnp.log(ff), HG_HEADS),
              heads(1.0 - fb, HG_HEADS), heads(jnp.log(fb), HG_HEADS), heads(idd, HG_HEADS))
        return gla, hg, heads(rc, GLA_HEADS), heads(gd, HG_HEADS)

    lat_gla, lat_hg, r, g = features(h)
    ctx_gla, ctx_hg, rc, gc = features(hc)
    o_gla, co_gla = _scan_two_way(lat_gla, ctx_gla, need_ctx_out)
    o_hg, co_hg = _scan_two_way(tuple(to_col(a) for a in lat_hg), ctx_hg, need_ctx_out)
    o_hg = from_col(o_hg)

    def readout(og, oh, r_, g_):
        yg = _rms_heads(og, gla_g) * jax.nn.silu(r_.astype(jnp.float32))
        yh = _rms_heads(oh.astype(jnp.float32) * jax.nn.sigmoid(g_.astype(jnp.float32)), hg_g)
        y = jnp.concatenate([yg.reshape(og.shape[:2] + (D_GLA,)), yh.reshape(oh.shape[:2] + (D_HG,))], axis=-1)
        return y.astype(h.dtype) @ w_out

    y = readout(o_gla, o_hg, r, g)
    yc = readout(co_gla, co_hg, rc, gc) if need_ctx_out else None
    return y, yc


def setup_inputs(seed: int = 0) -> dict:
    key = jax.random.key(seed)
    ks = iter(jax.random.split(key, 28))
    D = D_MODEL

    def nrm(shape, scale):
        return jax.random.normal(next(ks), shape, jnp.float32) * scale

    return {
        'x': nrm((BATCH, SEQ, D), 1.0),
        'c': nrm((BATCH, D), 1.0),
        'ctx': nrm((BATCH, CTX_LEN, D), 1.0),
        'c_ctx': nrm((D,), 1.0),
        'mod_w': nrm((DEPTH, D, N_MOD * D), D ** -0.5),
        'mod_b': nrm((DEPTH, N_MOD * D), 0.02),
        'ffn_w_in': nrm((DEPTH, 2, D, 2 * D_FF), D ** -0.5),
        'ffn_w_out': nrm((DEPTH, 2, D_FF, D), D_FF ** -0.5 * BETA),
        'ln_g': 1.0 + nrm((DEPTH, 3, D), 0.02),
        'ln_b': nrm((DEPTH, 3, D), 0.02),
        'ev_w_in': nrm((N_EVEN, D, EV_IN), D ** -0.5),
        'ev_w_out': nrm((N_EVEN, D_FOURIER + D_HYENA, D), (D_FOURIER + D_HYENA) ** -0.5 * BETA),
        'hy_conv_w': nrm((N_EVEN, HY_SHORT, 3 * D_HYENA), HY_SHORT ** -0.5),
        'hy_conv_b': nrm((N_EVEN, 3 * D_HYENA), 0.02),
        'hy_w1': nrm((N_EVEN, HY_EMB, HY_HID), HY_EMB ** -0.5),
        'hy_b1': nrm((N_EVEN, HY_HID), 0.02),
        'hy_w2': nrm((N_EVEN, HY_HID, HY_HID), HY_HID ** -0.5),
        'hy_b2': nrm((N_EVEN, HY_HID), 0.02),
        'hy_w3': nrm((N_EVEN, HY_HID, 2 * HY_ORDER * D_HYENA), HY_HID ** -0.5),
        'hy_freq': 1.0 + nrm((N_EVEN, HY_HID), 0.02),
        'hy_skip': nrm((N_EVEN, HY_ORDER, D_HYENA), 1.0),
        'od_w_in': nrm((N_ODD, D, OD_IN), D ** -0.5),
        'od_w_out': nrm((N_ODD, D_GLA + D_HG, D), (D_GLA + D_HG) ** -0.5 * BETA),
        'gla_a_up': nrm((N_ODD, 2, GLA_RANK, GLA_HEADS * GLA_DK), GLA_RANK ** -0.5),
        'gla_a_b': nrm((N_ODD, 2, GLA_HEADS * GLA_DK), 0.02),
        'gla_norm_g': 1.0 + nrm((N_ODD, GLA_DV), 0.02),
        'hg_lb': nrm((DEPTH, D_HG), 0.1),
        'hg_norm_g': 1.0 + nrm((N_ODD, HG_D), 0.02),
    }


def reference(x, c, ctx, c_ctx, mod_w, mod_b, ffn_w_in, ffn_w_out, ln_g, ln_b,
              ev_w_in, ev_w_out, hy_conv_w, hy_conv_b, hy_w1, hy_b1, hy_w2, hy_b2, hy_w3, hy_freq, hy_skip,
              od_w_in, od_w_out, gla_a_up, gla_a_b, gla_norm_g, hg_lb, hg_norm_g):
    B = x.shape[0]
    sm = jax.nn.softmax(hg_lb.astype(jnp.float32), axis=0)
    lower_bounds = jnp.cumsum(sm, axis=0) - sm[0]
    xc = ctx
    for i in range(DEPTH):
        last = i == DEPTH - 1
        odd = i % 2 == 1
        ctx_used = odd or not last
        j = i // 2
        m = (jax.nn.silu(c) @ mod_w[i] + mod_b[i]).reshape(B, N_MOD, 1, D_MODEL)
        mc = (jax.nn.silu(c_ctx) @ mod_w[i] + mod_b[i]).reshape(N_MOD, D_MODEL)
        x = _ffn_sublayer(x, m[:, 0], m[:, 1], m[:, 2], ffn_w_in[i, 0], ffn_w_out[i, 0], ln_g[i, 0], ln_b[i, 0])
        if ctx_used:
            xc = _ffn_sublayer(xc, mc[0], mc[1], mc[2], ffn_w_in[i, 0], ffn_w_out[i, 0], ln_g[i, 0], ln_b[i, 0])
        h = _modulate(x, m[:, 3], m[:, 4])
        if odd:
            hc = _modulate(xc, mc[3], mc[4])
            y, yc = _odd_mixer(h, hc, od_w_in[j], od_w_out[j], gla_a_up[j], gla_a_b[j], gla_norm_g[j],
                               lower_bounds[i], hg_norm_g[j], not last)
        else:
            ev = (ev_w_in[j], ev_w_out[j], hy_conv_w[j], hy_conv_b[j], hy_w1[j], hy_b1[j],
                  hy_w2[j], hy_b2[j], hy_w3[j], hy_freq[j], hy_skip[j])
            y = _even_mixer(h, *ev)
            yc = _even_mixer(_modulate(xc, mc[3], mc[4]), *ev) if not last else None
        x = _post_ln(x, m[:, 5] * y, ln_g[i, 1], ln_b[i, 1])
        x = _ffn_sublayer(x, m[:, 6], m[:, 7], m[:, 8], ffn_w_in[i, 1], ffn_w_out[i, 1], ln_g[i, 2], ln_b[i, 2])
        if not last:
            xc = _post_ln(xc, mc[5] * yc, ln_g[i, 1], ln_b[i, 1])
            xc = _ffn_sublayer(xc, mc[6], mc[7], mc[8], ffn_w_in[i, 1], ffn_w_out[i, 1], ln_g[i, 2], ln_b[i, 2])
    return x
```

```python
import functools
import math

import jax
import jax.numpy as jnp
from jax import lax
from jax.experimental import pallas as pl
from jax.experimental.pallas import tpu as pltpu

F32 = jnp.float32
BF16 = jnp.bfloat16

D_MODEL = 1024
DEPTH = 2
N_MOD = 9
D_FF = 2816
LN_EPS = 1e-6
ALPHA = (2.0 * DEPTH) ** 0.25
GRID_W = 64
D_FOURIER = 512
FOURIER_GROUPS = 8
D_HYENA = 512
HY_EMB = 33
HY_HID = 64
HY_FAST_DECAY = 0.3
HY_SLOW_DECAY = 1.5
HY_TARGET = 1e-2
CHUNK = 64
GLA_HEADS = 4
GLA_DK = 64
GLA_DV = 128
GLA_RANK = 16
GLA_TAU = 16.0
HG_HEADS = 4
HG_D = 128
D_MIX = 512

LANE = 128
MOD_ROWS = 24
VMEM_LIMIT = 56 * 1024 * 1024

P_QK, P_V, P_R, P_QD, P_FFL, P_FBL, P_IDD, P_GD, P_GATE = range(9)
OD_PROJ_W = 8 * D_MIX + LANE
OD_PROJ_OUT = 9 * D_MIX


def _params(*sem):
    return pltpu.CompilerParams(dimension_semantics=sem, vmem_limit_bytes=VMEM_LIMIT)


def _dot(a, b):
    return jnp.dot(a, b, preferred_element_type=F32)


def _dot_nt(a, b):
    return lax.dot_general(a, b, (((1,), (1,)), ((), ())), preferred_element_type=F32)


def _dot_tn(a, b):
    return lax.dot_general(a, b, (((0,), (0,)), ((), ())), preferred_element_type=F32)


def _ln(x):
    mu = jnp.mean(x, axis=-1, keepdims=True)
    xc = x - mu
    var = jnp.mean(xc * xc, axis=-1, keepdims=True)
    return xc * lax.rsqrt(var + LN_EPS)


def _sigmoid(x):
    return 1.0 / (1.0 + jnp.exp(-x))


def _mod_row(mod_ref, row, k):
    return mod_ref[pl.ds(row, 1), k * D_MODEL:(k + 1) * D_MODEL]


def _mod_row_index(tile, tm, seq, is_ctx):
    if is_ctx:
        return MOD_ROWS - 8
    return (tile * tm) // seq


def _const_spec(shape):
    return pl.BlockSpec(shape, lambda *_: (0,) * len(shape), pipeline_mode=pl.Buffered(1))


def _mod_kernel(c_ref, w_ref, b_ref, o_ref):
    c = c_ref[...]
    s = (c * _sigmoid(c)).astype(BF16)
    o_ref[...] = _dot(s, w_ref[...].astype(BF16)) + b_ref[...]


def _mod_table(c_all, mod_w, mod_b):
    return pl.pallas_call(
        _mod_kernel,
        grid=(DEPTH, N_MOD),
        in_specs=[
            _const_spec((MOD_ROWS, D_MODEL)),
            pl.BlockSpec((None, D_MODEL, D_MODEL), lambda l, j: (l, 0, j)),
            pl.BlockSpec((None, 1, D_MODEL), lambda l, j: (l, 0, j)),
        ],
        out_specs=pl.BlockSpec((None, MOD_ROWS, D_MODEL), lambda l, j: (l, 0, j)),
        out_shape=jax.ShapeDtypeStruct((DEPTH, MOD_ROWS, N_MOD * D_MODEL), F32),
        compiler_params=_params("parallel", "parallel"),
        name="mod_table",
    )(c_all, mod_w, mod_b.reshape(DEPTH, 1, N_MOD * D_MODEL))


FFN_CHUNK = 256


def _ffn_kernel(x_ref, mod_ref, win_ref, wout_ref, g_ref, b_ref, o_ref, *, k0, tm, seq, is_ctx):
    row = _mod_row_index(pl.program_id(0), tm, seq, is_ctx)
    shift, scale, gate = (_mod_row(mod_ref, row, k0 + t) for t in range(3))
    x = x_ref[...]
    xm = (_ln(x) * (1.0 + scale) + shift).astype(BF16)
    acc = None
    for j in range(D_FF // FFN_CHUNK):
        lo = j * FFN_CHUNK
        a = _dot(xm, win_ref[:, lo:lo + FFN_CHUNK])
        u = _dot(xm, win_ref[:, D_FF + lo:D_FF + lo + FFN_CHUNK])
        h = (a * _sigmoid(a) * u).astype(BF16)
        y = _dot(h, wout_ref[lo:lo + FFN_CHUNK, :])
        acc = y if acc is None else acc + y
    z = ALPHA * x + (0.5 * gate) * acc
    o_ref[...] = _ln(z) * g_ref[...] + b_ref[...]


def _ffn(x2, mod, k0, w_in, w_out, g, b, *, seq, is_ctx, tm=512):
    n = x2.shape[0]
    tm = min(tm, n)
    kern = functools.partial(_ffn_kernel, k0=k0, tm=tm, seq=seq, is_ctx=is_ctx)
    return pl.pallas_call(
        kern,
        grid=(n // tm,),
        in_specs=[
            pl.BlockSpec((tm, D_MODEL), lambda i: (i, 0)),
            _const_spec(mod.shape),
            _const_spec(w_in.shape),
            _const_spec(w_out.shape),
            _const_spec((1, D_MODEL)),
            _const_spec((1, D_MODEL)),
        ],
        out_specs=pl.BlockSpec((tm, D_MODEL), lambda i: (i, 0)),
        out_shape=jax.ShapeDtypeStruct((n, D_MODEL), F32),
        compiler_params=_params("parallel"),
        name="ffn",
    )(x2, mod, w_in, w_out, g.reshape(1, D_MODEL), b.reshape(1, D_MODEL))


def _log_sigmoid(x):
    return jnp.minimum(x, 0.0) - jnp.log(1.0 + jnp.exp(-jnp.abs(x)))


def _proj_kernel(x_ref, mod_ref, w_ref, *rest, tm, seq, is_ctx, gla_gates):
    row = _mod_row_index(pl.program_id(0), tm, seq, is_ctx)
    shift, scale = _mod_row(mod_ref, row, 3), _mod_row(mod_ref, row, 4)
    xm = (_ln(x_ref[...]) * (1.0 + scale) + shift).astype(BF16)
    p = _dot(xm, w_ref[...])
    if not gla_gates:
        (o_ref,) = rest
        o_ref[...] = p
        return
    aup_ref, ab_ref, o_ref = rest
    main = 8 * D_MIX
    o_ref[:, :main] = p[:, :main]
    low = p[:, main:].astype(BF16)
    o_ref[:, main:] = _log_sigmoid(_dot(low, aup_ref[...]) + ab_ref[...]) * (1.0 / GLA_TAU)


def _proj(x2, mod, w, *, seq, is_ctx, gates=None, tm=256):
    n = x2.shape[0]
    tm = min(tm, n)
    n_out = OD_PROJ_OUT if gates is not None else w.shape[1]
    kern = functools.partial(_proj_kernel, tm=tm, seq=seq, is_ctx=is_ctx, gla_gates=gates is not None)
    extra = [] if gates is None else list(gates)
    return pl.pallas_call(
        kern,
        grid=(n // tm,),
        in_specs=[pl.BlockSpec((tm, D_MODEL), lambda i: (i, 0)), _const_spec(mod.shape), _const_spec(w.shape)]
        + [_const_spec(e.shape) for e in extra],
        out_specs=pl.BlockSpec((tm, n_out), lambda i: (i, 0)),
        out_shape=jax.ShapeDtypeStruct((n, n_out), F32),
        compiler_params=_params("parallel"),
        name="mixer_proj",
    )(x2, mod, w, *extra)


def _fourier_kernel(a_ref, cg_ref, cs_ref, o_ref, stack_ref, *, seq):
    @pl.when(pl.program_id(1) == 0)
    def _():
        p = _dot(a_ref[...].astype(BF16), cg_ref[...])
        stack_ref[0:seq, :] = p[:, :D_MIX].astype(BF16)
        stack_ref[seq:2 * seq, :] = p[:, D_MIX:].astype(BF16)

    scale = 1.0 / math.sqrt(seq * (D_FOURIER // FOURIER_GROUPS))
    o_ref[...] = _dot(cs_ref[...], stack_ref[...]) * scale


def _fourier(p3, cg, cs, *, tl):
    nb, seq, _ = p3.shape
    return pl.pallas_call(
        functools.partial(_fourier_kernel, seq=seq),
        grid=(nb, seq // tl),
        in_specs=[
            pl.BlockSpec((None, seq, D_MIX), lambda b, j: (b, 0, 0)),
            _const_spec(cg.shape),
            pl.BlockSpec((tl, 2 * seq), lambda b, j: (j, 0)),
        ],
        out_specs=pl.BlockSpec((None, tl, D_MIX), lambda b, j: (b, j, 0)),
        out_shape=jax.ShapeDtypeStruct((nb, seq, D_MIX), F32),
        scratch_shapes=[pltpu.VMEM((2 * seq, D_MIX), BF16)],
        compiler_params=_params("parallel", "arbitrary"),
        name="fourier_mix",
    )(p3, cg, cs)


def _hyena_filter_kernel(z_ref, w1_ref, b1_ref, w2_ref, b2_ref, w3f_ref, w3b_ref, freq_ref, delta_ref,
                         fwd_ref, bwd_ref):
    hi = lax.Precision.HIGHEST
    z = z_ref[...]
    freq = freq_ref[...]
    hdn = jnp.sin(freq * (jnp.dot(z, w1_ref[...], precision=hi, preferred_element_type=F32) + b1_ref[...]))
    hdn = jnp.sin(freq * (jnp.dot(hdn, w2_ref[...], precision=hi, preferred_element_type=F32) + b2_ref[...]))
    decay = jnp.exp(-z[:, 0:1] * delta_ref[...])
    fwd = jnp.dot(hdn, w3f_ref[...], precision=hi, preferred_element_type=F32) * decay
    bwd = jnp.dot(hdn, w3b_ref[...], precision=hi, preferred_element_type=F32) * decay
    row = lax.broadcasted_iota(jnp.int32, bwd.shape, 0)
    bwd = jnp.where(row == 0, 0.0, bwd)
    norm = jnp.sum(jnp.abs(fwd), axis=0, keepdims=True) + jnp.sum(jnp.abs(bwd), axis=0, keepdims=True) + 1e-6
    inv = 1.0 / norm
    fwd_ref[...] = fwd * inv
    bwd_ref[...] = bwd * inv


def _hyena_filters(seq, w1, b1, w2, b2, w3, freq):
    t = jnp.linspace(0.0, 1.0, seq, dtype=F32)[:, None]
    bands = (HY_EMB - 1) // 2
    fr = jnp.linspace(1e-4, bands - 1, bands, dtype=F32)[None, :]
    idx = jnp.arange(seq, dtype=F32)[:, None]
    w = 2.0 * math.pi * idx * fr / seq
    z = jnp.concatenate([t, jnp.cos(w), -jnp.sin(w), jnp.zeros((seq, LANE - HY_EMB), F32)], axis=-1)
    pad = LANE - HY_HID
    w1p = jnp.pad(w1, ((0, LANE - HY_EMB), (0, pad)))
    w2p = jnp.pad(w2, ((0, pad), (0, pad)))
    w3p = jnp.pad(w3, ((0, pad), (0, 0)))
    row = lambda v: jnp.pad(v, (0, pad)).reshape(1, LANE)
    delta = jnp.abs(jnp.linspace(math.log(HY_TARGET) / HY_SLOW_DECAY, math.log(HY_TARGET) / HY_FAST_DECAY,
                                 D_HYENA, dtype=F32)).reshape(1, D_HYENA)
    sq = lambda: _const_spec((LANE, LANE))
    vec = lambda: _const_spec((1, LANE))
    n_ord = 2
    return pl.pallas_call(
        _hyena_filter_kernel,
        grid=(n_ord,),
        in_specs=[_const_spec((seq, LANE)), sq(), vec(), sq(), vec(),
                  pl.BlockSpec((LANE, D_MIX), lambda o: (0, o)),
                  pl.BlockSpec((LANE, D_MIX), lambda o: (0, n_ord + o)),
                  vec(), _const_spec((1, D_MIX))],
        out_specs=[pl.BlockSpec((seq, D_MIX), lambda o: (0, o))] * 2,
        out_shape=[jax.ShapeDtypeStruct((seq, n_ord * D_MIX), F32)] * 2,
        compiler_params=_params("parallel"),
        name="hyena_filters",
    )(z, w1p, row(b1), w2p, row(b2), w3p, w3p, row(freq), delta)


def _alt_sum(x):
    row = lax.broadcasted_iota(jnp.int32, x.shape, 0)
    return jnp.sum(jnp.where((row & 1) == 1, -x, x), axis=0, keepdims=True)


def _hyena_spec_kernel(c_ref, s_ref, fwd_ref, bwd_ref, a_ref, bz_ref, kn_ref, *, seq):
    n = 2 * seq
    fwd, bwd = fwd_ref[...], bwd_ref[...]
    fb, bb = fwd.astype(BF16), bwd.astype(BF16)
    c, s = c_ref[...], s_ref[...]
    k_re = _dot(c, fb) + _dot(c, bb)
    k_im = _dot(s, bb) - _dot(s, fb)
    row = lax.broadcasted_iota(jnp.int32, k_re.shape, 0)
    wgt = jnp.where(row == 0, 1.0 / n, 2.0 / n)
    a_ref[...] = wgt * k_re
    bz_ref[...] = wgt * k_im
    kn_ref[...] = _alt_sum(fwd + bwd) * (1.0 / n)


def _hyena_spectra(c_tab, s_tab, fwd, bwd, *, cb=256):
    seq, width = fwd.shape
    blk = lambda: pl.BlockSpec((seq, cb), lambda j: (0, j))
    return pl.pallas_call(
        functools.partial(_hyena_spec_kernel, seq=seq),
        grid=(width // cb,),
        in_specs=[_const_spec((seq, seq)), _const_spec((seq, seq)), blk(), blk()],
        out_specs=[blk(), blk(), pl.BlockSpec((1, cb), lambda j: (0, j))],
        out_shape=[jax.ShapeDtypeStruct((seq, width), F32)] * 2 + [jax.ShapeDtypeStruct((1, width), F32)],
        compiler_params=_params("parallel"),
        name="hyena_spectra",
    )(c_tab, s_tab, fwd, bwd)


def _short_conv(u, w_ref, b_ref):
    seq = u.shape[0]
    row = lax.broadcasted_iota(jnp.int32, u.shape, 0)
    prev = jnp.where(row == 0, 0.0, pltpu.roll(u, 1, 0))
    nxt = jnp.where(row == seq - 1, 0.0, pltpu.roll(u, seq - 1, 0))
    return prev * w_ref[0:1, :] + u * w_ref[1:2, :] + nxt * w_ref[2:3, :] + b_ref[...]


def _hyena_order_kernel(z_ref, x_ref, c_ref, s_ref, a_ref, bz_ref, kn_ref, skip_ref, wx_ref, bx_ref, *rest,
                        order, first):
    if first:
        wz_ref, bz0_ref, o_ref = rest
        z = _short_conv(z_ref[...], wz_ref, bz0_ref)
    else:
        (o_ref,) = rest
        z = z_ref[...]
    mult = _short_conv(x_ref[...], wx_ref, bx_ref)
    c, s = c_ref[...], s_ref[...]
    zb = z.astype(BF16)
    re = _dot(c, zb)
    im = _dot(s, zb)
    a, bz = a_ref[...], bz_ref[...]
    wr = (re * a + im * bz).astype(BF16)
    wi = (re * bz - im * a).astype(BF16)
    nyq = _alt_sum(z) * kn_ref[...]
    row = lax.broadcasted_iota(jnp.int32, z.shape, 0)
    conv = _dot(c, wr) - _dot(s, wi) + jnp.where((row & 1) == 1, -nyq, nyq)
    o_ref[...] = mult * (conv + z * skip_ref[order:order + 1, :])


def _hyena_order(z3, z_blk0, x3, x_blk0, c_tab, s_tab, a, bz, kn, skip, conv_w, conv_b, *, order, first, cb=256):
    nb, seq, _ = x3.shape
    ncb = D_MIX // cb
    xw0 = (1 + order) * ncb
    conv_b = conv_b.reshape(1, -1)
    tok = lambda blk0: pl.BlockSpec((None, seq, cb), lambda j, b: (b, 0, blk0 + j))
    colblk = lambda rows, blk0: pl.BlockSpec((rows, cb), lambda j, b: (0, blk0 + j), pipeline_mode=pl.Buffered(1))
    in_specs = [tok(z_blk0), tok(x_blk0), _const_spec((seq, seq)), _const_spec((seq, seq)),
                colblk(seq, order * ncb), colblk(seq, order * ncb), colblk(1, order * ncb),
                colblk(2, 0), colblk(3, xw0), colblk(1, xw0)]
    args = [z3, x3, c_tab, s_tab, a, bz, kn, skip, conv_w, conv_b]
    if first:
        in_specs += [colblk(3, 0), colblk(1, 0)]
        args += [conv_w, conv_b]
    return pl.pallas_call(
        functools.partial(_hyena_order_kernel, order=order, first=first),
        grid=(ncb, nb),
        in_specs=in_specs,
        out_specs=pl.BlockSpec((None, seq, cb), lambda j, b: (b, 0, j)),
        out_shape=jax.ShapeDtypeStruct((nb, seq, D_MIX), F32),
        compiler_params=_params("parallel", "parallel"),
        name="hyena_order",
    )(*args)


def _rms_heads(x, g):
    parts = []
    for h in range(D_MIX // LANE):
        xh = x[:, h * LANE:(h + 1) * LANE]
        parts.append(xh * lax.rsqrt(jnp.mean(xh * xh, axis=-1, keepdims=True) + LN_EPS) * g)
    return jnp.concatenate(parts, axis=-1)


def _outproj_kernel(x_ref, ya_ref, yb_ref, *rest, tm, seq, is_ctx, readout):
    if readout:
        r_ref, gd_ref, gg_ref, hg_ref, mod_ref, w_ref, g_ref, b_ref, o_ref = rest
        r = r_ref[...]
        ya = _rms_heads(ya_ref[...], gg_ref[...]) * (r * _sigmoid(r))
        yb = _rms_heads(yb_ref[...] * _sigmoid(gd_ref[...]), hg_ref[...])
    else:
        mod_ref, w_ref, g_ref, b_ref, o_ref = rest
        ya, yb = ya_ref[...], yb_ref[...]
    row = _mod_row_index(pl.program_id(0), tm, seq, is_ctx)
    gate = _mod_row(mod_ref, row, 5)
    y = _dot(ya.astype(BF16), w_ref[:D_MIX, :]) + _dot(yb.astype(BF16), w_ref[D_MIX:, :])
    z = ALPHA * x_ref[...] + gate * y
    o_ref[...] = _ln(z) * g_ref[...] + b_ref[...]


def _outproj(x2, ya, yb, mod, w, g, b, *, seq, is_ctx, readout=None, tm=512):
    n = x2.shape[0]
    tm = min(tm, n)
    tok = lambda: pl.BlockSpec((tm, D_MIX), lambda i: (i, 0))
    in_specs = [pl.BlockSpec((tm, D_MODEL), lambda i: (i, 0)), tok(), tok()]
    args = [x2, ya, yb]
    if readout is not None:
        p, gla_g, hg_g = readout
        in_specs += [pl.BlockSpec((tm, D_MIX), lambda i: (i, P_R)), pl.BlockSpec((tm, D_MIX), lambda i: (i, P_GD)),
                     _const_spec((1, LANE)), _const_spec((1, LANE))]
        args += [p, p, gla_g.reshape(1, LANE), hg_g.reshape(1, LANE)]
    in_specs += [_const_spec(mod.shape), _const_spec(w.shape), _const_spec((1, D_MODEL)), _const_spec((1, D_MODEL))]
    args += [mod, w, g.reshape(1, D_MODEL), b.reshape(1, D_MODEL)]
    kern = functools.partial(_outproj_kernel, tm=tm, seq=seq, is_ctx=is_ctx, readout=readout is not None)
    return pl.pallas_call(
        kern,
        grid=(n // tm,),
        in_specs=in_specs,
        out_specs=pl.BlockSpec((tm, D_MODEL), lambda i: (i, 0)),
        out_shape=jax.ShapeDtypeStruct((n, D_MODEL), F32),
        compiler_params=_params("parallel"),
        name="mixer_out",
    )(*args)


def _split3(x):
    hi = x.astype(BF16)
    r1 = x - hi.astype(F32)
    mid = r1.astype(BF16)
    lo = (r1 - mid.astype(F32)).astype(BF16)
    return hi, mid, lo


def _scan_consts(kt, dk, rev):
    i = lax.broadcasted_iota(jnp.int32, (CHUNK, CHUNK), 0)
    j = lax.broadcasted_iota(jnp.int32, (CHUNK, CHUNK), 1)
    tri = jnp.where((j >= i) if rev else (j <= i), 1.0, 0.0).astype(BF16)
    heads = kt // dk
    r4 = lax.broadcasted_iota(jnp.int32, (heads * CHUNK, CHUNK), 0) & (CHUNK - 1)
    c4 = lax.broadcasted_iota(jnp.int32, (heads * CHUNK, CHUNK), 1)
    causal = (c4 >= r4) if rev else (c4 <= r4)
    lane_head = lax.broadcasted_iota(jnp.int32, (1, kt), 1) // dk
    srow = lax.broadcasted_iota(jnp.int32, (kt, D_MIX), 0) // dk
    scol = lax.broadcasted_iota(jnp.int32, (kt, D_MIX), 1) // (D_MIX // heads)
    return tri, causal, lane_head, srow == scol


def _scan_chunk(q, k, g, v, s_ref, consts, *, rev, need_out):
    tri, causal, lane_head, blockdiag = consts
    kt = k.shape[1]
    heads = causal.shape[0] // CHUNK
    dv = D_MIX // heads
    gsum = sum(_dot(tri, part) for part in _split3(g))
    gtot = gsum[0:1, :] if rev else gsum[CHUNK - 1:CHUNK, :]
    k_out = (k * jnp.exp(gtot - gsum)).astype(BF16)
    vb = v.astype(BF16)
    s_old = s_ref[...]
    o = None
    if need_out:
        q_in = q * jnp.exp(gsum)
        k_in = (k * jnp.exp(-gsum)).astype(BF16)
        q4 = jnp.concatenate([jnp.where(lane_head == h, q_in, 0.0) for h in range(heads)], axis=0).astype(BF16)
        att = jnp.where(causal, _dot_nt(q4, k_in), 0.0).astype(BF16)
        inter = _dot(q_in.astype(BF16), s_old.astype(BF16))
        o = jnp.concatenate(
            [inter[:, h * dv:(h + 1) * dv] + _dot(att[h * CHUNK:(h + 1) * CHUNK, :], vb[:, h * dv:(h + 1) * dv])
             for h in range(heads)], axis=-1)
    decay = jnp.transpose(jnp.broadcast_to(jnp.exp(gtot), (LANE, kt)))
    decay = jnp.concatenate([decay] * (D_MIX // LANE), axis=-1)
    s_ref[...] = s_old * decay + jnp.where(blockdiag, _dot_tn(k_out, vb), 0.0)
    return o


def _gla_kernel(qk_ref, v_ref, g_ref, cqk_ref, cv_ref, cg_ref, o_ref, s_ref, *, seq, ctx_len):
    kt = GLA_HEADS * GLA_DK
    for rev in (False, True):
        consts = _scan_consts(kt, GLA_DK, rev)
        gcol = kt if rev else 0
        s_ref[...] = jnp.zeros_like(s_ref)

        def ctx_body(n, carry, rev=rev, consts=consts, gcol=gcol):
            c = (ctx_len // CHUNK - 1 - n) if rev else n
            sl = pl.ds(pl.multiple_of(c * CHUNK, CHUNK), CHUNK)
            _scan_chunk(None, cqk_ref[sl, kt:2 * kt], cg_ref[sl, gcol:gcol + kt], cv_ref[sl, :], s_ref, consts,
                        rev=rev, need_out=False)
            return carry

        lax.fori_loop(0, ctx_len // CHUNK, ctx_body, 0)

        def body(n, carry, rev=rev, consts=consts, gcol=gcol):
            c = (seq // CHUNK - 1 - n) if rev else n
            sl = pl.ds(pl.multiple_of(c * CHUNK, CHUNK), CHUNK)
            q = qk_ref[sl, 0:kt] * (GLA_DK ** -0.5)
            o = _scan_chunk(q, qk_ref[sl, kt:2 * kt], g_ref[sl, gcol:gcol + kt], v_ref[sl, :], s_ref, consts,
                            rev=rev, need_out=True)
            if rev:
                o_ref[sl, :] += o
            else:
                o_ref[sl, :] = o
            return carry

        lax.fori_loop(0, seq // CHUNK, body, 0)


def _gla(p3, pc3):
    nb, seq, _ = p3.shape
    ctx_len = pc3.shape[1]
    blk = lambda length, j: pl.BlockSpec((None, length, D_MIX), lambda b: (b, 0, j))
    return pl.pallas_call(
        functools.partial(_gla_kernel, seq=seq, ctx_len=ctx_len),
        grid=(nb,),
        in_specs=[blk(seq, P_QK), blk(seq, P_V), blk(seq, P_GATE),
                  blk(ctx_len, P_QK), blk(ctx_len, P_V), blk(ctx_len, P_GATE)],
        out_specs=pl.BlockSpec((None, seq, D_MIX), lambda b: (b, 0, 0)),
        out_shape=jax.ShapeDtypeStruct((nb, seq, D_MIX), F32),
        scratch_shapes=[pltpu.VMEM((GLA_HEADS * GLA_DK, D_MIX), F32)],
        compiler_params=_params("parallel"),
        name="gla_scan",
    )(p3, p3, p3, pc3, pc3, pc3)


def _forget(lb, logits):
    f = lb + (1.0 - lb) * _sigmoid(logits)
    return 1.0 - f, jnp.log(f)


def _hgrn_kernel(*refs, seq, ctx_len, layer):
    nh = HG_HEADS
    q_refs, ffl_refs, fbl_refs, v_refs = (refs[i * nh:(i + 1) * nh] for i in range(4))
    cffl_ref, cfbl_ref, cv_ref, lb_ref, o_ref, s_ref, ocol_ref = refs[4 * nh:]
    kt = HG_HEADS * HG_D
    rows = seq // GRID_W
    cols_per_chunk = CHUNK // rows
    logits = lb_ref[...]
    e = jnp.exp(logits - jnp.max(logits, axis=0, keepdims=True))
    sm = e / jnp.sum(e, axis=0, keepdims=True)
    lb = jnp.sum(sm[0:layer + 1, :], axis=0, keepdims=True) - sm[0:1, :]

    def col_major(head_refs, n):
        return jnp.concatenate(
            [jnp.concatenate([ref[pl.ds(n * cols_per_chunk + w, rows, stride=GRID_W), :]
                              for w in range(cols_per_chunk)], axis=0) for ref in head_refs], axis=-1)

    ocol_ref[...] = jnp.zeros_like(ocol_ref)
    for rev in (False, True):
        consts = _scan_consts(kt, HG_D, rev)
        gate_ref, cgate_ref = (fbl_refs, cfbl_ref) if rev else (ffl_refs, cffl_ref)
        s_ref[...] = jnp.zeros_like(s_ref)

        def ctx_body(n, carry, rev=rev, consts=consts, cgate_ref=cgate_ref):
            c = (ctx_len // CHUNK - 1 - n) if rev else n
            sl = pl.ds(pl.multiple_of(c * CHUNK, CHUNK), CHUNK)
            k, g = _forget(lb, cgate_ref[sl, :])
            _scan_chunk(None, k, g, cv_ref[sl, :], s_ref, consts, rev=rev, need_out=False)
            return carry

        lax.fori_loop(0, ctx_len // CHUNK, ctx_body, 0)

        def body(n, carry, rev=rev, consts=consts, gate_ref=gate_ref):
            c = (seq // CHUNK - 1 - n) if rev else n
            k, g = _forget(lb, col_major(gate_ref, c))
            o = _scan_chunk(col_major(q_refs, c), k, g, col_major(v_refs, c), s_ref, consts, rev=rev, need_out=True)
            for h in range(nh):
                for w in range(cols_per_chunk):
                    dst = pl.ds(c * cols_per_chunk + w, rows, stride=GRID_W)
                    ocol_ref[h, dst, :] = ocol_ref[h, dst, :] + o[w * rows:(w + 1) * rows, h * HG_D:(h + 1) * HG_D]
            return carry

        lax.fori_loop(0, seq // CHUNK, body, 0)

    for h in range(nh):
        o_ref[:, h * HG_D:(h + 1) * HG_D] = ocol_ref[h]


def _hgrn(p3, pc3, hg_lb, layer):
    nb, seq, _ = p3.shape
    ctx_len = pc3.shape[1]
    per_blk = D_MIX // HG_D
    blk = lambda length, j: pl.BlockSpec((None, length, D_MIX), lambda b: (b, 0, j))
    head_blks = lambda j: [pl.BlockSpec((None, seq, HG_D), lambda b, col=j * per_blk + h: (b, 0, col))
                           for h in range(HG_HEADS)]
    return pl.pallas_call(
        functools.partial(_hgrn_kernel, seq=seq, ctx_len=ctx_len, layer=layer),
        grid=(nb,),
        in_specs=head_blks(P_QD) + head_blks(P_FFL) + head_blks(P_FBL) + head_blks(P_IDD)
        + [blk(ctx_len, P_FFL), blk(ctx_len, P_FBL), blk(ctx_len, P_IDD), _const_spec(hg_lb.shape)],
        out_specs=pl.BlockSpec((None, seq, D_MIX), lambda b: (b, 0, 0)),
        out_shape=jax.ShapeDtypeStruct((nb, seq, D_MIX), F32),
        scratch_shapes=[pltpu.VMEM((HG_HEADS * HG_D, D_MIX), F32), pltpu.VMEM((HG_HEADS, seq, HG_D), F32)],
        compiler_params=_params("parallel"),
        name="hgrn_scan",
    )(*([p3] * (4 * HG_HEADS)), pc3, pc3, pc3, hg_lb)


def _dft_tables(n_rows, n_cols, period):
    r = lax.broadcasted_iota(jnp.int32, (n_rows, n_cols), 0)
    c = lax.broadcasted_iota(jnp.int32, (n_rows, n_cols), 1)
    ang = ((r * c) % period).astype(F32) * (2.0 * math.pi / period)
    return jnp.cos(ang), jnp.sin(ang)


def _fourier_tables(seq):
    c, s = _dft_tables(seq, seq, seq)
    cs = jnp.concatenate([c, -s], axis=1).astype(BF16)
    gsz = D_FOURIER // FOURIER_GROUPS
    cg, sg = _dft_tables(gsz, gsz, gsz)
    eye = jnp.eye(FOURIER_GROUPS, dtype=F32)
    return cs, jnp.concatenate([jnp.kron(eye, cg), jnp.kron(eye, sg)], axis=1).astype(BF16)


def _hyena_tables(seq):
    c, s = _dft_tables(seq, seq, 2 * seq)
    return c.astype(BF16), s.astype(BF16)


def _even_mixer(x2, nb, seq, mod, is_ctx, w_in, w_out, conv_w, conv_b, filt, skip, ln_g, ln_b):
    p3 = _proj(x2, mod, w_in, seq=seq, is_ctx=is_ctx).reshape(nb, seq, -1)
    cs, cg = _fourier_tables(seq)
    ya = _fourier(p3, cg, cs, tl=min(seq, 512))
    c_tab, s_tab = _hyena_tables(seq)
    fwd, bwd = _hyena_filters(seq, *filt)
    a, bz, kn = _hyena_spectra(c_tab, s_tab, fwd, bwd)
    cb = 256
    u0 = D_FOURIER // cb
    per = D_HYENA // cb
    y1 = _hyena_order(p3, u0, p3, u0 + per, c_tab, s_tab, a, bz, kn, skip, conv_w, conv_b, order=0, first=True, cb=cb)
    y2 = _hyena_order(y1, 0, p3, u0 + 2 * per, c_tab, s_tab, a, bz, kn, skip, conv_w, conv_b, order=1, first=False,
                      cb=cb)
    return _outproj(x2, ya.reshape(nb * seq, D_MIX), y2.reshape(nb * seq, D_MIX), mod, w_out, ln_g, ln_b,
                    seq=seq, is_ctx=is_ctx)


def _odd_proj_weights(w_in, a_up, a_b):
    kq = GLA_HEADS * GLA_DK
    low0 = 2 * kq + 2 * D_MIX
    low1 = low0 + 2 * GLA_RANK
    w = jnp.concatenate([w_in[:, :low0], w_in[:, low1:], w_in[:, low0:low1],
                         jnp.zeros((D_MODEL, LANE - 2 * GLA_RANK), w_in.dtype)], axis=1).astype(BF16)
    up = jnp.zeros((LANE, 2 * kq), F32)
    up = up.at[0:GLA_RANK, 0:kq].set(a_up[0]).at[GLA_RANK:2 * GLA_RANK, kq:].set(a_up[1]).astype(BF16)
    return w, up, a_b.reshape(1, 2 * kq)


def kernel(x, c, ctx, c_ctx, mod_w, mod_b, ffn_w_in, ffn_w_out, ln_g, ln_b, ev_w_in, ev_w_out, hy_conv_w, hy_conv_b, hy_w1, hy_b1, hy_w2, hy_b2, hy_w3, hy_freq, hy_skip, od_w_in, od_w_out, gla_a_up, gla_a_b, gla_norm_g, hg_lb, hg_norm_g):
    nb, seq, d = x.shape
    ctx_len = ctx.shape[1]
    assert d == D_MODEL and seq % 512 == 0 and ctx_len % CHUNK == 0 and nb + 1 <= MOD_ROWS - 7
    assert mod_w.shape[0] == DEPTH == 2

    c_all = jnp.zeros((MOD_ROWS, D_MODEL), F32).at[:nb].set(c).at[MOD_ROWS - 8].set(c_ctx)
    mod = _mod_table(c_all, mod_w, mod_b)
    w_in_b = ffn_w_in.astype(BF16)
    w_out_b = ffn_w_out.astype(BF16)

    xs = x.reshape(nb * seq, D_MODEL)
    xc = ctx.reshape(nb * ctx_len, D_MODEL)
    streams = [(xs, seq, False), (xc, ctx_len, True)]

    def ffn(stream, layer, half):
        arr, length, is_ctx = stream
        return (_ffn(arr, mod[layer], 6 * half, w_in_b[layer, half], w_out_b[layer, half],
                     ln_g[layer, 2 * half], ln_b[layer, 2 * half], seq=length, is_ctx=is_ctx), length, is_ctx)

    streams = [ffn(s, 0, 0) for s in streams]
    filt = (hy_w1[0], hy_b1[0], hy_w2[0], hy_b2[0], hy_w3[0], hy_freq[0])
    ev_in_b, ev_out_b = ev_w_in[0].astype(BF16), ev_w_out[0].astype(BF16)
    streams = [(_even_mixer(arr, nb, length, mod[0], is_ctx, ev_in_b, ev_out_b, hy_conv_w[0], hy_conv_b[0], filt,
                            hy_skip[0], ln_g[0, 1], ln_b[0, 1]), length, is_ctx) for arr, length, is_ctx in streams]
    streams = [ffn(s, 0, 1) for s in streams]

    streams = [ffn(s, 1, 0) for s in streams]
    (xs, _, _), (xc, _, _) = streams
    w_proj, up, up_b = _odd_proj_weights(od_w_in[0], gla_a_up[0], gla_a_b[0])
    p = _proj(xs, mod[1], w_proj, seq=seq, is_ctx=False, gates=(up, up_b))
    pc = _proj(xc, mod[1], w_proj, seq=ctx_len, is_ctx=True, gates=(up, up_b))
    p3, pc3 = p.reshape(nb, seq, -1), pc.reshape(nb, ctx_len, -1)
    o_gla = _gla(p3, pc3).reshape(nb * seq, D_MIX)
    o_hg = _hgrn(p3, pc3, hg_lb, 1).reshape(nb * seq, D_MIX)
    xs = _outproj(xs, o_gla, o_hg, mod[1], od_w_out[0].astype(BF16), ln_g[1, 1], ln_b[1, 1], seq=seq, is_ctx=False,
                  readout=(p, gla_norm_g[0], hg_norm_g[0]))
    xs, _, _ = ffn((xs, seq, False), 1, 1)
    return xs.reshape(nb, seq, D_MODEL)
```

```python
import functools
import math

import jax
import jax.numpy as jnp
from jax import lax
from jax.experimental import pallas as pl
from jax.experimental.pallas import tpu as pltpu

F32 = jnp.float32
BF16 = jnp.bfloat16

D_MODEL = 1024
DEPTH = 2
N_MOD = 9
D_FF = 2816
LN_EPS = 1e-6
ALPHA = (2.0 * DEPTH) ** 0.25
GRID_W = 64
D_FOURIER = 512
FOURIER_GROUPS = 8
D_HYENA = 512
HY_EMB = 33
HY_HID = 64
HY_FAST_DECAY = 0.3
HY_SLOW_DECAY = 1.5
HY_TARGET = 1e-2
CHUNK = 64
GLA_HEADS = 4
GLA_DK = 64
GLA_DV = 128
GLA_RANK = 16
GLA_TAU = 16.0
HG_HEADS = 4
HG_D = 128
D_MIX = 512

LANE = 128
MOD_ROWS = 24
VMEM_LIMIT = 56 * 1024 * 1024

P_QK, P_V, P_R, P_QD, P_FFL, P_FBL, P_IDD, P_GD, P_GATE = range(9)
PM_QK, PM_V, PM_R, PM_GATE = range(4)
PH_QD, PH_FFL, PH_FBL, PH_IDD, PH_GD = range(5)


def _params(*sem):
    return pltpu.CompilerParams(dimension_semantics=sem, vmem_limit_bytes=VMEM_LIMIT)


def _dot(a, b):
    return jnp.dot(a, b, preferred_element_type=F32)


def _dot_nt(a, b):
    return lax.dot_general(a, b, (((1,), (1,)), ((), ())), preferred_element_type=F32)


def _dot_tn(a, b):
    return lax.dot_general(a, b, (((0,), (0,)), ((), ())), preferred_element_type=F32)


def _ln(x):
    mu = jnp.mean(x, axis=-1, keepdims=True)
    xc = x - mu
    var = jnp.mean(xc * xc, axis=-1, keepdims=True)
    return xc * lax.rsqrt(var + LN_EPS)


def _sigmoid(x):
    return 1.0 / (1.0 + jnp.exp(-x))


def _mod_row(mod_ref, row, k):
    return mod_ref[pl.ds(row, 1), k * D_MODEL:(k + 1) * D_MODEL]


def _mod_row_index(tile, tm, seq, is_ctx):
    if is_ctx:
        return MOD_ROWS - 8
    return (tile * tm) // seq


def _const_spec(shape):
    return pl.BlockSpec(shape, lambda *_: (0,) * len(shape), pipeline_mode=pl.Buffered(1))


def _mod_kernel(c_ref, w_ref, b_ref, o_ref):
    c = c_ref[...]
    s = (c * _sigmoid(c)).astype(BF16)
    o_ref[...] = _dot(s, w_ref[...].astype(BF16)) + b_ref[...]


def _mod_table(c_all, mod_w, mod_b):
    return pl.pallas_call(
        _mod_kernel,
        grid=(DEPTH, N_MOD),
        in_specs=[
            _const_spec((MOD_ROWS, D_MODEL)),
            pl.BlockSpec((None, D_MODEL, D_MODEL), lambda l, j: (l, 0, j)),
            pl.BlockSpec((None, 1, D_MODEL), lambda l, j: (l, 0, j)),
        ],
        out_specs=pl.BlockSpec((None, MOD_ROWS, D_MODEL), lambda l, j: (l, 0, j)),
        out_shape=jax.ShapeDtypeStruct((DEPTH, MOD_ROWS, N_MOD * D_MODEL), F32),
        compiler_params=_params("parallel", "parallel"),
        name="mod_table",
    )(c_all, mod_w, mod_b.reshape(DEPTH, 1, N_MOD * D_MODEL))


FFN_CHUNK = 256


def _ffn_kernel(x_ref, mod_ref, win_ref, wout_ref, g_ref, b_ref, o_ref, *, k0, tm, seq, is_ctx):
    row = _mod_row_index(pl.program_id(0), tm, seq, is_ctx)
    shift, scale, gate = (_mod_row(mod_ref, row, k0 + t) for t in range(3))
    x = x_ref[...]
    xm = (_ln(x) * (1.0 + scale) + shift).astype(BF16)
    acc = None
    for j in range(D_FF // FFN_CHUNK):
        lo = j * FFN_CHUNK
        a = _dot(xm, win_ref[:, lo:lo + FFN_CHUNK])
        u = _dot(xm, win_ref[:, D_FF + lo:D_FF + lo + FFN_CHUNK])
        h = (a * _sigmoid(a) * u).astype(BF16)
        y = _dot(h, wout_ref[lo:lo + FFN_CHUNK, :])
        acc = y if acc is None else acc + y
    z = ALPHA * x + (0.5 * gate) * acc
    o_ref[...] = _ln(z) * g_ref[...] + b_ref[...]


def _ffn(x2, mod, k0, w_in, w_out, g, b, *, seq, is_ctx, tm=512):
    n = x2.shape[0]
    tm = min(tm, n)
    kern = functools.partial(_ffn_kernel, k0=k0, tm=tm, seq=seq, is_ctx=is_ctx)
    return pl.pallas_call(
        kern,
        grid=(n // tm,),
        in_specs=[
            pl.BlockSpec((tm, D_MODEL), lambda i: (i, 0)),
            _const_spec(mod.shape),
            _const_spec(w_in.shape),
            _const_spec(w_out.shape),
            _const_spec((1, D_MODEL)),
            _const_spec((1, D_MODEL)),
        ],
        out_specs=pl.BlockSpec((tm, D_MODEL), lambda i: (i, 0)),
        out_shape=jax.ShapeDtypeStruct((n, D_MODEL), F32),
        compiler_params=_params("parallel"),
        name="ffn",
    )(x2, mod, w_in, w_out, g.reshape(1, D_MODEL), b.reshape(1, D_MODEL))


def _log_sigmoid(x):
    return jnp.minimum(x, 0.0) - jnp.log(1.0 + jnp.exp(-jnp.abs(x)))


def _proj_kernel(x_ref, mod_ref, w_ref, *rest, tm, seq, is_ctx, gla_gates):
    row = _mod_row_index(pl.program_id(0), tm, seq, is_ctx)
    shift, scale = _mod_row(mod_ref, row, 3), _mod_row(mod_ref, row, 4)
    xm = (_ln(x_ref[...]) * (1.0 + scale) + shift).astype(BF16)
    p = _dot(xm, w_ref[...])
    if not gla_gates:
        (o_ref,) = rest
        o_ref[...] = p
        return
    aup_ref, ab_ref, o_ref = rest
    main = p.shape[1] - LANE
    o_ref[:, :main] = p[:, :main]
    low = p[:, main:].astype(BF16)
    o_ref[:, main:] = _log_sigmoid(_dot(low, aup_ref[...]) + ab_ref[...]) * (1.0 / GLA_TAU)


def _proj_colmajor_kernel(x_ref, mod_ref, perm_ref, w_ref, o_ref):
    rows, cols, _ = x_ref.shape
    row = pl.program_id(0)
    shift, scale = _mod_row(mod_ref, row, 3), _mod_row(mod_ref, row, 4)
    x = x_ref[...].reshape(rows * cols, D_MODEL)
    xm = (_ln(x) * (1.0 + scale) + shift).astype(BF16)
    xm = _dot(perm_ref[...], xm).astype(BF16)
    o_ref[...] = _dot(xm, w_ref[...]).reshape(cols, rows, w_ref.shape[1])


def _proj_colmajor(x2, nb, seq, mod, w, *, cols=8):
    rows = seq // GRID_W
    n_out = w.shape[1]
    i = lax.broadcasted_iota(jnp.int32, (rows * cols, rows * cols), 0)
    j = lax.broadcasted_iota(jnp.int32, (rows * cols, rows * cols), 1)
    perm = jnp.where(j == (i % rows) * cols + i // rows, 1.0, 0.0).astype(BF16)
    return pl.pallas_call(
        _proj_colmajor_kernel,
        grid=(nb, GRID_W // cols),
        in_specs=[pl.BlockSpec((None, rows, cols, D_MODEL), lambda b, c: (b, 0, c, 0)),
                  _const_spec(mod.shape), _const_spec(perm.shape), _const_spec(w.shape)],
        out_specs=pl.BlockSpec((None, cols, rows, n_out), lambda b, c: (b, c, 0, 0)),
        out_shape=jax.ShapeDtypeStruct((nb, GRID_W, rows, n_out), F32),
        compiler_params=_params("parallel", "parallel"),
        name="mixer_proj_colmajor",
    )(x2.reshape(nb, rows, GRID_W, D_MODEL), mod, perm, w)


def _proj(x2, mod, w, *, seq, is_ctx, gates=None, tm=256):
    n = x2.shape[0]
    tm = min(tm, n)
    n_out = w.shape[1] - LANE + gates[0].shape[1] if gates is not None else w.shape[1]
    kern = functools.partial(_proj_kernel, tm=tm, seq=seq, is_ctx=is_ctx, gla_gates=gates is not None)
    extra = [] if gates is None else list(gates)
    return pl.pallas_call(
        kern,
        grid=(n // tm,),
        in_specs=[pl.BlockSpec((tm, D_MODEL), lambda i: (i, 0)), _const_spec(mod.shape), _const_spec(w.shape)]
        + [_const_spec(e.shape) for e in extra],
        out_specs=pl.BlockSpec((tm, n_out), lambda i: (i, 0)),
        out_shape=jax.ShapeDtypeStruct((n, n_out), F32),
        compiler_params=_params("parallel"),
        name="mixer_proj",
    )(x2, mod, w, *extra)


def _fourier_kernel(a_ref, cg_ref, cs_ref, o_ref, stack_ref, *, seq):
    @pl.when(pl.program_id(1) == 0)
    def _():
        p = _dot(a_ref[...].astype(BF16), cg_ref[...])
        stack_ref[0:seq, :] = p[:, :D_MIX].astype(BF16)
        stack_ref[seq:2 * seq, :] = p[:, D_MIX:].astype(BF16)

    scale = 1.0 / math.sqrt(seq * (D_FOURIER // FOURIER_GROUPS))
    o_ref[...] = _dot(cs_ref[...], stack_ref[...]) * scale


def _fourier(p3, cg, cs, *, tl):
    nb, seq, _ = p3.shape
    return pl.pallas_call(
        functools.partial(_fourier_kernel, seq=seq),
        grid=(nb, seq // tl),
        in_specs=[
            pl.BlockSpec((None, seq, D_MIX), lambda b, j: (b, 0, 0)),
            _const_spec(cg.shape),
            pl.BlockSpec((tl, 2 * seq), lambda b, j: (j, 0)),
        ],
        out_specs=pl.BlockSpec((None, tl, D_MIX), lambda b, j: (b, j, 0)),
        out_shape=jax.ShapeDtypeStruct((nb, seq, D_MIX), F32),
        scratch_shapes=[pltpu.VMEM((2 * seq, D_MIX), BF16)],
        compiler_params=_params("parallel", "arbitrary"),
        name="fourier_mix",
    )(p3, cg, cs)


def _hyena_filter_kernel(z_ref, w1_ref, b1_ref, w2_ref, b2_ref, w3f_ref, w3b_ref, freq_ref, delta_ref,
                         fwd_ref, bwd_ref):
    hi = lax.Precision.HIGHEST
    z = z_ref[...]
    freq = freq_ref[...]
    hdn = jnp.sin(freq * (jnp.dot(z, w1_ref[...], precision=hi, preferred_element_type=F32) + b1_ref[...]))
    hdn = jnp.sin(freq * (jnp.dot(hdn, w2_ref[...], precision=hi, preferred_element_type=F32) + b2_ref[...]))
    decay = jnp.exp(-z[:, 0:1] * delta_ref[...])
    fwd = jnp.dot(hdn, w3f_ref[...], precision=hi, preferred_element_type=F32) * decay
    bwd = jnp.dot(hdn, w3b_ref[...], precision=hi, preferred_element_type=F32) * decay
    row = lax.broadcasted_iota(jnp.int32, bwd.shape, 0)
    bwd = jnp.where(row == 0, 0.0, bwd)
    norm = jnp.sum(jnp.abs(fwd), axis=0, keepdims=True) + jnp.sum(jnp.abs(bwd), axis=0, keepdims=True) + 1e-6
    inv = 1.0 / norm
    fwd_ref[...] = fwd * inv
    bwd_ref[...] = bwd * inv


def _hyena_filters(seq, w1, b1, w2, b2, w3, freq):
    t = jnp.linspace(0.0, 1.0, seq, dtype=F32)[:, None]
    bands = (HY_EMB - 1) // 2
    fr = jnp.linspace(1e-4, bands - 1, bands, dtype=F32)[None, :]
    idx = jnp.arange(seq, dtype=F32)[:, None]
    w = 2.0 * math.pi * idx * fr / seq
    z = jnp.concatenate([t, jnp.cos(w), -jnp.sin(w), jnp.zeros((seq, LANE - HY_EMB), F32)], axis=-1)
    pad = LANE - HY_HID
    w1p = jnp.pad(w1, ((0, LANE - HY_EMB), (0, pad)))
    w2p = jnp.pad(w2, ((0, pad), (0, pad)))
    w3p = jnp.pad(w3, ((0, pad), (0, 0)))
    row = lambda v: jnp.pad(v, (0, pad)).reshape(1, LANE)
    delta = jnp.abs(jnp.linspace(math.log(HY_TARGET) / HY_SLOW_DECAY, math.log(HY_TARGET) / HY_FAST_DECAY,
                                 D_HYENA, dtype=F32)).reshape(1, D_HYENA)
    sq = lambda: _const_spec((LANE, LANE))
    vec = lambda: _const_spec((1, LANE))
    n_ord = 2
    return pl.pallas_call(
        _hyena_filter_kernel,
        grid=(n_ord,),
        in_specs=[_const_spec((seq, LANE)), sq(), vec(), sq(), vec(),
                  pl.BlockSpec((LANE, D_MIX), lambda o: (0, o)),
                  pl.BlockSpec((LANE, D_MIX), lambda o: (0, n_ord + o)),
                  vec(), _const_spec((1, D_MIX))],
        out_specs=[pl.BlockSpec((seq, D_MIX), lambda o: (0, o))] * 2,
        out_shape=[jax.ShapeDtypeStruct((seq, n_ord * D_MIX), F32)] * 2,
        compiler_params=_params("parallel"),
        name="hyena_filters",
    )(z, w1p, row(b1), w2p, row(b2), w3p, w3p, row(freq), delta)


def _alt_sum(x):
    row = lax.broadcasted_iota(jnp.int32, x.shape, 0)
    return jnp.sum(jnp.where((row & 1) == 1, -x, x), axis=0, keepdims=True)


def _hyena_spec_kernel(c_ref, s_ref, fwd_ref, bwd_ref, a_ref, bz_ref, kn_ref, *, seq):
    n = 2 * seq
    fwd, bwd = fwd_ref[...], bwd_ref[...]
    fb, bb = fwd.astype(BF16), bwd.astype(BF16)
    c, s = c_ref[...], s_ref[...]
    k_re = _dot(c, fb) + _dot(c, bb)
    k_im = _dot(s, bb) - _dot(s, fb)
    row = lax.broadcasted_iota(jnp.int32, k_re.shape, 0)
    wgt = jnp.where(row == 0, 1.0 / n, 2.0 / n)
    a_ref[...] = wgt * k_re
    bz_ref[...] = wgt * k_im
    kn_ref[...] = _alt_sum(fwd + bwd) * (1.0 / n)


def _hyena_spectra(c_tab, s_tab, fwd, bwd, *, cb=256):
    seq, width = fwd.shape
    blk = lambda: pl.BlockSpec((seq, cb), lambda j: (0, j))
    return pl.pallas_call(
        functools.partial(_hyena_spec_kernel, seq=seq),
        grid=(width // cb,),
        in_specs=[_const_spec((seq, seq)), _const_spec((seq, seq)), blk(), blk()],
        out_specs=[blk(), blk(), pl.BlockSpec((1, cb), lambda j: (0, j))],
        out_shape=[jax.ShapeDtypeStruct((seq, width), F32)] * 2 + [jax.ShapeDtypeStruct((1, width), F32)],
        compiler_params=_params("parallel"),
        name="hyena_spectra",
    )(c_tab, s_tab, fwd, bwd)


def _short_conv(u, w_ref, b_ref):
    seq = u.shape[0]
    row = lax.broadcasted_iota(jnp.int32, u.shape, 0)
    prev = jnp.where(row == 0, 0.0, pltpu.roll(u, 1, 0))
    nxt = jnp.where(row == seq - 1, 0.0, pltpu.roll(u, seq - 1, 0))
    return prev * w_ref[0:1, :] + u * w_ref[1:2, :] + nxt * w_ref[2:3, :] + b_ref[...]


def _hyena_order_kernel(z_ref, x_ref, c_ref, s_ref, a_ref, bz_ref, kn_ref, skip_ref, wx_ref, bx_ref, *rest,
                        order, first):
    if first:
        wz_ref, bz0_ref, o_ref = rest
        z = _short_conv(z_ref[...], wz_ref, bz0_ref)
    else:
        (o_ref,) = rest
        z = z_ref[...]
    mult = _short_conv(x_ref[...], wx_ref, bx_ref)
    c, s = c_ref[...], s_ref[...]
    zb = z.astype(BF16)
    re = _dot(c, zb)
    im = _dot(s, zb)
    a, bz = a_ref[...], bz_ref[...]
    wr = (re * a + im * bz).astype(BF16)
    wi = (re * bz - im * a).astype(BF16)
    nyq = _alt_sum(z) * kn_ref[...]
    row = lax.broadcasted_iota(jnp.int32, z.shape, 0)
    conv = _dot(c, wr) - _dot(s, wi) + jnp.where((row & 1) == 1, -nyq, nyq)
    o_ref[...] = mult * (conv + z * skip_ref[order:order + 1, :])


def _hyena_order(z3, z_blk0, x3, x_blk0, c_tab, s_tab, a, bz, kn, skip, conv_w, conv_b, *, order, first, cb=256):
    nb, seq, _ = x3.shape
    ncb = D_MIX // cb
    xw0 = (1 + order) * ncb
    conv_b = conv_b.reshape(1, -1)
    tok = lambda blk0: pl.BlockSpec((None, seq, cb), lambda j, b: (b, 0, blk0 + j))
    colblk = lambda rows, blk0: pl.BlockSpec((rows, cb), lambda j, b: (0, blk0 + j), pipeline_mode=pl.Buffered(1))
    in_specs = [tok(z_blk0), tok(x_blk0), _const_spec((seq, seq)), _const_spec((seq, seq)),
                colblk(seq, order * ncb), colblk(seq, order * ncb), colblk(1, order * ncb),
                colblk(2, 0), colblk(3, xw0), colblk(1, xw0)]
    args = [z3, x3, c_tab, s_tab, a, bz, kn, skip, conv_w, conv_b]
    if first:
        in_specs += [colblk(3, 0), colblk(1, 0)]
        args += [conv_w, conv_b]
    return pl.pallas_call(
        functools.partial(_hyena_order_kernel, order=order, first=first),
        grid=(ncb, nb),
        in_specs=in_specs,
        out_specs=pl.BlockSpec((None, seq, cb), lambda j, b: (b, 0, j)),
        out_shape=jax.ShapeDtypeStruct((nb, seq, D_MIX), F32),
        compiler_params=_params("parallel", "parallel"),
        name="hyena_order",
    )(*args)


def _rms_heads(x, g):
    parts = []
    for h in range(D_MIX // LANE):
        xh = x[:, h * LANE:(h + 1) * LANE]
        parts.append(xh * lax.rsqrt(jnp.mean(xh * xh, axis=-1, keepdims=True) + LN_EPS) * g)
    return jnp.concatenate(parts, axis=-1)


def _outproj_kernel(x_ref, ya_ref, yb_ref, *rest, tm, seq, is_ctx, readout):
    if readout:
        r_ref, gd_ref, gg_ref, hg_ref, unperm_ref, mod_ref, w_ref, g_ref, b_ref, o_ref = rest
        r = r_ref[...]
        ya = (_rms_heads(ya_ref[...], gg_ref[...]) * (r * _sigmoid(r))).astype(BF16)
        yb = _rms_heads(yb_ref[...].reshape(tm, D_MIX) * _sigmoid(gd_ref[...].reshape(tm, D_MIX)), hg_ref[...])
        yb = _dot(unperm_ref[...], yb.astype(BF16)).astype(BF16)
    else:
        mod_ref, w_ref, g_ref, b_ref, o_ref = rest
        ya, yb = ya_ref[...].astype(BF16), yb_ref[...].astype(BF16)
    row = _mod_row_index(pl.program_id(0), tm, seq, is_ctx)
    gate = _mod_row(mod_ref, row, 5)
    y = _dot(ya, w_ref[:D_MIX, :]) + _dot(yb, w_ref[D_MIX:, :])
    z = ALPHA * x_ref[...] + gate * y
    o_ref[...] = _ln(z) * g_ref[...] + b_ref[...]


def _outproj(x2, ya, yb, mod, w, g, b, *, seq, is_ctx, readout=None, tm=512):
    n = x2.shape[0]
    tm = min(tm, n)
    tok = lambda: pl.BlockSpec((tm, D_MIX), lambda i: (i, 0))
    in_specs = [pl.BlockSpec((tm, D_MODEL), lambda i: (i, 0)), tok()]
    args = [x2, ya, yb]
    if readout is None:
        in_specs += [tok()]
    else:
        p_main, p_hg, gla_g, hg_g = readout
        rows = tm // GRID_W
        tiles = seq // tm
        assert tm % GRID_W == 0 and rows % 8 == 0
        i = lax.broadcasted_iota(jnp.int32, (tm, tm), 0)
        j = lax.broadcasted_iota(jnp.int32, (tm, tm), 1)
        unperm = jnp.where(j == (i % GRID_W) * rows + i // GRID_W, 1.0, 0.0).astype(BF16)
        colmajor = lambda blk: pl.BlockSpec((None, GRID_W, rows, D_MIX), lambda i: (i // tiles, 0, i % tiles, blk))
        in_specs += [colmajor(0), pl.BlockSpec((tm, D_MIX), lambda i: (i, PM_R)), colmajor(PH_GD),
                     _const_spec((1, LANE)), _const_spec((1, LANE)), _const_spec((tm, tm))]
        args += [p_main, p_hg, gla_g.reshape(1, LANE), hg_g.reshape(1, LANE), unperm]
    in_specs += [_const_spec(mod.shape), _const_spec(w.shape), _const_spec((1, D_MODEL)), _const_spec((1, D_MODEL))]
    args += [mod, w, g.reshape(1, D_MODEL), b.reshape(1, D_MODEL)]
    kern = functools.partial(_outproj_kernel, tm=tm, seq=seq, is_ctx=is_ctx, readout=readout is not None)
    return pl.pallas_call(
        kern,
        grid=(n // tm,),
        in_specs=in_specs,
        out_specs=pl.BlockSpec((tm, D_MODEL), lambda i: (i, 0)),
        out_shape=jax.ShapeDtypeStruct((n, D_MODEL), F32),
        compiler_params=_params("parallel"),
        name="mixer_out",
    )(*args)


def _split3(x):
    hi = x.astype(BF16)
    r1 = x - hi.astype(F32)
    mid = r1.astype(BF16)
    lo = (r1 - mid.astype(F32)).astype(BF16)
    return hi, mid, lo


SCAN_UNROLL = 4


def _scan_unroll(n_chunks):
    return math.gcd(SCAN_UNROLL, n_chunks)


def _scan_consts(rev, unroll, heads, dk):
    n = unroll * CHUNK
    i = lax.broadcasted_iota(jnp.int32, (n, n), 0)
    j = lax.broadcasted_iota(jnp.int32, (n, n), 1)
    same_chunk = (i // CHUNK) == (j // CHUNK)
    tri = jnp.where(same_chunk & ((j >= i) if rev else (j <= i)), 1.0, 0.0).astype(BF16)
    r = lax.broadcasted_iota(jnp.int32, (CHUNK, heads * CHUNK), 0)
    c = lax.broadcasted_iota(jnp.int32, (CHUNK, heads * CHUNK), 1) & (CHUNK - 1)
    causal = (c >= r) if rev else (c <= r)
    klane_head = lax.broadcasted_iota(jnp.int32, (1, heads * dk), 1) // dk
    vlane_head = lax.broadcasted_iota(jnp.int32, (1, heads * LANE), 1) // LANE
    return tri, causal, klane_head, vlane_head


def _scan_local(q, k, g, vb, consts, *, rev, heads, dk, need_out):
    tri, causal, klane_head, vlane_head = consts
    kt = heads * dk
    gsum = sum(_dot(tri, part) for part in _split3(g))
    staged = []
    for u in range(g.shape[0] // CHUNK):
        rows = slice(u * CHUNK, (u + 1) * CHUNK)
        gs = gsum[rows]
        gt = gs[0:1] if rev else gs[CHUNK - 1:CHUNK]
        ku, vu = k[rows], vb[rows]
        k_out = (ku * jnp.exp(gt - gs)).astype(BF16)
        q_in = o_intra = None
        if need_out:
            q_in = (q[rows] * jnp.exp(gs)).astype(BF16)
            k_in = (ku * jnp.exp(-gs)).astype(BF16)
            k4 = jnp.concatenate([jnp.where(klane_head == h, k_in, jnp.zeros_like(k_in)) for h in range(heads)], axis=0)
            att = jnp.where(causal, _dot_nt(q_in, k4), 0.0).astype(BF16)
            v4 = jnp.concatenate([jnp.where(vlane_head == h, vu, jnp.zeros_like(vu)) for h in range(heads)], axis=0)
            o_intra = _dot(att, v4)
        if dk % LANE == 0:
            inc = [_dot_tn(k_out[:, h * dk:(h + 1) * dk], vu[:, h * LANE:(h + 1) * LANE]) for h in range(heads)]
        else:
            full = _dot_tn(k_out, vu)
            inc = [full[h * dk:(h + 1) * dk, h * LANE:(h + 1) * LANE] for h in range(heads)]
        dec = jnp.transpose(jnp.broadcast_to(jnp.exp(gt), (LANE, kt)))
        staged.append((q_in, o_intra, inc, [dec[h * dk:(h + 1) * dk] for h in range(heads)]))
    return staged


def _block_diag(blocks):
    z = jnp.zeros_like(blocks[0])
    return jnp.concatenate([jnp.concatenate([b if c == h else z for c in range(len(blocks))], axis=1)
                            for h, b in enumerate(blocks)], axis=0)


def _scan_states(s_ref, base, staged):
    heads = len(staged[0][2])
    state = [s_ref[base + h] for h in range(heads)]
    outs = []
    for q_in, o_intra, inc, decay in staged:
        if q_in is not None:
            outs.append(o_intra + _dot(q_in, _block_diag([s.astype(BF16) for s in state])))
        state = [s * d + i for s, d, i in zip(state, decay, inc)]
    for h in range(heads):
        s_ref[base + h] = state[h]
    return outs


def _gla_kernel(qk_ref, v_ref, g_ref, cqk_ref, cv_ref, cg_ref, o_ref, s_ref, ob_ref, *, seq, ctx_len):
    heads, dk = GLA_HEADS, GLA_DK
    kt = heads * dk
    s_ref[...] = jnp.zeros_like(s_ref)

    def make_run(qk, vv, gg, n_chunks, need_out):
        unroll = _scan_unroll(n_chunks)
        rows = unroll * CHUNK
        consts = [_scan_consts(rev, unroll, heads, dk) for rev in (False, True)]

        def run(n):
            work = []
            for d, rev in enumerate((False, True)):
                start = (n_chunks * CHUNK - rows - n * rows) if rev else n * rows
                sl = pl.ds(pl.multiple_of(start, rows), rows)
                q = qk[sl, 0:kt] * (dk ** -0.5) if need_out else None
                staged = _scan_local(q, qk[sl, kt:2 * kt], gg[sl, d * kt:(d + 1) * kt], vv[sl, :].astype(BF16),
                                     consts[d], rev=rev, heads=heads, dk=dk, need_out=need_out)
                work.append((d, rev, sl, staged[::-1] if rev else staged))
            for d, rev, sl, staged in work:
                outs = _scan_states(s_ref, d * heads, staged)
                if need_out:
                    (ob_ref if rev else o_ref)[sl, :] = jnp.concatenate(outs[::-1] if rev else outs, axis=0)

        return run, n_chunks // unroll

    run, steps = make_run(cqk_ref, cv_ref, cg_ref, ctx_len // CHUNK, False)
    for n in range(steps):
        run(n)
    run, steps = make_run(qk_ref, v_ref, g_ref, seq // CHUNK, True)

    def body(n, carry):
        run(n)
        return carry

    lax.fori_loop(0, steps, body, 0)
    o_ref[...] = o_ref[...] + ob_ref[...]


def _gla(p3, pc3):
    nb, seq, _ = p3.shape
    ctx_len = pc3.shape[1]
    blk = lambda length, j: pl.BlockSpec((None, length, D_MIX), lambda b: (b, 0, j))
    return pl.pallas_call(
        functools.partial(_gla_kernel, seq=seq, ctx_len=ctx_len),
        grid=(nb,),
        in_specs=[blk(seq, PM_QK), blk(seq, PM_V), blk(seq, PM_GATE),
                  blk(ctx_len, P_QK), blk(ctx_len, P_V), blk(ctx_len, P_GATE)],
        out_specs=pl.BlockSpec((None, seq, D_MIX), lambda b: (b, 0, 0)),
        out_shape=jax.ShapeDtypeStruct((nb, seq, D_MIX), F32),
        scratch_shapes=[pltpu.VMEM((2 * GLA_HEADS, GLA_DK, GLA_DV), F32),
                        pltpu.VMEM((seq, D_MIX), F32)],
        compiler_params=_params("parallel"),
        name="gla_scan",
    )(p3, p3, p3, pc3, pc3, pc3)


def _forget(lb, logits):
    f = lb + (1.0 - lb) * _sigmoid(logits)
    return 1.0 - f, jnp.log(f)


def _hgrn_kernel(q_ref, ffl_ref, fbl_ref, v_ref, cffl_ref, cfbl_ref, cv_ref, lb_ref, o_ref, s_ref, ob_ref, *,
                 seq, ctx_len, layer):
    heads, dk = HG_HEADS, HG_D
    logits = lb_ref[...]
    e = jnp.exp(logits - jnp.max(logits, axis=0, keepdims=True))
    sm = e / jnp.sum(e, axis=0, keepdims=True)
    lb = jnp.sum(sm[0:layer + 1, :], axis=0, keepdims=True) - sm[0:1, :]
    s_ref[...] = jnp.zeros_like(s_ref)

    def make_run(qq, gates, vv, n_chunks, need_out):
        unroll = _scan_unroll(n_chunks)
        rows = unroll * CHUNK
        consts = [_scan_consts(rev, unroll, heads, dk) for rev in (False, True)]

        def run(n):
            work = []
            for d, rev in enumerate((False, True)):
                start = (n_chunks * CHUNK - rows - n * rows) if rev else n * rows
                sl = pl.ds(pl.multiple_of(start, rows), rows)
                k, g = _forget(lb, gates[d][sl, :])
                staged = _scan_local(qq[sl, :] if need_out else None, k, g, vv[sl, :].astype(BF16), consts[d],
                                     rev=rev, heads=heads, dk=dk, need_out=need_out)
                work.append((d, rev, sl, staged[::-1] if rev else staged))
            for d, rev, sl, staged in work:
                outs = _scan_states(s_ref, d * heads, staged)
                if need_out:
                    (ob_ref if rev else o_ref)[sl, :] = jnp.concatenate(outs[::-1] if rev else outs, axis=0)

        return run, n_chunks // unroll

    run, steps = make_run(None, (cffl_ref, cfbl_ref), cv_ref, ctx_len // CHUNK, False)
    for n in range(steps):
        run(n)
    run, steps = make_run(q_ref, (ffl_ref, fbl_ref), v_ref, seq // CHUNK, True)

    def body(n, carry):
        run(n)
        return carry

    lax.fori_loop(0, steps, body, 0)
    o_ref[...] = o_ref[...] + ob_ref[...]


def _hgrn(ph3, pc3, hg_lb, layer):
    nb, seq, _ = ph3.shape
    ctx_len = pc3.shape[1]
    blk = lambda length, j: pl.BlockSpec((None, length, D_MIX), lambda b: (b, 0, j))
    return pl.pallas_call(
        functools.partial(_hgrn_kernel, seq=seq, ctx_len=ctx_len, layer=layer),
        grid=(nb,),
        in_specs=[blk(seq, PH_QD), blk(seq, PH_FFL), blk(seq, PH_FBL), blk(seq, PH_IDD),
                  blk(ctx_len, P_FFL), blk(ctx_len, P_FBL), blk(ctx_len, P_IDD), _const_spec(hg_lb.shape)],
        out_specs=pl.BlockSpec((None, seq, D_MIX), lambda b: (b, 0, 0)),
        out_shape=jax.ShapeDtypeStruct((nb, seq, D_MIX), F32),
        scratch_shapes=[pltpu.VMEM((2 * HG_HEADS, HG_D, HG_D), F32),
                        pltpu.VMEM((seq, D_MIX), F32)],
        compiler_params=_params("parallel"),
        name="hgrn_scan",
    )(ph3, ph3, ph3, ph3, pc3, pc3, pc3, hg_lb)


def _dft_tables(n, period):
    split = 64

    def part(rows, mult):
        r = lax.broadcasted_iota(jnp.int32, (rows, n), 0) * mult
        c = lax.broadcasted_iota(jnp.int32, (rows, n), 1)
        ang = ((r * c) % period).astype(F32) * (2.0 * math.pi / period)
        return jnp.cos(ang), jnp.sin(ang)

    ca, sa = (t[:, None, :] for t in part(n // split, split))
    cb, sb = (t[None, :, :] for t in part(split, 1))
    return (ca * cb - sa * sb).reshape(n, n), (sa * cb + ca * sb).reshape(n, n)


def _fourier_tables(seq):
    c, s = _dft_tables(seq, seq)
    cs = jnp.concatenate([c, -s], axis=1).astype(BF16)
    gsz = D_FOURIER // FOURIER_GROUPS
    cg, sg = _dft_tables(gsz, gsz)
    eye = jnp.eye(FOURIER_GROUPS, dtype=F32)
    return cs, jnp.concatenate([jnp.kron(eye, cg), jnp.kron(eye, sg)], axis=1).astype(BF16)


def _hyena_tables(seq):
    c, s = _dft_tables(seq, 2 * seq)
    return c.astype(BF16), s.astype(BF16)


def _even_mixer(x2, nb, seq, mod, is_ctx, w_in, w_out, conv_w, conv_b, filt, skip, ln_g, ln_b):
    p3 = _proj(x2, mod, w_in, seq=seq, is_ctx=is_ctx).reshape(nb, seq, -1)
    cs, cg = _fourier_tables(seq)
    ya = _fourier(p3, cg, cs, tl=min(seq, 512))
    c_tab, s_tab = _hyena_tables(seq)
    fwd, bwd = _hyena_filters(seq, *filt)
    a, bz, kn = _hyena_spectra(c_tab, s_tab, fwd, bwd)
    cb = 256
    u0 = D_FOURIER // cb
    per = D_HYENA // cb
    y1 = _hyena_order(p3, u0, p3, u0 + per, c_tab, s_tab, a, bz, kn, skip, conv_w, conv_b, order=0, first=True, cb=cb)
    y2 = _hyena_order(y1, 0, p3, u0 + 2 * per, c_tab, s_tab, a, bz, kn, skip, conv_w, conv_b, order=1, first=False,
                      cb=cb)
    return _outproj(x2, ya.reshape(nb * seq, D_MIX), y2.reshape(nb * seq, D_MIX), mod, w_out, ln_g, ln_b,
                    seq=seq, is_ctx=is_ctx)


def _odd_proj_weights(w_in, a_up, a_b):
    kq = GLA_HEADS * GLA_DK
    low0 = 2 * kq + 2 * D_MIX
    low1 = low0 + 2 * GLA_RANK
    w = jnp.concatenate([w_in[:, :low0], w_in[:, low1:], w_in[:, low0:low1],
                         jnp.zeros((D_MODEL, LANE - 2 * GLA_RANK), w_in.dtype)], axis=1).astype(BF16)
    up = jnp.zeros((LANE, 2 * kq), F32)
    up = up.at[0:GLA_RANK, 0:kq].set(a_up[0]).at[GLA_RANK:2 * GLA_RANK, kq:].set(a_up[1]).astype(BF16)
    return w, up, a_b.reshape(1, 2 * kq)


def kernel(x, c, ctx, c_ctx, mod_w, mod_b, ffn_w_in, ffn_w_out, ln_g, ln_b, ev_w_in, ev_w_out, hy_conv_w, hy_conv_b, hy_w1, hy_b1, hy_w2, hy_b2, hy_w3, hy_freq, hy_skip, od_w_in, od_w_out, gla_a_up, gla_a_b, gla_norm_g, hg_lb, hg_norm_g):
    nb, seq, d = x.shape
    ctx_len = ctx.shape[1]
    assert d == D_MODEL and seq % 512 == 0 and ctx_len % CHUNK == 0 and nb + 1 <= MOD_ROWS - 7
    assert mod_w.shape[0] == DEPTH == 2

    c_all = jnp.zeros((MOD_ROWS, D_MODEL), F32).at[:nb].set(c).at[MOD_ROWS - 8].set(c_ctx)
    mod = _mod_table(c_all, mod_w, mod_b)
    w_in_b = ffn_w_in.astype(BF16)
    w_out_b = ffn_w_out.astype(BF16)

    xs = x.reshape(nb * seq, D_MODEL)
    xc = ctx.reshape(nb * ctx_len, D_MODEL)
    streams = [(xs, seq, False), (xc, ctx_len, True)]

    def ffn(stream, layer, half):
        arr, length, is_ctx = stream
        return (_ffn(arr, mod[layer], 6 * half, w_in_b[layer, half], w_out_b[layer, half],
                     ln_g[layer, 2 * half], ln_b[layer, 2 * half], seq=length, is_ctx=is_ctx), length, is_ctx)

    streams = [ffn(s, 0, 0) for s in streams]
    filt = (hy_w1[0], hy_b1[0], hy_w2[0], hy_b2[0], hy_w3[0], hy_freq[0])
    ev_in_b, ev_out_b = ev_w_in[0].astype(BF16), ev_w_out[0].astype(BF16)
    streams = [(_even_mixer(arr, nb, length, mod[0], is_ctx, ev_in_b, ev_out_b, hy_conv_w[0], hy_conv_b[0], filt,
                            hy_skip[0], ln_g[0, 1], ln_b[0, 1]), length, is_ctx) for arr, length, is_ctx in streams]
    streams = [ffn(s, 0, 1) for s in streams]

    streams = [ffn(s, 1, 0) for s in streams]
    (xs, _, _), (xc, _, _) = streams
    w_proj, up, up_b = _odd_proj_weights(od_w_in[0], gla_a_up[0], gla_a_b[0])
    n_gla, n_hg = (P_R + 1) * D_MIX, (PH_GD + 1) * D_MIX
    w_main = jnp.concatenate([w_proj[:, :n_gla], w_proj[:, n_gla + n_hg:]], axis=1)
    pc = _proj(xc, mod[1], w_proj, seq=ctx_len, is_ctx=True, gates=(up, up_b))
    pm = _proj(xs, mod[1], w_main, seq=seq, is_ctx=False, gates=(up, up_b))
    ph = _proj_colmajor(xs, nb, seq, mod[1], w_proj[:, n_gla:n_gla + n_hg])
    pc3 = pc.reshape(nb, ctx_len, -1)
    o_gla = _gla(pm.reshape(nb, seq, -1), pc3).reshape(nb * seq, D_MIX)
    o_hg = _hgrn(ph.reshape(nb, seq, -1), pc3, hg_lb, 1).reshape(ph.shape[:3] + (D_MIX,))
    xs = _outproj(xs, o_gla, o_hg, mod[1], od_w_out[0].astype(BF16), ln_g[1, 1], ln_b[1, 1], seq=seq, is_ctx=False,
                  readout=(pm, ph, gla_norm_g[0], hg_norm_g[0]))
    xs, _, _ = ffn((xs, seq, False), 1, 1)
    return xs.reshape(nb, seq, D_MODEL)
```

```python
import functools
import math

import jax
import jax.numpy as jnp
from jax import lax
from jax.experimental import pallas as pl
from jax.experimental.pallas import tpu as pltpu

F32 = jnp.float32
BF16 = jnp.bfloat16

D_MODEL = 1024
DEPTH = 2
N_MOD = 9
D_FF = 2816
LN_EPS = 1e-6
ALPHA = (2.0 * DEPTH) ** 0.25
GRID_W = 64
D_FOURIER = 512
FOURIER_GROUPS = 8
D_HYENA = 512
HY_EMB = 33
HY_HID = 64
HY_FAST_DECAY = 0.3
HY_SLOW_DECAY = 1.5
HY_TARGET = 1e-2
CHUNK = 64
GLA_HEADS = 4
GLA_DK = 64
GLA_DV = 128
GLA_RANK = 16
GLA_TAU = 16.0
HG_HEADS = 4
HG_D = 128
D_MIX = 512

LANE = 128
MOD_ROWS = 24
VMEM_LIMIT = 56 * 1024 * 1024

P_QK, P_V, P_R, P_QD, P_FFL, P_FBL, P_IDD, P_GD, P_GATE = range(9)
PM_QK, PM_V, PM_R, PM_GATE = range(4)
PH_QD, PH_FFL, PH_FBL, PH_IDD, PH_GD = range(5)


def _params(*sem):
    return pltpu.CompilerParams(dimension_semantics=sem, vmem_limit_bytes=VMEM_LIMIT)


def _dot(a, b):
    return jnp.dot(a, b, preferred_element_type=F32)


def _dot_nt(a, b):
    return lax.dot_general(a, b, (((1,), (1,)), ((), ())), preferred_element_type=F32)


def _dot_tn(a, b):
    return lax.dot_general(a, b, (((0,), (0,)), ((), ())), preferred_element_type=F32)


def _ln(x):
    mu = jnp.mean(x, axis=-1, keepdims=True)
    xc = x - mu
    var = jnp.mean(xc * xc, axis=-1, keepdims=True)
    return xc * lax.rsqrt(var + LN_EPS)


def _sigmoid(x):
    return 1.0 / (1.0 + jnp.exp(-x))


def _mod_row(mod_ref, row, k):
    return mod_ref[pl.ds(row, 1), k * D_MODEL:(k + 1) * D_MODEL]


def _mod_row_index(tile, tm, seq, is_ctx):
    if is_ctx:
        return MOD_ROWS - 8
    return (tile * tm) // seq


def _const_spec(shape):
    return pl.BlockSpec(shape, lambda *_: (0,) * len(shape), pipeline_mode=pl.Buffered(1))


def _picked_spec(shape, lead):
    tail = tuple(shape[len(lead):])
    return pl.BlockSpec((None,) * len(lead) + tail, lambda *_: tuple(lead) + (0,) * len(tail),
                        pipeline_mode=pl.Buffered(1))


TOKEN_TILE = 1024
SUB_TILES = 2


def _token_tile(n, seq, is_ctx, want=TOKEN_TILE):
    tm = min(want, n if is_ctx else seq)
    while n % tm or (not is_ctx and seq % tm):
        tm //= 2
    return tm


def _mod_kernel(c_ref, w_ref, b_ref, o_ref):
    c = c_ref[...]
    s = (c * _sigmoid(c)).astype(BF16)
    o_ref[...] = _dot(s, w_ref[...].astype(BF16)) + b_ref[...]


def _mod_table(c_all, mod_w, mod_b):
    return pl.pallas_call(
        _mod_kernel,
        grid=(DEPTH, N_MOD),
        in_specs=[
            _const_spec((MOD_ROWS, D_MODEL)),
            pl.BlockSpec((None, D_MODEL, D_MODEL), lambda l, j: (l, 0, j)),
            pl.BlockSpec((None, 1, D_MODEL), lambda l, j: (l, 0, j)),
        ],
        out_specs=pl.BlockSpec((None, MOD_ROWS, D_MODEL), lambda l, j: (l, 0, j)),
        out_shape=jax.ShapeDtypeStruct((DEPTH, MOD_ROWS, N_MOD * D_MODEL), F32),
        compiler_params=_params("parallel", "parallel"),
        name="mod_table",
    )(c_all, mod_w, mod_b.reshape(DEPTH, 1, N_MOD * D_MODEL))


FFN_CHUNK = 256


def _swiglu(xm, win_ref, wout_ref, side_work=()):
    side_work = list(side_work)
    acc = None
    for j in range(D_FF // FFN_CHUNK):
        lo = j * FFN_CHUNK
        a = _dot(xm, win_ref[:, lo:lo + FFN_CHUNK])
        u = _dot(xm, win_ref[:, D_FF + lo:D_FF + lo + FFN_CHUNK])
        h = (a * _sigmoid(a) * u).astype(BF16)
        y = _dot(h, wout_ref[lo:lo + FFN_CHUNK, :])
        acc = y if acc is None else acc + y
        if side_work:
            side_work.pop(0)()
    for work in side_work:
        work()
    return acc


FFN_SLABS = 8


def _ffn_kernel(x_ref, mod_ref, win_ref, wout_ref, g_ref, b_ref, o_ref, *, k0, tm, seq, is_ctx):
    row = _mod_row_index(pl.program_id(0), tm, seq, is_ctx)
    shift, scale, gate = (_mod_row(mod_ref, row, k0 + t) for t in range(3))
    half = tm // 2
    slab = half // FFN_SLABS

    def modulated(r0, n):
        return (_ln(x_ref[r0:r0 + n, :]) * (1.0 + scale) + shift).astype(BF16)

    def finish(r0, n, acc):
        z = ALPHA * x_ref[r0:r0 + n, :] + (0.5 * gate) * acc
        o_ref[r0:r0 + n, :] = _ln(z) * g_ref[...] + b_ref[...]

    xm_b = []
    acc_a = _swiglu(modulated(0, half), win_ref, wout_ref,
                    [functools.partial(lambda s: xm_b.append(modulated(half + s * slab, slab)), s)
                     for s in range(FFN_SLABS)])
    acc_b = _swiglu(jnp.concatenate(xm_b, axis=0), win_ref, wout_ref,
                    [functools.partial(lambda s: finish(s * slab, slab, acc_a[s * slab:(s + 1) * slab]), s)
                     for s in range(FFN_SLABS)])
    finish(half, half, acc_b)


def _ffn(x2, mod, layer, half, w_in, w_out, ln_g, ln_b, *, seq, is_ctx):
    n = x2.shape[0]
    tm = _token_tile(n, seq, is_ctx)
    kern = functools.partial(_ffn_kernel, k0=6 * half, tm=tm, seq=seq, is_ctx=is_ctx)
    return pl.pallas_call(
        kern,
        grid=(n // tm,),
        in_specs=[
            pl.BlockSpec((tm, D_MODEL), lambda i: (i, 0)),
            _picked_spec(mod.shape, (layer,)),
            _picked_spec(w_in.shape, (layer, half)),
            _picked_spec(w_out.shape, (layer, half)),
            _picked_spec(ln_g.shape, (layer, 2 * half)),
            _picked_spec(ln_b.shape, (layer, 2 * half)),
        ],
        out_specs=pl.BlockSpec((tm, D_MODEL), lambda i: (i, 0)),
        out_shape=jax.ShapeDtypeStruct((n, D_MODEL), F32),
        compiler_params=_params("parallel"),
        name="ffn",
    )(x2, mod, w_in, w_out, ln_g, ln_b)


def _log_sigmoid(x):
    return jnp.minimum(x, 0.0) - jnp.log(1.0 + jnp.exp(-jnp.abs(x)))


def _proj_kernel(x_ref, mod_ref, w_ref, *rest, tm, seq, is_ctx, gla_gates):
    row = _mod_row_index(pl.program_id(0), tm, seq, is_ctx)
    shift, scale = _mod_row(mod_ref, row, 3), _mod_row(mod_ref, row, 4)
    subs = [slice(r, r + tm // SUB_TILES) for r in range(0, tm, tm // SUB_TILES)]
    xms = [(_ln(x_ref[s, :]) * (1.0 + scale) + shift).astype(BF16) for s in subs]
    ps = [_dot(xm, w_ref[...]) for xm in xms]
    if not gla_gates:
        (o_ref,) = rest
        for s, p in zip(subs, ps):
            o_ref[s, :] = p
        return
    aup_ref, ab_ref, o_ref = rest
    for s, p in zip(subs, ps):
        main = p.shape[1] - LANE
        o_ref[s, :main] = p[:, :main]
        low = p[:, main:].astype(BF16)
        o_ref[s, main:] = _log_sigmoid(_dot(low, aup_ref[...]) + ab_ref[...]) * (1.0 / GLA_TAU)


PERM_COLS = 8


def _proj_colmajor_kernel(x_ref, mod_ref, perm_ref, w_ref, o_ref):
    rows, cols, _ = x_ref.shape
    row = pl.program_id(0)
    shift, scale = _mod_row(mod_ref, row, 3), _mod_row(mod_ref, row, 4)
    per_sub = cols // SUB_TILES
    for s in range(SUB_TILES):
        parts = []
        for c in range(s * per_sub, (s + 1) * per_sub, PERM_COLS):
            x = x_ref[:, c:c + PERM_COLS, :].reshape(rows * PERM_COLS, D_MODEL)
            xm = (_ln(x) * (1.0 + scale) + shift).astype(BF16)
            parts.append(_dot(perm_ref[...], xm).astype(BF16))
        p = _dot(jnp.concatenate(parts, axis=0), w_ref[...])
        o_ref[s * per_sub:(s + 1) * per_sub] = p.reshape(per_sub, rows, w_ref.shape[1])


def _proj_colmajor(x2, nb, seq, mod, layer, w, *, cols=32):
    rows = seq // GRID_W
    n_out = w.shape[1]
    n = rows * PERM_COLS
    i = lax.broadcasted_iota(jnp.int32, (n, n), 0)
    j = lax.broadcasted_iota(jnp.int32, (n, n), 1)
    perm = jnp.where(j == (i % rows) * PERM_COLS + i // rows, 1.0, 0.0).astype(BF16)
    return pl.pallas_call(
        _proj_colmajor_kernel,
        grid=(nb, GRID_W // cols),
        in_specs=[pl.BlockSpec((None, rows, cols, D_MODEL), lambda b, c: (b, 0, c, 0)),
                  _picked_spec(mod.shape, (layer,)), _const_spec(perm.shape), _const_spec(w.shape)],
        out_specs=pl.BlockSpec((None, cols, rows, n_out), lambda b, c: (b, c, 0, 0)),
        out_shape=jax.ShapeDtypeStruct((nb, GRID_W, rows, n_out), F32),
        compiler_params=_params("parallel", "parallel"),
        name="mixer_proj_colmajor",
    )(x2.reshape(nb, rows, GRID_W, D_MODEL), mod, perm, w)


def _proj(x2, mod, layer, w, *, seq, is_ctx, gates=None, want=TOKEN_TILE):
    n = x2.shape[0]
    tm = _token_tile(n, seq, is_ctx, want)
    n_out = w.shape[1] - LANE + gates[0].shape[1] if gates is not None else w.shape[1]
    kern = functools.partial(_proj_kernel, tm=tm, seq=seq, is_ctx=is_ctx, gla_gates=gates is not None)
    extra = [] if gates is None else list(gates)
    return pl.pallas_call(
        kern,
        grid=(n // tm,),
        in_specs=[pl.BlockSpec((tm, D_MODEL), lambda i: (i, 0)), _picked_spec(mod.shape, (layer,)),
                  _const_spec(w.shape)] + [_const_spec(e.shape) for e in extra],
        out_specs=pl.BlockSpec((tm, n_out), lambda i: (i, 0)),
        out_shape=jax.ShapeDtypeStruct((n, n_out), F32),
        compiler_params=_params("parallel"),
        name="mixer_proj",
    )(x2, mod, w, *extra)


def _fourier_kernel(a_ref, cg_ref, cs_ref, o_ref, stack_ref, *, seq):
    @pl.when(pl.program_id(1) == 0)
    def _():
        p = _dot(a_ref[...].astype(BF16), cg_ref[...])
        stack_ref[0:seq, :] = p[:, :D_MIX].astype(BF16)
        stack_ref[seq:2 * seq, :] = p[:, D_MIX:].astype(BF16)

    scale = 1.0 / math.sqrt(seq * (D_FOURIER // FOURIER_GROUPS))
    o_ref[...] = _dot(cs_ref[...], stack_ref[...]) * scale


def _fourier(p3, cg, cs, *, tl):
    nb, seq, _ = p3.shape
    return pl.pallas_call(
        functools.partial(_fourier_kernel, seq=seq),
        grid=(nb, seq // tl),
        in_specs=[
            pl.BlockSpec((None, seq, D_MIX), lambda b, j: (b, 0, 0)),
            _const_spec(cg.shape),
            pl.BlockSpec((tl, 2 * seq), lambda b, j: (j, 0)),
        ],
        out_specs=pl.BlockSpec((None, tl, D_MIX), lambda b, j: (b, j, 0)),
        out_shape=jax.ShapeDtypeStruct((nb, seq, D_MIX), F32),
        scratch_shapes=[pltpu.VMEM((2 * seq, D_MIX), BF16)],
        compiler_params=_params("parallel", "arbitrary"),
        name="fourier_mix",
    )(p3, cg, cs)


def _hyena_filter_kernel(z_ref, w1_ref, b1_ref, w2_ref, b2_ref, w3f_ref, w3b_ref, freq_ref, delta_ref,
                         fwd_ref, bwd_ref):
    hi = lax.Precision.HIGHEST
    z = z_ref[...]
    freq = freq_ref[...]
    hdn = jnp.sin(freq * (jnp.dot(z, w1_ref[...], precision=hi, preferred_element_type=F32) + b1_ref[...]))
    hdn = jnp.sin(freq * (jnp.dot(hdn, w2_ref[...], precision=hi, preferred_element_type=F32) + b2_ref[...]))
    decay = jnp.exp(-z[:, 0:1] * delta_ref[...])
    fwd = jnp.dot(hdn, w3f_ref[...], precision=hi, preferred_element_type=F32) * decay
    bwd = jnp.dot(hdn, w3b_ref[...], precision=hi, preferred_element_type=F32) * decay
    row = lax.broadcasted_iota(jnp.int32, bwd.shape, 0)
    bwd = jnp.where(row == 0, 0.0, bwd)
    norm = jnp.sum(jnp.abs(fwd), axis=0, keepdims=True) + jnp.sum(jnp.abs(bwd), axis=0, keepdims=True) + 1e-6
    inv = 1.0 / norm
    fwd_ref[...] = fwd * inv
    bwd_ref[...] = bwd * inv


def _hyena_filters(seq, w1, b1, w2, b2, w3, freq):
    t = jnp.linspace(0.0, 1.0, seq, dtype=F32)[:, None]
    bands = (HY_EMB - 1) // 2
    fr = jnp.linspace(1e-4, bands - 1, bands, dtype=F32)[None, :]
    idx = jnp.arange(seq, dtype=F32)[:, None]
    w = 2.0 * math.pi * idx * fr / seq
    z = jnp.concatenate([t, jnp.cos(w), -jnp.sin(w), jnp.zeros((seq, LANE - HY_EMB), F32)], axis=-1)
    pad = LANE - HY_HID
    w1p = jnp.pad(w1, ((0, LANE - HY_EMB), (0, pad)))
    w2p = jnp.pad(w2, ((0, pad), (0, pad)))
    w3p = jnp.pad(w3, ((0, pad), (0, 0)))
    row = lambda v: jnp.pad(v, (0, pad)).reshape(1, LANE)
    delta = jnp.abs(jnp.linspace(math.log(HY_TARGET) / HY_SLOW_DECAY, math.log(HY_TARGET) / HY_FAST_DECAY,
                                 D_HYENA, dtype=F32)).reshape(1, D_HYENA)
    sq = lambda: _const_spec((LANE, LANE))
    vec = lambda: _const_spec((1, LANE))
    n_ord = 2
    return pl.pallas_call(
        _hyena_filter_kernel,
        grid=(n_ord,),
        in_specs=[_const_spec((seq, LANE)), sq(), vec(), sq(), vec(),
                  pl.BlockSpec((LANE, D_MIX), lambda o: (0, o)),
                  pl.BlockSpec((LANE, D_MIX), lambda o: (0, n_ord + o)),
                  vec(), _const_spec((1, D_MIX))],
        out_specs=[pl.BlockSpec((seq, D_MIX), lambda o: (0, o))] * 2,
        out_shape=[jax.ShapeDtypeStruct((seq, n_ord * D_MIX), F32)] * 2,
        compiler_params=_params("parallel"),
        name="hyena_filters",
    )(z, w1p, row(b1), w2p, row(b2), w3p, w3p, row(freq), delta)


def _alt_sum(x):
    row = lax.broadcasted_iota(jnp.int32, x.shape, 0)
    return jnp.sum(jnp.where((row & 1) == 1, -x, x), axis=0, keepdims=True)


def _hyena_spec_kernel(c_ref, s_ref, fwd_ref, bwd_ref, a_ref, bz_ref, kn_ref, *, seq):
    n = 2 * seq
    fwd, bwd = fwd_ref[...], bwd_ref[...]
    fb, bb = fwd.astype(BF16), bwd.astype(BF16)
    c, s = c_ref[...], s_ref[...]
    k_re = _dot(c, fb) + _dot(c, bb)
    k_im = _dot(s, bb) - _dot(s, fb)
    row = lax.broadcasted_iota(jnp.int32, k_re.shape, 0)
    wgt = jnp.where(row == 0, 1.0 / n, 2.0 / n)
    a_ref[...] = wgt * k_re
    bz_ref[...] = wgt * k_im
    kn_ref[...] = _alt_sum(fwd + bwd) * (1.0 / n)


def _hyena_spectra(c_tab, s_tab, fwd, bwd, *, cb=256):
    seq, width = fwd.shape
    blk = lambda: pl.BlockSpec((seq, cb), lambda j: (0, j))
    return pl.pallas_call(
        functools.partial(_hyena_spec_kernel, seq=seq),
        grid=(width // cb,),
        in_specs=[_const_spec((seq, seq)), _const_spec((seq, seq)), blk(), blk()],
        out_specs=[blk(), blk(), pl.BlockSpec((1, cb), lambda j: (0, j))],
        out_shape=[jax.ShapeDtypeStruct((seq, width), F32)] * 2 + [jax.ShapeDtypeStruct((1, width), F32)],
        compiler_params=_params("parallel"),
        name="hyena_spectra",
    )(c_tab, s_tab, fwd, bwd)


def _short_conv(u, w_ref, b_ref):
    seq = u.shape[0]
    row = lax.broadcasted_iota(jnp.int32, u.shape, 0)
    prev = jnp.where(row == 0, 0.0, pltpu.roll(u, 1, 0))
    nxt = jnp.where(row == seq - 1, 0.0, pltpu.roll(u, seq - 1, 0))
    return prev * w_ref[0:1, :] + u * w_ref[1:2, :] + nxt * w_ref[2:3, :] + b_ref[...]


def _hyena_order_kernel(z_ref, x_ref, c_ref, s_ref, a_ref, bz_ref, kn_ref, skip_ref, wx_ref, bx_ref, *rest,
                        order, first):
    if first:
        wz_ref, bz0_ref, o_ref = rest
        z = _short_conv(z_ref[...], wz_ref, bz0_ref)
    else:
        (o_ref,) = rest
        z = z_ref[...]
    mult = _short_conv(x_ref[...], wx_ref, bx_ref)
    c, s = c_ref[...], s_ref[...]
    zb = z.astype(BF16)
    re = _dot(c, zb)
    im = _dot(s, zb)
    a, bz = a_ref[...], bz_ref[...]
    wr = (re * a + im * bz).astype(BF16)
    wi = (re * bz - im * a).astype(BF16)
    nyq = _alt_sum(z) * kn_ref[...]
    row = lax.broadcasted_iota(jnp.int32, z.shape, 0)
    conv = _dot(c, wr) - _dot(s, wi) + jnp.where((row & 1) == 1, -nyq, nyq)
    o_ref[...] = mult * (conv + z * skip_ref[order:order + 1, :])


def _hyena_order(z3, z_blk0, x3, x_blk0, c_tab, s_tab, a, bz, kn, skip, conv_w, conv_b, *, order, first, cb=256):
    nb, seq, _ = x3.shape
    ncb = D_MIX // cb
    xw0 = (1 + order) * ncb
    conv_b = conv_b.reshape(1, -1)
    tok = lambda blk0: pl.BlockSpec((None, seq, cb), lambda j, b: (b, 0, blk0 + j))
    colblk = lambda rows, blk0: pl.BlockSpec((rows, cb), lambda j, b: (0, blk0 + j), pipeline_mode=pl.Buffered(1))
    in_specs = [tok(z_blk0), tok(x_blk0), _const_spec((seq, seq)), _const_spec((seq, seq)),
                colblk(seq, order * ncb), colblk(seq, order * ncb), colblk(1, order * ncb),
                colblk(2, 0), colblk(3, xw0), colblk(1, xw0)]
    args = [z3, x3, c_tab, s_tab, a, bz, kn, skip, conv_w, conv_b]
    if first:
        in_specs += [colblk(3, 0), colblk(1, 0)]
        args += [conv_w, conv_b]
    return pl.pallas_call(
        functools.partial(_hyena_order_kernel, order=order, first=first),
        grid=(ncb, nb),
        in_specs=in_specs,
        out_specs=pl.BlockSpec((None, seq, cb), lambda j, b: (b, 0, j)),
        out_shape=jax.ShapeDtypeStruct((nb, seq, D_MIX), F32),
        compiler_params=_params("parallel", "parallel"),
        name="hyena_order",
    )(*args)


def _rms_heads(x, g):
    parts = []
    for h in range(D_MIX // LANE):
        xh = x[:, h * LANE:(h + 1) * LANE]
        parts.append(xh * lax.rsqrt(jnp.mean(xh * xh, axis=-1, keepdims=True) + LN_EPS) * g)
    return jnp.concatenate(parts, axis=-1)


def _outproj_kernel(x_ref, ya_ref, yb_ref, *rest, tm, seq, is_ctx, readout):
    if readout:
        r_ref, gd_ref, gg_ref, hg_ref, unperm_ref, mod_ref, w_ref, g_ref, b_ref, o_ref = rest
        r = r_ref[...]
        ya = (_rms_heads(ya_ref[...], gg_ref[...]) * (r * _sigmoid(r))).astype(BF16)
        yb = _rms_heads(yb_ref[...].reshape(tm, D_MIX) * _sigmoid(gd_ref[...].reshape(tm, D_MIX)), hg_ref[...])
        yb = _dot(unperm_ref[...], yb.astype(BF16)).astype(BF16)
    else:
        mod_ref, w_ref, g_ref, b_ref, o_ref = rest
        ya, yb = ya_ref[...].astype(BF16), yb_ref[...].astype(BF16)
    row = _mod_row_index(pl.program_id(0), tm, seq, is_ctx)
    gate = _mod_row(mod_ref, row, 5)
    y = _dot(ya, w_ref[:D_MIX, :]) + _dot(yb, w_ref[D_MIX:, :])
    z = ALPHA * x_ref[...] + gate * y
    o_ref[...] = _ln(z) * g_ref[...] + b_ref[...]


def _outproj(x2, ya, yb, mod, layer, w, ln_g, ln_b, *, seq, is_ctx, readout=None):
    n = x2.shape[0]
    tm = _token_tile(n, seq, is_ctx, want=512)
    tok = lambda: pl.BlockSpec((tm, D_MIX), lambda i: (i, 0))
    in_specs = [pl.BlockSpec((tm, D_MODEL), lambda i: (i, 0)), tok()]
    args = [x2, ya, yb]
    if readout is None:
        in_specs += [tok()]
    else:
        p_main, p_hg, gla_g, hg_g = readout
        rows = tm // GRID_W
        tiles = seq // tm
        assert tm % GRID_W == 0 and rows % 8 == 0
        i = lax.broadcasted_iota(jnp.int32, (tm, tm), 0)
        j = lax.broadcasted_iota(jnp.int32, (tm, tm), 1)
        unperm = jnp.where(j == (i % GRID_W) * rows + i // GRID_W, 1.0, 0.0).astype(BF16)
        colmajor = lambda blk: pl.BlockSpec((None, GRID_W, rows, D_MIX), lambda i: (i // tiles, 0, i % tiles, blk))
        in_specs += [colmajor(0), pl.BlockSpec((tm, D_MIX), lambda i: (i, PM_R)), colmajor(PH_GD),
                     _const_spec((1, LANE)), _const_spec((1, LANE)), _const_spec((tm, tm))]
        args += [p_main, p_hg, gla_g.reshape(1, LANE), hg_g.reshape(1, LANE), unperm]
    in_specs += [_picked_spec(mod.shape, (layer,)), _const_spec(w.shape),
                 _picked_spec(ln_g.shape, (layer, 1)), _picked_spec(ln_b.shape, (layer, 1))]
    args += [mod, w, ln_g, ln_b]
    kern = functools.partial(_outproj_kernel, tm=tm, seq=seq, is_ctx=is_ctx, readout=readout is not None)
    return pl.pallas_call(
        kern,
        grid=(n // tm,),
        in_specs=in_specs,
        out_specs=pl.BlockSpec((tm, D_MODEL), lambda i: (i, 0)),
        out_shape=jax.ShapeDtypeStruct((n, D_MODEL), F32),
        compiler_params=_params("parallel"),
        name="mixer_out",
    )(*args)


def _split3(x):
    hi = x.astype(BF16)
    r1 = x - hi.astype(F32)
    mid = r1.astype(BF16)
    lo = (r1 - mid.astype(F32)).astype(BF16)
    return hi, mid, lo


SCAN_UNROLL = 8
CUMSUM_CHUNKS = 4


def _scan_unroll(n_chunks):
    return math.gcd(SCAN_UNROLL, n_chunks)


def _scan_consts(rev, unroll, heads, dk):
    n = math.gcd(unroll, CUMSUM_CHUNKS) * CHUNK
    i = lax.broadcasted_iota(jnp.int32, (n, n), 0)
    j = lax.broadcasted_iota(jnp.int32, (n, n), 1)
    same_chunk = (i // CHUNK) == (j // CHUNK)
    tri = jnp.where(same_chunk & ((j >= i) if rev else (j <= i)), 1.0, 0.0).astype(BF16)
    r = lax.broadcasted_iota(jnp.int32, (CHUNK, heads * CHUNK), 0)
    c = lax.broadcasted_iota(jnp.int32, (CHUNK, heads * CHUNK), 1) & (CHUNK - 1)
    causal = (c >= r) if rev else (c <= r)
    klane_head = lax.broadcasted_iota(jnp.int32, (1, heads * dk), 1) // dk
    vlane_head = lax.broadcasted_iota(jnp.int32, (1, heads * LANE), 1) // LANE
    return tri, causal, klane_head, vlane_head


def _scan_local(q, k, g, vb, consts, *, rev, heads, dk, need_out):
    tri, causal, klane_head, vlane_head = consts
    kt = heads * dk
    parts = _split3(g)
    span = tri.shape[0]
    gsum = jnp.concatenate([sum(_dot(tri, part[r:r + span]) for part in parts) for r in range(0, g.shape[0], span)],
                           axis=0)
    staged = []
    for u in range(g.shape[0] // CHUNK):
        rows = slice(u * CHUNK, (u + 1) * CHUNK)
        gs = gsum[rows]
        gt = gs[0:1] if rev else gs[CHUNK - 1:CHUNK]
        ku, vu = k[rows], vb[rows]
        k_out = (ku * jnp.exp(gt - gs)).astype(BF16)
        q_in = o_intra = None
        if need_out:
            q_in = (q[rows] * jnp.exp(gs)).astype(BF16)
            k_in = (ku * jnp.exp(-gs)).astype(BF16)
            k4 = jnp.concatenate([jnp.where(klane_head == h, k_in, jnp.zeros_like(k_in)) for h in range(heads)], axis=0)
            att = jnp.where(causal, _dot_nt(q_in, k4), 0.0).astype(BF16)
            v4 = jnp.concatenate([jnp.where(vlane_head == h, vu, jnp.zeros_like(vu)) for h in range(heads)], axis=0)
            o_intra = _dot(att, v4)
        if dk % LANE == 0:
            inc = [_dot_tn(k_out[:, h * dk:(h + 1) * dk], vu[:, h * LANE:(h + 1) * LANE]) for h in range(heads)]
        else:
            full = _dot_tn(k_out, vu)
            inc = [full[h * dk:(h + 1) * dk, h * LANE:(h + 1) * LANE] for h in range(heads)]
        dec = jnp.transpose(jnp.broadcast_to(jnp.exp(gt), (LANE, kt)))
        staged.append((q_in, o_intra, inc, [dec[h * dk:(h + 1) * dk] for h in range(heads)]))
    return staged


def _block_diag(blocks):
    z = jnp.zeros_like(blocks[0])
    return jnp.concatenate([jnp.concatenate([b if c == h else z for c in range(len(blocks))], axis=1)
                            for h, b in enumerate(blocks)], axis=0)


def _scan_states(s_ref, base, staged):
    heads = len(staged[0][2])
    state = [s_ref[base + h] for h in range(heads)]
    outs = []
    for q_in, o_intra, inc, decay in staged:
        if q_in is not None:
            outs.append(o_intra + _dot(q_in, _block_diag([s.astype(BF16) for s in state])))
        state = [s * d + i for s, d, i in zip(state, decay, inc)]
    for h in range(heads):
        s_ref[base + h] = state[h]
    return outs


def _gla_kernel(qk_ref, v_ref, g_ref, cqk_ref, cv_ref, cg_ref, o_ref, s_ref, ob_ref, *, seq, ctx_len):
    heads, dk = GLA_HEADS, GLA_DK
    kt = heads * dk
    s_ref[...] = jnp.zeros_like(s_ref)

    def make_run(qk, vv, gg, n_chunks, need_out):
        unroll = _scan_unroll(n_chunks)
        rows = unroll * CHUNK
        consts = [_scan_consts(rev, unroll, heads, dk) for rev in (False, True)]

        def run(n):
            work = []
            for d, rev in enumerate((False, True)):
                start = (n_chunks * CHUNK - rows - n * rows) if rev else n * rows
                sl = pl.ds(pl.multiple_of(start, rows), rows)
                q = qk[sl, 0:kt] * (dk ** -0.5) if need_out else None
                staged = _scan_local(q, qk[sl, kt:2 * kt], gg[sl, d * kt:(d + 1) * kt], vv[sl, :].astype(BF16),
                                     consts[d], rev=rev, heads=heads, dk=dk, need_out=need_out)
                work.append((d, rev, sl, staged[::-1] if rev else staged))
            for d, rev, sl, staged in work:
                outs = _scan_states(s_ref, d * heads, staged)
                if need_out:
                    (ob_ref if rev else o_ref)[sl, :] = jnp.concatenate(outs[::-1] if rev else outs, axis=0)

        return run, n_chunks // unroll

    run, steps = make_run(cqk_ref, cv_ref, cg_ref, ctx_len // CHUNK, False)
    for n in range(steps):
        run(n)
    run, steps = make_run(qk_ref, v_ref, g_ref, seq // CHUNK, True)

    def body(n, carry):
        run(n)
        return carry

    lax.fori_loop(0, steps, body, 0)
    o_ref[...] = o_ref[...] + ob_ref[...]


def _gla(p3, pc3):
    nb, seq, _ = p3.shape
    ctx_len = pc3.shape[1]
    blk = lambda length, j: pl.BlockSpec((None, length, D_MIX), lambda b: (b, 0, j))
    return pl.pallas_call(
        functools.partial(_gla_kernel, seq=seq, ctx_len=ctx_len),
        grid=(nb,),
        in_specs=[blk(seq, PM_QK), blk(seq, PM_V), blk(seq, PM_GATE),
                  blk(ctx_len, P_QK), blk(ctx_len, P_V), blk(ctx_len, P_GATE)],
        out_specs=pl.BlockSpec((None, seq, D_MIX), lambda b: (b, 0, 0)),
        out_shape=jax.ShapeDtypeStruct((nb, seq, D_MIX), F32),
        scratch_shapes=[pltpu.VMEM((2 * GLA_HEADS, GLA_DK, GLA_DV), F32),
                        pltpu.VMEM((seq, D_MIX), F32)],
        compiler_params=_params("parallel"),
        name="gla_scan",
    )(p3, p3, p3, pc3, pc3, pc3)


def _forget(lb, logits):
    f = lb + (1.0 - lb) * _sigmoid(logits)
    return 1.0 - f, jnp.log(f)


def _hgrn_kernel(q_ref, ffl_ref, fbl_ref, v_ref, cffl_ref, cfbl_ref, cv_ref, lb_ref, o_ref, s_ref, *,
                 seq, ctx_len, layer):
    heads, dk = HG_HEADS, HG_D
    logits = lb_ref[...]
    e = jnp.exp(logits - jnp.max(logits, axis=0, keepdims=True))
    sm = e / jnp.sum(e, axis=0, keepdims=True)
    lb = jnp.sum(sm[0:layer + 1, :], axis=0, keepdims=True) - sm[0:1, :]
    s_ref[...] = jnp.zeros_like(s_ref)

    def make_run(qq, gates, vv, n_chunks, need_out):
        unroll = _scan_unroll(n_chunks)
        rows = unroll * CHUNK
        consts = [_scan_consts(rev, unroll, heads, dk) for rev in (False, True)]

        def run(n):
            work = []
            for d, rev in enumerate((False, True)):
                start = (n_chunks * CHUNK - rows - n * rows) if rev else n * rows
                sl = pl.ds(pl.multiple_of(start, rows), rows)
                k, g = _forget(lb, gates[d][sl, :])
                staged = _scan_local(qq[sl, :] if need_out else None, k, g, vv[sl, :].astype(BF16), consts[d],
                                     rev=rev, heads=heads, dk=dk, need_out=need_out)
                work.append((d, rev, sl, staged[::-1] if rev else staged))
            for d, rev, sl, staged in work:
                outs = _scan_states(s_ref, d * heads, staged)
                if need_out:
                    o_ref[sl, :] += jnp.concatenate(outs[::-1] if rev else outs, axis=0)

        return run, n_chunks // unroll

    run, steps = make_run(None, (cffl_ref, cfbl_ref), cv_ref, ctx_len // CHUNK, False)
    for n in range(steps):
        run(n)
    o_ref[...] = jnp.zeros_like(o_ref)
    run, steps = make_run(q_ref, (ffl_ref, fbl_ref), v_ref, seq // CHUNK, True)

    def body(n, carry):
        run(n)
        return carry

    lax.fori_loop(0, steps, body, 0)


def _hgrn(ph3, pc3, hg_lb, layer):
    nb, seq, _ = ph3.shape
    ctx_len = pc3.shape[1]
    blk = lambda length, j: pl.BlockSpec((None, length, D_MIX), lambda b: (b, 0, j))
    return pl.pallas_call(
        functools.partial(_hgrn_kernel, seq=seq, ctx_len=ctx_len, layer=layer),
        grid=(nb,),
        in_specs=[blk(seq, PH_QD), blk(seq, PH_FFL), blk(seq, PH_FBL), blk(seq, PH_IDD),
                  blk(ctx_len, P_FFL), blk(ctx_len, P_FBL), blk(ctx_len, P_IDD), _const_spec(hg_lb.shape)],
        out_specs=pl.BlockSpec((None, seq, D_MIX), lambda b: (b, 0, 0)),
        out_shape=jax.ShapeDtypeStruct((nb, seq, D_MIX), F32),
        scratch_shapes=[pltpu.VMEM((2 * HG_HEADS, HG_D, HG_D), F32)],
        compiler_params=_params("parallel"),
        name="hgrn_scan",
    )(ph3, ph3, ph3, ph3, pc3, pc3, pc3, hg_lb)


def _dft_tables(n, period):
    split = 64

    def part(rows, mult):
        r = lax.broadcasted_iota(jnp.int32, (rows, n), 0) * mult
        c = lax.broadcasted_iota(jnp.int32, (rows, n), 1)
        ang = ((r * c) % period).astype(F32) * (2.0 * math.pi / period)
        return jnp.cos(ang), jnp.sin(ang)

    ca, sa = (t[:, None, :] for t in part(n // split, split))
    cb, sb = (t[None, :, :] for t in part(split, 1))
    return (ca * cb - sa * sb).reshape(n, n), (sa * cb + ca * sb).reshape(n, n)


def _fourier_tables(seq):
    c, s = _dft_tables(seq, seq)
    cs = jnp.concatenate([c, -s], axis=1).astype(BF16)
    gsz = D_FOURIER // FOURIER_GROUPS
    cg, sg = _dft_tables(gsz, gsz)
    eye = jnp.eye(FOURIER_GROUPS, dtype=F32)
    return cs, jnp.concatenate([jnp.kron(eye, cg), jnp.kron(eye, sg)], axis=1).astype(BF16)


def _hyena_tables(seq):
    c, s = _dft_tables(seq, 2 * seq)
    return c.astype(BF16), s.astype(BF16)


def _even_mixer(x2, nb, seq, mod, layer, is_ctx, w_in, w_out, conv_w, conv_b, filt, skip, ln_g, ln_b):
    p3 = _proj(x2, mod, layer, w_in, seq=seq, is_ctx=is_ctx).reshape(nb, seq, -1)
    cs, cg = _fourier_tables(seq)
    ya = _fourier(p3, cg, cs, tl=min(seq, 512))
    c_tab, s_tab = _hyena_tables(seq)
    fwd, bwd = _hyena_filters(seq, *filt)
    a, bz, kn = _hyena_spectra(c_tab, s_tab, fwd, bwd)
    cb = 256
    u0 = D_FOURIER // cb
    per = D_HYENA // cb
    y1 = _hyena_order(p3, u0, p3, u0 + per, c_tab, s_tab, a, bz, kn, skip, conv_w, conv_b, order=0, first=True, cb=cb)
    y2 = _hyena_order(y1, 0, p3, u0 + 2 * per, c_tab, s_tab, a, bz, kn, skip, conv_w, conv_b, order=1, first=False,
                      cb=cb)
    return _outproj(x2, ya.reshape(nb * seq, D_MIX), y2.reshape(nb * seq, D_MIX), mod, layer, w_out, ln_g, ln_b,
                    seq=seq, is_ctx=is_ctx)


def _odd_proj_weights(w_in, a_up, a_b):
    kq = GLA_HEADS * GLA_DK
    low0 = 2 * kq + 2 * D_MIX
    low1 = low0 + 2 * GLA_RANK
    w = jnp.concatenate([w_in[:, :low0], w_in[:, low1:], w_in[:, low0:low1],
                         jnp.zeros((D_MODEL, LANE - 2 * GLA_RANK), w_in.dtype)], axis=1).astype(BF16)
    up = jnp.zeros((LANE, 2 * kq), F32)
    up = up.at[0:GLA_RANK, 0:kq].set(a_up[0]).at[GLA_RANK:2 * GLA_RANK, kq:].set(a_up[1]).astype(BF16)
    return w, up, a_b.reshape(1, 2 * kq)


def kernel(x, c, ctx, c_ctx, mod_w, mod_b, ffn_w_in, ffn_w_out, ln_g, ln_b, ev_w_in, ev_w_out, hy_conv_w, hy_conv_b, hy_w1, hy_b1, hy_w2, hy_b2, hy_w3, hy_freq, hy_skip, od_w_in, od_w_out, gla_a_up, gla_a_b, gla_norm_g, hg_lb, hg_norm_g):
    nb, seq, d = x.shape
    ctx_len = ctx.shape[1]
    assert d == D_MODEL and seq % 512 == 0 and ctx_len % CHUNK == 0 and nb + 1 <= MOD_ROWS - 7
    assert mod_w.shape[0] == DEPTH == 2

    c_all = jnp.zeros((MOD_ROWS, D_MODEL), F32).at[:nb].set(c).at[MOD_ROWS - 8].set(c_ctx)
    mod = _mod_table(c_all, mod_w, mod_b)
    w_in_b = ffn_w_in.astype(BF16)
    w_out_b = ffn_w_out.astype(BF16)

    xs = x.reshape(nb * seq, D_MODEL)
    xc = ctx.reshape(nb * ctx_len, D_MODEL)
    streams = [(xs, seq, False), (xc, ctx_len, True)]

    ln_g4 = ln_g.reshape(DEPTH, 3, 1, D_MODEL)
    ln_b4 = ln_b.reshape(DEPTH, 3, 1, D_MODEL)

    def ffn(stream, layer, half):
        arr, length, is_ctx = stream
        return (_ffn(arr, mod, layer, half, w_in_b, w_out_b, ln_g4, ln_b4, seq=length, is_ctx=is_ctx), length, is_ctx)

    streams = [ffn(s, 0, 0) for s in streams]
    filt = (hy_w1[0], hy_b1[0], hy_w2[0], hy_b2[0], hy_w3[0], hy_freq[0])
    ev_in_b, ev_out_b = ev_w_in[0].astype(BF16), ev_w_out[0].astype(BF16)
    streams = [(_even_mixer(arr, nb, length, mod, 0, is_ctx, ev_in_b, ev_out_b, hy_conv_w[0], hy_conv_b[0], filt,
                            hy_skip[0], ln_g4, ln_b4), length, is_ctx) for arr, length, is_ctx in streams]
    streams = [ffn(s, 0, 1) for s in streams]

    streams = [ffn(s, 1, 0) for s in streams]
    (xs, _, _), (xc, _, _) = streams
    w_proj, up, up_b = _odd_proj_weights(od_w_in[0], gla_a_up[0], gla_a_b[0])
    n_gla, n_hg = (P_R + 1) * D_MIX, (PH_GD + 1) * D_MIX
    w_main = jnp.concatenate([w_proj[:, :n_gla], w_proj[:, n_gla + n_hg:]], axis=1)
    pc = _proj(xc, mod, 1, w_proj, seq=ctx_len, is_ctx=True, gates=(up, up_b), want=TOKEN_TILE // 2)
    pm = _proj(xs, mod, 1, w_main, seq=seq, is_ctx=False, gates=(up, up_b))
    ph = _proj_colmajor(xs, nb, seq, mod, 1, w_proj[:, n_gla:n_gla + n_hg])
    pc3 = pc.reshape(nb, ctx_len, -1)
    o_gla = _gla(pm.reshape(nb, seq, -1), pc3).reshape(nb * seq, D_MIX)
    o_hg = _hgrn(ph.reshape(nb, seq, -1), pc3, hg_lb, 1).reshape(ph.shape[:3] + (D_MIX,))
    xs = _outproj(xs, o_gla, o_hg, mod, 1, od_w_out[0].astype(BF16), ln_g4, ln_b4, seq=seq, is_ctx=False,
                  readout=(pm, ph, gla_norm_g[0], hg_norm_g[0]))
    xs, _, _ = ffn((xs, seq, False), 1, 1)
    return xs.reshape(nb, seq, D_MODEL)
```

```python
import functools
import math

import jax
import jax.numpy as jnp
from jax import lax
from jax.experimental import pallas as pl
from jax.experimental.pallas import tpu as pltpu

F32 = jnp.float32
BF16 = jnp.bfloat16

D_MODEL = 1024
DEPTH = 2
N_MOD = 9
D_FF = 2816
LN_EPS = 1e-6
ALPHA = (2.0 * DEPTH) ** 0.25
GRID_W = 64
D_FOURIER = 512
FOURIER_GROUPS = 8
D_HYENA = 512
HY_EMB = 33
HY_HID = 64
HY_FAST_DECAY = 0.3
HY_SLOW_DECAY = 1.5
HY_TARGET = 1e-2
CHUNK = 64
GLA_HEADS = 4
GLA_DK = 64
GLA_DV = 128
GLA_RANK = 16
GLA_TAU = 16.0
HG_HEADS = 4
HG_D = 128
D_MIX = 512

LANE = 128
MOD_ROWS = 24
VMEM_LIMIT = 56 * 1024 * 1024

P_QK, P_V, P_R, P_QD, P_FFL, P_FBL, P_IDD, P_GD, P_GATE = range(9)
PM_QK, PM_V, PM_R, PM_GATE = range(4)
PH_QD, PH_FFL, PH_FBL, PH_IDD, PH_GD = range(5)


def _params(*sem):
    return pltpu.CompilerParams(dimension_semantics=sem, vmem_limit_bytes=VMEM_LIMIT)


def _dot(a, b):
    return jnp.dot(a, b, preferred_element_type=F32)


def _dot_nt(a, b):
    return lax.dot_general(a, b, (((1,), (1,)), ((), ())), preferred_element_type=F32)


def _dot_tn(a, b):
    return lax.dot_general(a, b, (((0,), (0,)), ((), ())), preferred_element_type=F32)


def _ln(x):
    mu = jnp.mean(x, axis=-1, keepdims=True)
    xc = x - mu
    var = jnp.mean(xc * xc, axis=-1, keepdims=True)
    return xc * lax.rsqrt(var + LN_EPS)


def _sigmoid(x):
    return 1.0 / (1.0 + jnp.exp(-x))


def _mod_row(mod_ref, row, k):
    return mod_ref[pl.ds(row, 1), k * D_MODEL:(k + 1) * D_MODEL]


def _mod_row_index(tile, tm, seq, is_ctx):
    if is_ctx:
        return MOD_ROWS - 8
    return (tile * tm) // seq


def _const_spec(shape):
    return pl.BlockSpec(shape, lambda *_: (0,) * len(shape), pipeline_mode=pl.Buffered(1))


def _picked_spec(shape, lead):
    tail = tuple(shape[len(lead):])
    return pl.BlockSpec((None,) * len(lead) + tail, lambda *_: tuple(lead) + (0,) * len(tail),
                        pipeline_mode=pl.Buffered(1))


TOKEN_TILE = 1024
SUB_TILES = 2


def _token_tile(n, seq, is_ctx, want=TOKEN_TILE):
    tm = min(want, n if is_ctx else seq)
    while n % tm or (not is_ctx and seq % tm):
        tm //= 2
    return tm


def _mod_kernel(c_ref, w_ref, b_ref, o_ref):
    c = c_ref[...]
    s = (c * _sigmoid(c)).astype(BF16)
    o_ref[...] = _dot(s, w_ref[...].astype(BF16)) + b_ref[...]


def _mod_table(c_all, mod_w, mod_b):
    return pl.pallas_call(
        _mod_kernel,
        grid=(DEPTH, N_MOD),
        in_specs=[
            _const_spec((MOD_ROWS, D_MODEL)),
            pl.BlockSpec((None, D_MODEL, D_MODEL), lambda l, j: (l, 0, j)),
            pl.BlockSpec((None, 1, D_MODEL), lambda l, j: (l, 0, j)),
        ],
        out_specs=pl.BlockSpec((None, MOD_ROWS, D_MODEL), lambda l, j: (l, 0, j)),
        out_shape=jax.ShapeDtypeStruct((DEPTH, MOD_ROWS, N_MOD * D_MODEL), F32),
        compiler_params=_params("parallel", "parallel"),
        name="mod_table",
    )(c_all, mod_w, mod_b.reshape(DEPTH, 1, N_MOD * D_MODEL))


FFN_CHUNK = 256


def _swiglu(xm, win_ref, wout_ref, side_work=()):
    side_work = list(side_work)
    acc = None
    for j in range(D_FF // FFN_CHUNK):
        lo = j * FFN_CHUNK
        a = _dot(xm, win_ref[:, lo:lo + FFN_CHUNK])
        u = _dot(xm, win_ref[:, D_FF + lo:D_FF + lo + FFN_CHUNK])
        h = (a * _sigmoid(a) * u).astype(BF16)
        y = _dot(h, wout_ref[lo:lo + FFN_CHUNK, :])
        acc = y if acc is None else acc + y
        if side_work:
            side_work.pop(0)()
    for work in side_work:
        work()
    return acc


FFN_SLABS = 8


def _ffn_kernel(x_ref, mod_ref, win_ref, wout_ref, g_ref, b_ref, o_ref, *, k0, tm, seq, is_ctx):
    row = _mod_row_index(pl.program_id(0), tm, seq, is_ctx)
    shift, scale, gate = (_mod_row(mod_ref, row, k0 + t) for t in range(3))
    half = tm // 2
    slab = half // FFN_SLABS

    def modulated(r0, n):
        return (_ln(x_ref[r0:r0 + n, :]) * (1.0 + scale) + shift).astype(BF16)

    def finish(r0, n, acc):
        z = ALPHA * x_ref[r0:r0 + n, :] + (0.5 * gate) * acc
        o_ref[r0:r0 + n, :] = _ln(z) * g_ref[...] + b_ref[...]

    xm_b = []
    acc_a = _swiglu(modulated(0, half), win_ref, wout_ref,
                    [functools.partial(lambda s: xm_b.append(modulated(half + s * slab, slab)), s)
                     for s in range(FFN_SLABS)])
    acc_b = _swiglu(jnp.concatenate(xm_b, axis=0), win_ref, wout_ref,
                    [functools.partial(lambda s: finish(s * slab, slab, acc_a[s * slab:(s + 1) * slab]), s)
                     for s in range(FFN_SLABS)])
    finish(half, half, acc_b)


def _ffn(x2, mod, layer, half, w_in, w_out, ln_g, ln_b, *, seq, is_ctx):
    n = x2.shape[0]
    tm = _token_tile(n, seq, is_ctx)
    kern = functools.partial(_ffn_kernel, k0=6 * half, tm=tm, seq=seq, is_ctx=is_ctx)
    return pl.pallas_call(
        kern,
        grid=(n // tm,),
        in_specs=[
            pl.BlockSpec((tm, D_MODEL), lambda i: (i, 0)),
            _picked_spec(mod.shape, (layer,)),
            _picked_spec(w_in.shape, (layer, half)),
            _picked_spec(w_out.shape, (layer, half)),
            _picked_spec(ln_g.shape, (layer, 2 * half)),
            _picked_spec(ln_b.shape, (layer, 2 * half)),
        ],
        out_specs=pl.BlockSpec((tm, D_MODEL), lambda i: (i, 0)),
        out_shape=jax.ShapeDtypeStruct((n, D_MODEL), F32),
        compiler_params=_params("parallel"),
        name="ffn",
    )(x2, mod, w_in, w_out, ln_g, ln_b)


def _log_sigmoid(x):
    return jnp.minimum(x, 0.0) - jnp.log(1.0 + jnp.exp(-jnp.abs(x)))


def _proj_kernel(x_ref, mod_ref, w_ref, *rest, tm, seq, is_ctx, gla_gates):
    row = _mod_row_index(pl.program_id(0), tm, seq, is_ctx)
    shift, scale = _mod_row(mod_ref, row, 3), _mod_row(mod_ref, row, 4)
    subs = [slice(r, r + tm // SUB_TILES) for r in range(0, tm, tm // SUB_TILES)]
    xms = [(_ln(x_ref[s, :]) * (1.0 + scale) + shift).astype(BF16) for s in subs]
    ps = [_dot(xm, w_ref[...]) for xm in xms]
    if not gla_gates:
        (o_ref,) = rest
        for s, p in zip(subs, ps):
            o_ref[s, :] = p
        return
    aup_ref, ab_ref, o_ref = rest
    for s, p in zip(subs, ps):
        main = p.shape[1] - LANE
        o_ref[s, :main] = p[:, :main]
        low = p[:, main:].astype(BF16)
        o_ref[s, main:] = _log_sigmoid(_dot(low, aup_ref[...]) + ab_ref[...]) * (1.0 / GLA_TAU)


PERM_COLS = 8


def _proj_colmajor_kernel(x_ref, mod_ref, perm_ref, w_ref, o_ref):
    rows, cols, _ = x_ref.shape
    row = pl.program_id(0)
    shift, scale = _mod_row(mod_ref, row, 3), _mod_row(mod_ref, row, 4)
    per_sub = cols // SUB_TILES
    for s in range(SUB_TILES):
        parts = []
        for c in range(s * per_sub, (s + 1) * per_sub, PERM_COLS):
            x = x_ref[:, c:c + PERM_COLS, :].reshape(rows * PERM_COLS, D_MODEL)
            xm = (_ln(x) * (1.0 + scale) + shift).astype(BF16)
            parts.append(_dot(perm_ref[...], xm).astype(BF16))
        p = _dot(jnp.concatenate(parts, axis=0), w_ref[...])
        o_ref[s * per_sub:(s + 1) * per_sub] = p.reshape(per_sub, rows, w_ref.shape[1])


def _proj_colmajor(x2, nb, seq, mod, layer, w, *, cols=32):
    rows = seq // GRID_W
    n_out = w.shape[1]
    n = rows * PERM_COLS
    i = lax.broadcasted_iota(jnp.int32, (n, n), 0)
    j = lax.broadcasted_iota(jnp.int32, (n, n), 1)
    perm = jnp.where(j == (i % rows) * PERM_COLS + i // rows, 1.0, 0.0).astype(BF16)
    return pl.pallas_call(
        _proj_colmajor_kernel,
        grid=(nb, GRID_W // cols),
        in_specs=[pl.BlockSpec((None, rows, cols, D_MODEL), lambda b, c: (b, 0, c, 0)),
                  _picked_spec(mod.shape, (layer,)), _const_spec(perm.shape), _const_spec(w.shape)],
        out_specs=pl.BlockSpec((None, cols, rows, n_out), lambda b, c: (b, c, 0, 0)),
        out_shape=jax.ShapeDtypeStruct((nb, GRID_W, rows, n_out), F32),
        compiler_params=_params("parallel", "parallel"),
        name="mixer_proj_colmajor",
    )(x2.reshape(nb, rows, GRID_W, D_MODEL), mod, perm, w)


def _proj(x2, mod, layer, w, *, seq, is_ctx, gates=None, want=TOKEN_TILE):
    n = x2.shape[0]
    tm = _token_tile(n, seq, is_ctx, want)
    n_out = w.shape[1] - LANE + gates[0].shape[1] if gates is not None else w.shape[1]
    kern = functools.partial(_proj_kernel, tm=tm, seq=seq, is_ctx=is_ctx, gla_gates=gates is not None)
    extra = [] if gates is None else list(gates)
    return pl.pallas_call(
        kern,
        grid=(n // tm,),
        in_specs=[pl.BlockSpec((tm, D_MODEL), lambda i: (i, 0)), _picked_spec(mod.shape, (layer,)),
                  _const_spec(w.shape)] + [_const_spec(e.shape) for e in extra],
        out_specs=pl.BlockSpec((tm, n_out), lambda i: (i, 0)),
        out_shape=jax.ShapeDtypeStruct((n, n_out), F32),
        compiler_params=_params("parallel"),
        name="mixer_proj",
    )(x2, mod, w, *extra)


DEINT_BLOCK = 256


def _deint_perm(inverse=False):
    i = lax.broadcasted_iota(jnp.int32, (DEINT_BLOCK, DEINT_BLOCK), 0)
    j = lax.broadcasted_iota(jnp.int32, (DEINT_BLOCK, DEINT_BLOCK), 1)
    half = DEINT_BLOCK // 2
    src = 2 * (i % half) + i // half
    hit = (i == 2 * (j % half) + j // half) if inverse else (j == src)
    return jnp.where(hit, 1.0, 0.0).astype(BF16)


def _proj_deint_kernel(x_ref, mod_ref, perm_ref, w_ref, o_ref, *, tm, seq, is_ctx):
    row = _mod_row_index(pl.program_id(0), tm, seq, is_ctx)
    shift, scale = _mod_row(mod_ref, row, 3), _mod_row(mod_ref, row, 4)
    half = DEINT_BLOCK // 2
    n_sub = SUB_TILES if tm % (SUB_TILES * DEINT_BLOCK) == 0 else 1
    sub = tm // n_sub
    for s in range(n_sub):
        parts = []
        for r in range(s * sub, (s + 1) * sub, DEINT_BLOCK):
            xm = (_ln(x_ref[r:r + DEINT_BLOCK, :]) * (1.0 + scale) + shift).astype(BF16)
            parts.append(_dot(perm_ref[...], xm).astype(BF16))
        p = _dot(parts[0] if len(parts) == 1 else jnp.concatenate(parts, axis=0), w_ref[...])
        for k in range(sub // DEINT_BLOCK):
            dst = (s * sub + k * DEINT_BLOCK) // 2
            o_ref[0, dst:dst + half, :] = p[k * DEINT_BLOCK:k * DEINT_BLOCK + half]
            o_ref[1, dst:dst + half, :] = p[k * DEINT_BLOCK + half:(k + 1) * DEINT_BLOCK]


def _proj_deint(x2, nb, seq, mod, layer, w, *, is_ctx):
    n = x2.shape[0]
    tm = _token_tile(n, seq, False)
    assert tm % DEINT_BLOCK == 0
    tiles = seq // tm
    n_out = w.shape[1]
    perm = _deint_perm()
    return pl.pallas_call(
        functools.partial(_proj_deint_kernel, tm=tm, seq=seq, is_ctx=is_ctx),
        grid=(n // tm,),
        in_specs=[pl.BlockSpec((tm, D_MODEL), lambda i: (i, 0)), _picked_spec(mod.shape, (layer,)),
                  _const_spec(perm.shape), _const_spec(w.shape)],
        out_specs=pl.BlockSpec((None, 2, tm // 2, n_out), lambda i: (i // tiles, 0, i % tiles, 0)),
        out_shape=jax.ShapeDtypeStruct((nb, 2, seq // 2, n_out), F32),
        compiler_params=_params("parallel"),
        name="mixer_proj_deint",
    )(x2, mod, perm, w)


def _fourier_kernel(a_ref, cg_ref, te_ref, to_ref, o_ref, stack_ref, *, half):
    @pl.when(pl.program_id(1) == 0)
    def _():
        for par in range(2):
            p = _dot(a_ref[par].astype(BF16), cg_ref[...])
            stack_ref[par, 0:half, :] = p[:, :D_MIX].astype(BF16)
            stack_ref[par, half:2 * half, :] = p[:, D_MIX:].astype(BF16)

    scale = 1.0 / math.sqrt(2 * half * (D_FOURIER // FOURIER_GROUPS))
    ev = _dot(te_ref[...], stack_ref[0])
    od = _dot(to_ref[...], stack_ref[1])
    o_ref[0] = ((ev + od) * scale).astype(o_ref.dtype)
    o_ref[1] = ((ev - od) * scale).astype(o_ref.dtype)


def _fourier(p4, cg, te, to, *, tl):
    nb, _, half, _ = p4.shape
    out = pl.pallas_call(
        functools.partial(_fourier_kernel, half=half),
        grid=(nb, half // tl),
        in_specs=[
            pl.BlockSpec((None, 2, half, D_MIX), lambda b, j: (b, 0, 0, 0)),
            _const_spec(cg.shape),
            pl.BlockSpec((tl, 2 * half), lambda b, j: (j, 0)),
            pl.BlockSpec((tl, 2 * half), lambda b, j: (j, 0)),
        ],
        out_specs=pl.BlockSpec((None, 2, tl, D_MIX), lambda b, j: (b, 0, j, 0)),
        out_shape=jax.ShapeDtypeStruct((nb, 2, half, D_MIX), BF16),
        scratch_shapes=[pltpu.VMEM((2, 2 * half, D_MIX), BF16)],
        compiler_params=_params("parallel", "arbitrary"),
        name="fourier_mix",
    )(p4, cg, te, to)
    return out.reshape(nb * 2 * half, D_MIX)


def _hyena_filter_kernel(z_ref, w1_ref, b1_ref, w2_ref, b2_ref, w3f_ref, w3b_ref, freq_ref, delta_ref,
                         fwd_ref, bwd_ref):
    hi = lax.Precision.HIGHEST
    z = z_ref[...]
    freq = freq_ref[...]
    hdn = jnp.sin(freq * (jnp.dot(z, w1_ref[...], precision=hi, preferred_element_type=F32) + b1_ref[...]))
    hdn = jnp.sin(freq * (jnp.dot(hdn, w2_ref[...], precision=hi, preferred_element_type=F32) + b2_ref[...]))
    decay = jnp.exp(-z[:, 0:1] * delta_ref[...])
    fwd = jnp.dot(hdn, w3f_ref[...], precision=hi, preferred_element_type=F32) * decay
    bwd = jnp.dot(hdn, w3b_ref[...], precision=hi, preferred_element_type=F32) * decay
    row = lax.broadcasted_iota(jnp.int32, bwd.shape, 0)
    bwd = jnp.where(row == 0, 0.0, bwd)
    norm = jnp.sum(jnp.abs(fwd), axis=0, keepdims=True) + jnp.sum(jnp.abs(bwd), axis=0, keepdims=True) + 1e-6
    inv = 1.0 / norm
    fwd_ref[...] = fwd * inv
    bwd_ref[...] = bwd * inv


def _hyena_filters(seq, w1, b1, w2, b2, w3, freq):
    pos = jnp.concatenate([jnp.arange(0, seq, 2), jnp.arange(1, seq, 2)]).astype(F32)[:, None]
    t = pos / (seq - 1)
    bands = (HY_EMB - 1) // 2
    fr = jnp.linspace(1e-4, bands - 1, bands, dtype=F32)[None, :]
    w = 2.0 * math.pi * pos * fr / seq
    z = jnp.concatenate([t, jnp.cos(w), -jnp.sin(w), jnp.zeros((seq, LANE - HY_EMB), F32)], axis=-1)
    pad = LANE - HY_HID
    w1p = jnp.pad(w1, ((0, LANE - HY_EMB), (0, pad)))
    w2p = jnp.pad(w2, ((0, pad), (0, pad)))
    w3p = jnp.pad(w3, ((0, pad), (0, 0)))
    row = lambda v: jnp.pad(v, (0, pad)).reshape(1, LANE)
    delta = jnp.abs(jnp.linspace(math.log(HY_TARGET) / HY_SLOW_DECAY, math.log(HY_TARGET) / HY_FAST_DECAY,
                                 D_HYENA, dtype=F32)).reshape(1, D_HYENA)
    sq = lambda: _const_spec((LANE, LANE))
    vec = lambda: _const_spec((1, LANE))
    n_ord = 2
    return pl.pallas_call(
        _hyena_filter_kernel,
        grid=(n_ord,),
        in_specs=[_const_spec((seq, LANE)), sq(), vec(), sq(), vec(),
                  pl.BlockSpec((LANE, D_MIX), lambda o: (0, o)),
                  pl.BlockSpec((LANE, D_MIX), lambda o: (0, n_ord + o)),
                  vec(), _const_spec((1, D_MIX))],
        out_specs=[pl.BlockSpec((seq, D_MIX), lambda o: (0, o))] * 2,
        out_shape=[jax.ShapeDtypeStruct((seq, n_ord * D_MIX), F32)] * 2,
        compiler_params=_params("parallel"),
        name="hyena_filters",
    )(z, w1p, row(b1), w2p, row(b2), w3p, w3p, row(freq), delta)


def _alt_sum(x):
    row = lax.broadcasted_iota(jnp.int32, x.shape, 0)
    return jnp.sum(jnp.where((row & 1) == 1, -x, x), axis=0, keepdims=True)


def _alt_rows(v, shape):
    row = lax.broadcasted_iota(jnp.int32, shape, 0)
    return jnp.where((row & 1) == 1, -v, v)


def _half_spectrum(xe, xo, ce, se, co, so):
    ac, bc = _dot(ce, xe), _dot(co, xo)
    a_s, b_s = _dot(se, xe), _dot(so, xo)
    return ac + bc, a_s + b_s, ac - bc, b_s - a_s


def _hyena_spec_kernel(ce_ref, se_ref, co_ref, so_ref, fwd_ref, bwd_ref, alo_ref, blo_ref, ahi_ref, bhi_ref, mid_ref,
                       *, half):
    n = 4 * half
    tabs = (ce_ref[...], se_ref[...], co_ref[...], so_ref[...])
    fwd, bwd = fwd_ref[...], bwd_ref[...]
    fe, fo, be, bo = fwd[:half], fwd[half:], bwd[:half], bwd[half:]
    f_re_lo, f_im_lo, f_re_hi, f_im_hi = _half_spectrum(fe.astype(BF16), fo.astype(BF16), *tabs)
    b_re_lo, b_im_lo, b_re_hi, b_im_hi = _half_spectrum(be.astype(BF16), bo.astype(BF16), *tabs)
    row = lax.broadcasted_iota(jnp.int32, f_re_lo.shape, 0)
    wgt = jnp.where(row == 0, 1.0 / n, 2.0 / n)
    alo_ref[...] = wgt * (f_re_lo + b_re_lo)
    blo_ref[...] = wgt * (b_im_lo - f_im_lo)
    ahi_ref[...] = wgt * (f_re_hi + b_re_hi)
    bhi_ref[...] = wgt * (b_im_hi - f_im_hi)
    mid_ref[0:1, :] = (2.0 / n) * (_alt_sum(fe) + _alt_sum(be))
    mid_ref[1:2, :] = (2.0 / n) * (_alt_sum(bo) - _alt_sum(fo))


def _hyena_spectra(tabs, fwd, bwd, *, cb=256):
    seq, width = fwd.shape
    half = seq // 2
    blk = lambda rows: pl.BlockSpec((rows, cb), lambda j: (0, j))
    return pl.pallas_call(
        functools.partial(_hyena_spec_kernel, half=half),
        grid=(width // cb,),
        in_specs=[_const_spec((half, half))] * 4 + [blk(seq), blk(seq)],
        out_specs=[blk(half)] * 4 + [blk(2)],
        out_shape=[jax.ShapeDtypeStruct((half, width), F32)] * 4 + [jax.ShapeDtypeStruct((2, width), F32)],
        compiler_params=_params("parallel"),
        name="hyena_spectra",
    )(*tabs[:4], fwd, bwd)


def _short_conv(e, o, w, b):
    half = e.shape[0]
    row = lax.broadcasted_iota(jnp.int32, e.shape, 0)
    o_prev = jnp.where(row == 0, 0.0, pltpu.roll(o, 1, 0))
    e_next = jnp.where(row == half - 1, 0.0, pltpu.roll(e, half - 1, 0))
    w0, w1, w2 = w[0:1, :], w[1:2, :], w[2:3, :]
    return o_prev * w0 + e * w1 + o * w2 + b, e * w0 + o * w1 + e_next * w2 + b


HYENA_ROW_SPLIT = 2


def _hyena_order_kernel(z_ref, x_ref, ce_ref, se_ref, co_ref, so_ref, cot_ref, sot_ref, alo_ref, blo_ref, ahi_ref,
                        bhi_ref, mid_ref, skip_ref, wx_ref, bx_ref, *rest, order, first):
    if first:
        wz_ref, bz0_ref, o_ref = rest
        ze, zo = _short_conv(z_ref[0], z_ref[1], wz_ref[...], bz0_ref[...])
    else:
        (o_ref,) = rest
        ze, zo = z_ref[0], z_ref[1]
    half = ze.shape[0]
    groups = [slice(r, r + half // HYENA_ROW_SPLIT) for r in range(0, half, half // HYENA_ROW_SPLIT)]
    zeb, zob = ze.astype(BF16), zo.astype(BF16)
    prods = []
    for rs in groups:
        re_lo, im_lo, re_hi, im_hi = _half_spectrum(zeb, zob, ce_ref[rs, :], se_ref[rs, :], co_ref[rs, :],
                                                    so_ref[rs, :])
        alo, blo, ahi, bhi = alo_ref[rs, :], blo_ref[rs, :], ahi_ref[rs, :], bhi_ref[rs, :]
        wr_lo, wi_lo = re_lo * alo + im_lo * blo, re_lo * blo - im_lo * alo
        wr_hi, wi_hi = re_hi * ahi + im_hi * bhi, re_hi * bhi - im_hi * ahi
        prods.append(((wr_lo + wr_hi).astype(BF16), (wi_lo - wi_hi).astype(BF16),
                      (wr_lo - wr_hi).astype(BF16), (wi_lo + wi_hi).astype(BF16)))
    p_even_c, p_even_s, p_odd_c, p_odd_s = (jnp.concatenate(p, axis=0) for p in zip(*prods))
    me, mo = _short_conv(x_ref[0], x_ref[1], wx_ref[...], bx_ref[...])
    re_mid, im_mid = _alt_sum(ze), _alt_sum(zo)
    a_mid, b_mid = mid_ref[0:1, :], mid_ref[1:2, :]
    mid_e, mid_o = re_mid * a_mid + im_mid * b_mid, re_mid * b_mid - im_mid * a_mid
    skip = skip_ref[order:order + 1, :]
    for rs in groups:
        conv_e = _dot(ce_ref[rs, :], p_even_c) - _dot(se_ref[rs, :], p_even_s)
        conv_o = _dot(cot_ref[rs, :], p_odd_c) - _dot(sot_ref[rs, :], p_odd_s)
        conv_e = conv_e + _alt_rows(mid_e, conv_e.shape)
        conv_o = conv_o - _alt_rows(mid_o, conv_o.shape)
        o_ref[0, rs, :] = (me[rs] * (conv_e + ze[rs] * skip)).astype(o_ref.dtype)
        o_ref[1, rs, :] = (mo[rs] * (conv_o + zo[rs] * skip)).astype(o_ref.dtype)


def _hyena_order(z4, z_blk0, x4, x_blk0, tabs, coefs, skip, conv_w, conv_b, *, order, first, out_dtype, cb=256):
    nb, _, half, _ = x4.shape
    ncb = D_MIX // cb
    xw0 = (1 + order) * ncb
    conv_b = conv_b.reshape(1, -1)
    tok = lambda blk0: pl.BlockSpec((None, 2, half, cb), lambda j, b: (b, 0, 0, blk0 + j))
    colblk = lambda rows, blk0: pl.BlockSpec((rows, cb), lambda j, b: (0, blk0 + j), pipeline_mode=pl.Buffered(1))
    in_specs = [tok(z_blk0), tok(x_blk0)] + [_const_spec((half, half))] * 6
    in_specs += [colblk(half, order * ncb)] * 4 + [colblk(2, order * ncb), colblk(2, 0), colblk(3, xw0), colblk(1, xw0)]
    args = [z4, x4, *tabs, *coefs, skip, conv_w, conv_b]
    if first:
        in_specs += [colblk(3, 0), colblk(1, 0)]
        args += [conv_w, conv_b]
    return pl.pallas_call(
        functools.partial(_hyena_order_kernel, order=order, first=first),
        grid=(ncb, nb),
        in_specs=in_specs,
        out_specs=pl.BlockSpec((None, 2, half, cb), lambda j, b: (b, 0, 0, j)),
        out_shape=jax.ShapeDtypeStruct((nb, 2, half, D_MIX), out_dtype),
        compiler_params=_params("parallel", "parallel"),
        name="hyena_order",
    )(*args)


def _dit_tables(half, period):
    split = math.gcd(half, 64)

    def part(rows, mult, col_off):
        r = lax.broadcasted_iota(jnp.int32, (rows, half), 0) * mult
        c = 2 * lax.broadcasted_iota(jnp.int32, (rows, half), 1) + col_off
        ang = ((r * c) % period).astype(F32) * (2.0 * math.pi / period)
        return jnp.cos(ang), jnp.sin(ang)

    out = []
    for col_off in (0, 1):
        ca, sa = (t[:, None, :] for t in part(half // split, split, col_off))
        cb, sb = (t[None, :, :] for t in part(split, 1, col_off))
        out += [(ca * cb - sa * sb).reshape(half, half), (sa * cb + ca * sb).reshape(half, half)]
    return out


def _fourier_tables(seq):
    ce, se, co, so = _dit_tables(seq // 2, seq)
    te = jnp.concatenate([ce, -se], axis=1).astype(BF16)
    to = jnp.concatenate([co, -so], axis=1).astype(BF16)
    gsz = D_FOURIER // FOURIER_GROUPS
    r = lax.broadcasted_iota(jnp.int32, (gsz, gsz), 0)
    c = lax.broadcasted_iota(jnp.int32, (gsz, gsz), 1)
    ang = ((r * c) % gsz).astype(F32) * (2.0 * math.pi / gsz)
    eye = jnp.eye(FOURIER_GROUPS, dtype=F32)
    cg = jnp.concatenate([jnp.kron(eye, jnp.cos(ang)), jnp.kron(eye, jnp.sin(ang))], axis=1).astype(BF16)
    return te, to, cg


def _hyena_tables(seq):
    ce, se, co, so = (t.astype(BF16) for t in _dit_tables(seq // 2, 2 * seq))
    return ce, se, co, so, co.T, so.T


def _rms_heads(x, g):
    parts = []
    for h in range(D_MIX // LANE):
        xh = x[:, h * LANE:(h + 1) * LANE]
        parts.append(xh * lax.rsqrt(jnp.mean(xh * xh, axis=-1, keepdims=True) + LN_EPS) * g)
    return jnp.concatenate(parts, axis=-1)


def _outproj_kernel(x_ref, ya_ref, yb_ref, *rest, tm, seq, is_ctx, readout):
    if readout:
        r_ref, gd_ref, gg_ref, hg_ref, unperm_ref, mod_ref, w_ref, g_ref, b_ref, o_ref = rest
        r = r_ref[...]
        ya = (_rms_heads(ya_ref[...], gg_ref[...]) * (r * _sigmoid(r))).astype(BF16)
        yb = _rms_heads(yb_ref[...].reshape(tm, D_MIX) * _sigmoid(gd_ref[...].reshape(tm, D_MIX)), hg_ref[...])
        yb = _dot(unperm_ref[...], yb.astype(BF16)).astype(BF16)
    else:
        unperm_ref, mod_ref, w_ref, g_ref, b_ref, o_ref = rest
        ya = ya_ref[...]
        half = DEINT_BLOCK // 2
        parts = []
        for r in range(0, tm // 2, half):
            blk = jnp.concatenate([yb_ref[0, r:r + half, :], yb_ref[1, r:r + half, :]], axis=0)
            parts.append(_dot(unperm_ref[...], blk).astype(BF16))
        yb = parts[0] if len(parts) == 1 else jnp.concatenate(parts, axis=0)
    row = _mod_row_index(pl.program_id(0), tm, seq, is_ctx)
    gate = _mod_row(mod_ref, row, 5)
    y = _dot(ya, w_ref[:D_MIX, :]) + _dot(yb, w_ref[D_MIX:, :])
    z = ALPHA * x_ref[...] + gate * y
    o_ref[...] = _ln(z) * g_ref[...] + b_ref[...]


def _outproj(x2, ya, yb, mod, layer, w, ln_g, ln_b, *, seq, is_ctx, readout=None):
    n = x2.shape[0]
    tm = _token_tile(n, seq, False, want=TOKEN_TILE if readout is None else TOKEN_TILE // 2)
    tiles = seq // tm
    tok = lambda: pl.BlockSpec((tm, D_MIX), lambda i: (i, 0))
    in_specs = [pl.BlockSpec((tm, D_MODEL), lambda i: (i, 0)), tok()]
    args = [x2, ya, yb]
    if readout is None:
        assert tm % DEINT_BLOCK == 0
        unperm = _deint_perm(inverse=True)
        in_specs += [pl.BlockSpec((None, 2, tm // 2, D_MIX), lambda i: (i // tiles, 0, i % tiles, 0)),
                     _const_spec(unperm.shape)]
        args += [unperm]
    else:
        p_main, p_hg, gla_g, hg_g = readout
        rows = tm // GRID_W
        assert tm % GRID_W == 0 and rows % 8 == 0
        i = lax.broadcasted_iota(jnp.int32, (tm, tm), 0)
        j = lax.broadcasted_iota(jnp.int32, (tm, tm), 1)
        unperm = jnp.where(j == (i % GRID_W) * rows + i // GRID_W, 1.0, 0.0).astype(BF16)
        colmajor = lambda blk: pl.BlockSpec((None, GRID_W, rows, D_MIX), lambda i: (i // tiles, 0, i % tiles, blk))
        in_specs += [colmajor(0), pl.BlockSpec((tm, D_MIX), lambda i: (i, PM_R)), colmajor(PH_GD),
                     _const_spec((1, LANE)), _const_spec((1, LANE)), _const_spec((tm, tm))]
        args += [p_main, p_hg, gla_g.reshape(1, LANE), hg_g.reshape(1, LANE), unperm]
    in_specs += [_picked_spec(mod.shape, (layer,)), _const_spec(w.shape),
                 _picked_spec(ln_g.shape, (layer, 1)), _picked_spec(ln_b.shape, (layer, 1))]
    args += [mod, w, ln_g, ln_b]
    kern = functools.partial(_outproj_kernel, tm=tm, seq=seq, is_ctx=is_ctx, readout=readout is not None)
    return pl.pallas_call(
        kern,
        grid=(n // tm,),
        in_specs=in_specs,
        out_specs=pl.BlockSpec((tm, D_MODEL), lambda i: (i, 0)),
        out_shape=jax.ShapeDtypeStruct((n, D_MODEL), F32),
        compiler_params=_params("parallel"),
        name="mixer_out",
    )(*args)


def _split3(x):
    hi = x.astype(BF16)
    r1 = x - hi.astype(F32)
    mid = r1.astype(BF16)
    lo = (r1 - mid.astype(F32)).astype(BF16)
    return hi, mid, lo


SCAN_UNROLL = 8
CUMSUM_CHUNKS = 4


def _scan_unroll(n_chunks):
    return math.gcd(SCAN_UNROLL, n_chunks)


def _scan_consts(rev, unroll, heads, dk):
    n = math.gcd(unroll, CUMSUM_CHUNKS) * CHUNK
    i = lax.broadcasted_iota(jnp.int32, (n, n), 0)
    j = lax.broadcasted_iota(jnp.int32, (n, n), 1)
    same_chunk = (i // CHUNK) == (j // CHUNK)
    tri = jnp.where(same_chunk & ((j >= i) if rev else (j <= i)), 1.0, 0.0).astype(BF16)
    r = lax.broadcasted_iota(jnp.int32, (CHUNK, heads * CHUNK), 0)
    c = lax.broadcasted_iota(jnp.int32, (CHUNK, heads * CHUNK), 1) & (CHUNK - 1)
    causal = (c >= r) if rev else (c <= r)
    klane_head = lax.broadcasted_iota(jnp.int32, (1, heads * dk), 1) // dk
    vlane_head = lax.broadcasted_iota(jnp.int32, (1, heads * LANE), 1) // LANE
    return tri, causal, klane_head, vlane_head


def _scan_local(q, k, g, vb, consts, *, rev, heads, dk, need_out):
    tri, causal, klane_head, vlane_head = consts
    kt = heads * dk
    parts = _split3(g)
    span = tri.shape[0]
    gsum = jnp.concatenate([sum(_dot(tri, part[r:r + span]) for part in parts) for r in range(0, g.shape[0], span)],
                           axis=0)
    staged = []
    for u in range(g.shape[0] // CHUNK):
        rows = slice(u * CHUNK, (u + 1) * CHUNK)
        gs = gsum[rows]
        gt = gs[0:1] if rev else gs[CHUNK - 1:CHUNK]
        ku, vu = k[rows], vb[rows]
        k_out = (ku * jnp.exp(gt - gs)).astype(BF16)
        q_in = o_intra = None
        if need_out:
            q_in = (q[rows] * jnp.exp(gs)).astype(BF16)
            k_in = (ku * jnp.exp(-gs)).astype(BF16)
            k4 = jnp.concatenate([jnp.where(klane_head == h, k_in, jnp.zeros_like(k_in)) for h in range(heads)], axis=0)
            att = jnp.where(causal, _dot_nt(q_in, k4), 0.0).astype(BF16)
            v4 = jnp.concatenate([jnp.where(vlane_head == h, vu, jnp.zeros_like(vu)) for h in range(heads)], axis=0)
            o_intra = _dot(att, v4)
        if dk % LANE == 0:
            inc = [_dot_tn(k_out[:, h * dk:(h + 1) * dk], vu[:, h * LANE:(h + 1) * LANE]) for h in range(heads)]
        else:
            k_t = jnp.transpose(jnp.concatenate([k_out.astype(F32), jnp.zeros((LANE - CHUNK, kt), F32)], axis=0))
            k_t = k_t.astype(BF16)
            v_pad = jnp.concatenate([vu, jnp.zeros((LANE - CHUNK, heads * LANE), BF16)], axis=0)
            inc = [_dot(k_t[h * dk:(h + 1) * dk, :], v_pad[:, h * LANE:(h + 1) * LANE]) for h in range(heads)]
        dec = jnp.transpose(jnp.broadcast_to(jnp.exp(gt), (LANE, kt)))
        staged.append((q_in, o_intra, inc, [dec[h * dk:(h + 1) * dk] for h in range(heads)]))
    return staged


def _block_diag(blocks):
    z = jnp.zeros_like(blocks[0])
    return jnp.concatenate([jnp.concatenate([b if c == h else z for c in range(len(blocks))], axis=1)
                            for h, b in enumerate(blocks)], axis=0)


def _scan_states(s_ref, base, staged):
    heads = len(staged[0][2])
    state = [s_ref[base + h] for h in range(heads)]
    outs = []
    for q_in, o_intra, inc, decay in staged:
        if q_in is not None:
            outs.append(o_intra + _dot(q_in, _block_diag([s.astype(BF16) for s in state])))
        state = [s * d + i for s, d, i in zip(state, decay, inc)]
    for h in range(heads):
        s_ref[base + h] = state[h]
    return outs


def _gla_kernel(qk_ref, v_ref, g_ref, cqk_ref, cv_ref, cg_ref, o_ref, s_ref, ob_ref, *, seq, ctx_len):
    heads, dk = GLA_HEADS, GLA_DK
    kt = heads * dk
    s_ref[...] = jnp.zeros_like(s_ref)

    def make_run(qk, vv, gg, n_chunks, need_out):
        unroll = _scan_unroll(n_chunks)
        rows = unroll * CHUNK
        consts = [_scan_consts(rev, unroll, heads, dk) for rev in (False, True)]

        def run(n):
            work = []
            for d, rev in enumerate((False, True)):
                start = (n_chunks * CHUNK - rows - n * rows) if rev else n * rows
                sl = pl.ds(pl.multiple_of(start, rows), rows)
                q = qk[sl, 0:kt] * (dk ** -0.5) if need_out else None
                staged = _scan_local(q, qk[sl, kt:2 * kt], gg[sl, d * kt:(d + 1) * kt], vv[sl, :].astype(BF16),
                                     consts[d], rev=rev, heads=heads, dk=dk, need_out=need_out)
                work.append((d, rev, sl, staged[::-1] if rev else staged))
            for d, rev, sl, staged in work:
                outs = _scan_states(s_ref, d * heads, staged)
                if need_out:
                    (ob_ref if rev else o_ref)[sl, :] = jnp.concatenate(outs[::-1] if rev else outs, axis=0)

        return run, n_chunks // unroll

    run, steps = make_run(cqk_ref, cv_ref, cg_ref, ctx_len // CHUNK, False)
    for n in range(steps):
        run(n)
    run, steps = make_run(qk_ref, v_ref, g_ref, seq // CHUNK, True)

    def body(n, carry):
        run(n)
        return carry

    lax.fori_loop(0, steps, body, 0)
    o_ref[...] = o_ref[...] + ob_ref[...]


def _gla(p3, pc3):
    nb, seq, _ = p3.shape
    ctx_len = pc3.shape[1]
    blk = lambda length, j: pl.BlockSpec((None, length, D_MIX), lambda b: (b, 0, j))
    return pl.pallas_call(
        functools.partial(_gla_kernel, seq=seq, ctx_len=ctx_len),
        grid=(nb,),
        in_specs=[blk(seq, PM_QK), blk(seq, PM_V), blk(seq, PM_GATE),
                  blk(ctx_len, P_QK), blk(ctx_len, P_V), blk(ctx_len, P_GATE)],
        out_specs=pl.BlockSpec((None, seq, D_MIX), lambda b: (b, 0, 0)),
        out_shape=jax.ShapeDtypeStruct((nb, seq, D_MIX), F32),
        scratch_shapes=[pltpu.VMEM((2 * GLA_HEADS, GLA_DK, GLA_DV), F32),
                        pltpu.VMEM((seq, D_MIX), F32)],
        compiler_params=_params("parallel"),
        name="gla_scan",
    )(p3, p3, p3, pc3, pc3, pc3)


def _forget(lb, logits):
    f = lb + (1.0 - lb) * _sigmoid(logits)
    return 1.0 - f, jnp.log(f)


def _hgrn_kernel(q_ref, ffl_ref, fbl_ref, v_ref, cffl_ref, cfbl_ref, cv_ref, lb_ref, o_ref, s_ref, *,
                 seq, ctx_len, layer):
    heads, dk = HG_HEADS, HG_D
    logits = lb_ref[...]
    e = jnp.exp(logits - jnp.max(logits, axis=0, keepdims=True))
    sm = e / jnp.sum(e, axis=0, keepdims=True)
    lb = jnp.sum(sm[0:layer + 1, :], axis=0, keepdims=True) - sm[0:1, :]
    s_ref[...] = jnp.zeros_like(s_ref)

    def make_run(qq, gates, vv, n_chunks, need_out):
        unroll = _scan_unroll(n_chunks)
        rows = unroll * CHUNK
        consts = [_scan_consts(rev, unroll, heads, dk) for rev in (False, True)]

        def run(n):
            work = []
            for d, rev in enumerate((False, True)):
                start = (n_chunks * CHUNK - rows - n * rows) if rev else n * rows
                sl = pl.ds(pl.multiple_of(start, rows), rows)
                k, g = _forget(lb, gates[d][sl, :])
                staged = _scan_local(qq[sl, :] if need_out else None, k, g, vv[sl, :].astype(BF16), consts[d],
                                     rev=rev, heads=heads, dk=dk, need_out=need_out)
                work.append((d, rev, sl, staged[::-1] if rev else staged))
            for d, rev, sl, staged in work:
                outs = _scan_states(s_ref, d * heads, staged)
                if need_out:
                    o_ref[sl, :] += jnp.concatenate(outs[::-1] if rev else outs, axis=0)

        return run, n_chunks // unroll

    run, steps = make_run(None, (cffl_ref, cfbl_ref), cv_ref, ctx_len // CHUNK, False)
    for n in range(steps):
        run(n)
    o_ref[...] = jnp.zeros_like(o_ref)
    run, steps = make_run(q_ref, (ffl_ref, fbl_ref), v_ref, seq // CHUNK, True)

    def body(n, carry):
        run(n)
        return carry

    lax.fori_loop(0, steps, body, 0)


def _hgrn(ph3, pc3, hg_lb, layer):
    nb, seq, _ = ph3.shape
    ctx_len = pc3.shape[1]
    blk = lambda length, j: pl.BlockSpec((None, length, D_MIX), lambda b: (b, 0, j))
    return pl.pallas_call(
        functools.partial(_hgrn_kernel, seq=seq, ctx_len=ctx_len, layer=layer),
        grid=(nb,),
        in_specs=[blk(seq, PH_QD), blk(seq, PH_FFL), blk(seq, PH_FBL), blk(seq, PH_IDD),
                  blk(ctx_len, P_FFL), blk(ctx_len, P_FBL), blk(ctx_len, P_IDD), _const_spec(hg_lb.shape)],
        out_specs=pl.BlockSpec((None, seq, D_MIX), lambda b: (b, 0, 0)),
        out_shape=jax.ShapeDtypeStruct((nb, seq, D_MIX), F32),
        scratch_shapes=[pltpu.VMEM((2 * HG_HEADS, HG_D, HG_D), F32)],
        compiler_params=_params("parallel"),
        name="hgrn_scan",
    )(ph3, ph3, ph3, ph3, pc3, pc3, pc3, hg_lb)


def _even_mixer(x2, nb, seq, mod, layer, is_ctx, w_in, w_out, conv_w, conv_b, filt, skip, ln_g, ln_b):
    p4 = _proj_deint(x2, nb, seq, mod, layer, w_in, is_ctx=is_ctx)
    te, to, cg = _fourier_tables(seq)
    ya = _fourier(p4, cg, te, to, tl=min(seq // 2, 512))
    tabs = _hyena_tables(seq)
    fwd, bwd = _hyena_filters(seq, *filt)
    coefs = _hyena_spectra(tabs, fwd, bwd)
    cb = 256
    u0 = D_FOURIER // cb
    per = D_HYENA // cb
    y1 = _hyena_order(p4, u0, p4, u0 + per, tabs, coefs, skip, conv_w, conv_b, order=0, first=True, out_dtype=F32,
                      cb=cb)
    y2 = _hyena_order(y1, 0, p4, u0 + 2 * per, tabs, coefs, skip, conv_w, conv_b, order=1, first=False,
                      out_dtype=BF16, cb=cb)
    return _outproj(x2, ya, y2, mod, layer, w_out, ln_g, ln_b, seq=seq, is_ctx=is_ctx)


def _odd_proj_weights(w_in, a_up, a_b):
    kq = GLA_HEADS * GLA_DK
    low0 = 2 * kq + 2 * D_MIX
    low1 = low0 + 2 * GLA_RANK
    w = jnp.concatenate([w_in[:, :low0], w_in[:, low1:], w_in[:, low0:low1],
                         jnp.zeros((D_MODEL, LANE - 2 * GLA_RANK), w_in.dtype)], axis=1).astype(BF16)
    up = jnp.zeros((LANE, 2 * kq), F32)
    up = up.at[0:GLA_RANK, 0:kq].set(a_up[0]).at[GLA_RANK:2 * GLA_RANK, kq:].set(a_up[1]).astype(BF16)
    return w, up, a_b.reshape(1, 2 * kq)


def kernel(x, c, ctx, c_ctx, mod_w, mod_b, ffn_w_in, ffn_w_out, ln_g, ln_b, ev_w_in, ev_w_out, hy_conv_w, hy_conv_b, hy_w1, hy_b1, hy_w2, hy_b2, hy_w3, hy_freq, hy_skip, od_w_in, od_w_out, gla_a_up, gla_a_b, gla_norm_g, hg_lb, hg_norm_g):
    nb, seq, d = x.shape
    ctx_len = ctx.shape[1]
    assert d == D_MODEL and seq % 512 == 0 and ctx_len % CHUNK == 0 and nb + 1 <= MOD_ROWS - 7
    assert mod_w.shape[0] == DEPTH == 2

    c_all = jnp.zeros((MOD_ROWS, D_MODEL), F32).at[:nb].set(c).at[MOD_ROWS - 8].set(c_ctx)
    mod = _mod_table(c_all, mod_w, mod_b)
    w_in_b = ffn_w_in.astype(BF16)
    w_out_b = ffn_w_out.astype(BF16)

    xs = x.reshape(nb * seq, D_MODEL)
    xc = ctx.reshape(nb * ctx_len, D_MODEL)
    streams = [(xs, seq, False), (xc, ctx_len, True)]

    ln_g4 = ln_g.reshape(DEPTH, 3, 1, D_MODEL)
    ln_b4 = ln_b.reshape(DEPTH, 3, 1, D_MODEL)

    def ffn(stream, layer, half):
        arr, length, is_ctx = stream
        return (_ffn(arr, mod, layer, half, w_in_b, w_out_b, ln_g4, ln_b4, seq=length, is_ctx=is_ctx), length, is_ctx)

    streams = [ffn(s, 0, 0) for s in streams]
    filt = (hy_w1[0], hy_b1[0], hy_w2[0], hy_b2[0], hy_w3[0], hy_freq[0])
    ev_in_b, ev_out_b = ev_w_in[0].astype(BF16), ev_w_out[0].astype(BF16)
    streams = [(_even_mixer(arr, nb, length, mod, 0, is_ctx, ev_in_b, ev_out_b, hy_conv_w[0], hy_conv_b[0], filt,
                            hy_skip[0], ln_g4, ln_b4), length, is_ctx) for arr, length, is_ctx in streams]
    streams = [ffn(s, 0, 1) for s in streams]

    streams = [ffn(s, 1, 0) for s in streams]
    (xs, _, _), (xc, _, _) = streams
    w_proj, up, up_b = _odd_proj_weights(od_w_in[0], gla_a_up[0], gla_a_b[0])
    n_gla, n_hg = (P_R + 1) * D_MIX, (PH_GD + 1) * D_MIX
    w_main = jnp.concatenate([w_proj[:, :n_gla], w_proj[:, n_gla + n_hg:]], axis=1)
    pc = _proj(xc, mod, 1, w_proj, seq=ctx_len, is_ctx=True, gates=(up, up_b), want=TOKEN_TILE // 2)
    pm = _proj(xs, mod, 1, w_main, seq=seq, is_ctx=False, gates=(up, up_b))
    ph = _proj_colmajor(xs, nb, seq, mod, 1, w_proj[:, n_gla:n_gla + n_hg])
    pc3 = pc.reshape(nb, ctx_len, -1)
    o_gla = _gla(pm.reshape(nb, seq, -1), pc3).reshape(nb * seq, D_MIX)
    o_hg = _hgrn(ph.reshape(nb, seq, -1), pc3, hg_lb, 1).reshape(ph.shape[:3] + (D_MIX,))
    xs = _outproj(xs, o_gla, o_hg, mod, 1, od_w_out[0].astype(BF16), ln_g4, ln_b4, seq=seq, is_ctx=False,
                  readout=(pm, ph, gla_norm_g[0], hg_norm_g[0]))
    xs, _, _ = ffn((xs, seq, False), 1, 1)
    return xs.reshape(nb, seq, D_MODEL)
```

```python
import functools
import math

import jax
import jax.numpy as jnp
from jax import lax
from jax.experimental import pallas as pl
from jax.experimental.pallas import tpu as pltpu

F32 = jnp.float32
BF16 = jnp.bfloat16

D_MODEL = 1024
DEPTH = 2
N_MOD = 9
D_FF = 2816
LN_EPS = 1e-6
ALPHA = (2.0 * DEPTH) ** 0.25
GRID_W = 64
D_FOURIER = 512
FOURIER_GROUPS = 8
D_HYENA = 512
HY_EMB = 33
HY_HID = 64
HY_FAST_DECAY = 0.3
HY_SLOW_DECAY = 1.5
HY_TARGET = 1e-2
CHUNK = 64
GLA_HEADS = 4
GLA_DK = 64
GLA_DV = 128
GLA_RANK = 16
GLA_TAU = 16.0
HG_HEADS = 4
HG_D = 128
D_MIX = 512

LANE = 128
MOD_ROWS = 24
VMEM_LIMIT = 56 * 1024 * 1024

P_QK, P_V, P_R, P_QD, P_FFL, P_FBL, P_IDD, P_GD, P_GATE = range(9)
PM_QK, PM_V, PM_R, PM_GATE = range(4)
PH_QD, PH_FFL, PH_FBL, PH_IDD, PH_GD = range(5)


def _params(*sem):
    return pltpu.CompilerParams(dimension_semantics=sem, vmem_limit_bytes=VMEM_LIMIT)


def _dot(a, b):
    return jnp.dot(a, b, preferred_element_type=F32)


def _dot_nt(a, b):
    return lax.dot_general(a, b, (((1,), (1,)), ((), ())), preferred_element_type=F32)


def _dot_tn(a, b):
    return lax.dot_general(a, b, (((0,), (0,)), ((), ())), preferred_element_type=F32)


def _ln(x):
    mu = jnp.mean(x, axis=-1, keepdims=True)
    xc = x - mu
    var = jnp.mean(xc * xc, axis=-1, keepdims=True)
    return xc * lax.rsqrt(var + LN_EPS)


def _sigmoid(x):
    return 1.0 / (1.0 + jnp.exp(-x))


def _mod_row(mod_ref, row, k):
    return mod_ref[pl.ds(row, 1), k * D_MODEL:(k + 1) * D_MODEL]


def _mod_row_index(tile, tm, seq, is_ctx):
    if is_ctx:
        return MOD_ROWS - 8
    return (tile * tm) // seq


def _const_spec(shape):
    return pl.BlockSpec(shape, lambda *_: (0,) * len(shape), pipeline_mode=pl.Buffered(1))


def _picked_spec(shape, lead):
    tail = tuple(shape[len(lead):])
    return pl.BlockSpec((None,) * len(lead) + tail, lambda *_: tuple(lead) + (0,) * len(tail),
                        pipeline_mode=pl.Buffered(1))


TOKEN_TILE = 1024
SUB_TILES = 2


def _token_tile(n, seq, is_ctx, want=TOKEN_TILE):
    tm = min(want, n if is_ctx else seq)
    while n % tm or (not is_ctx and seq % tm):
        tm //= 2
    return tm


def _mod_kernel(c_ref, w_ref, b_ref, o_ref):
    c = c_ref[...]
    s = (c * _sigmoid(c)).astype(BF16)
    o_ref[...] = _dot(s, w_ref[...].astype(BF16)) + b_ref[...]


def _mod_table(c_all, mod_w, mod_b):
    return pl.pallas_call(
        _mod_kernel,
        grid=(DEPTH, N_MOD),
        in_specs=[
            _const_spec((MOD_ROWS, D_MODEL)),
            pl.BlockSpec((None, D_MODEL, D_MODEL), lambda l, j: (l, 0, j)),
            pl.BlockSpec((None, 1, D_MODEL), lambda l, j: (l, 0, j)),
        ],
        out_specs=pl.BlockSpec((None, MOD_ROWS, D_MODEL), lambda l, j: (l, 0, j)),
        out_shape=jax.ShapeDtypeStruct((DEPTH, MOD_ROWS, N_MOD * D_MODEL), F32),
        compiler_params=_params("parallel", "parallel"),
        name="mod_table",
    )(c_all, mod_w, mod_b.reshape(DEPTH, 1, N_MOD * D_MODEL))


FFN_CHUNK = 256


def _swiglu(xm, win_ref, wout_ref, side_work=()):
    side_work = list(side_work)
    acc = None
    for j in range(D_FF // FFN_CHUNK):
        lo = j * FFN_CHUNK
        a = _dot(xm, win_ref[:, lo:lo + FFN_CHUNK])
        u = _dot(xm, win_ref[:, D_FF + lo:D_FF + lo + FFN_CHUNK])
        h = (a * _sigmoid(a) * u).astype(BF16)
        y = _dot(h, wout_ref[lo:lo + FFN_CHUNK, :])
        acc = y if acc is None else acc + y
        if side_work:
            side_work.pop(0)()
    for work in side_work:
        work()
    return acc


FFN_SLABS = 8


def _ffn_kernel(x_ref, mod_ref, win_ref, wout_ref, g_ref, b_ref, o_ref, *, k0, tm, seq, is_ctx):
    row = _mod_row_index(pl.program_id(0), tm, seq, is_ctx)
    shift, scale, gate = (_mod_row(mod_ref, row, k0 + t) for t in range(3))
    half = tm // 2
    slab = half // FFN_SLABS

    def modulated(r0, n):
        return (_ln(x_ref[r0:r0 + n, :]) * (1.0 + scale) + shift).astype(BF16)

    def finish(r0, n, acc):
        z = ALPHA * x_ref[r0:r0 + n, :] + (0.5 * gate) * acc
        o_ref[r0:r0 + n, :] = _ln(z) * g_ref[...] + b_ref[...]

    xm_b = []
    acc_a = _swiglu(modulated(0, half), win_ref, wout_ref,
                    [functools.partial(lambda s: xm_b.append(modulated(half + s * slab, slab)), s)
                     for s in range(FFN_SLABS)])
    acc_b = _swiglu(jnp.concatenate(xm_b, axis=0), win_ref, wout_ref,
                    [functools.partial(lambda s: finish(s * slab, slab, acc_a[s * slab:(s + 1) * slab]), s)
                     for s in range(FFN_SLABS)])
    finish(half, half, acc_b)


def _ffn(x2, mod, layer, half, w_in, w_out, ln_g, ln_b, *, seq, is_ctx):
    n = x2.shape[0]
    tm = _token_tile(n, seq, is_ctx)
    kern = functools.partial(_ffn_kernel, k0=6 * half, tm=tm, seq=seq, is_ctx=is_ctx)
    return pl.pallas_call(
        kern,
        grid=(n // tm,),
        in_specs=[
            pl.BlockSpec((tm, D_MODEL), lambda i: (i, 0)),
            _picked_spec(mod.shape, (layer,)),
            _picked_spec(w_in.shape, (layer, half)),
            _picked_spec(w_out.shape, (layer, half)),
            _picked_spec(ln_g.shape, (layer, 2 * half)),
            _picked_spec(ln_b.shape, (layer, 2 * half)),
        ],
        out_specs=pl.BlockSpec((tm, D_MODEL), lambda i: (i, 0)),
        out_shape=jax.ShapeDtypeStruct((n, D_MODEL), F32),
        compiler_params=_params("parallel"),
        name="ffn",
    )(x2, mod, w_in, w_out, ln_g, ln_b)


def _log_sigmoid(x):
    return jnp.minimum(x, 0.0) - jnp.log(1.0 + jnp.exp(-jnp.abs(x)))


def _forget_gates(lb_ref, layer, logits):
    raw = lb_ref[...]
    e = jnp.exp(raw - jnp.max(raw, axis=0, keepdims=True))
    sm = e / jnp.sum(e, axis=0, keepdims=True)
    lb = jnp.sum(sm[0:layer + 1, :], axis=0, keepdims=True) - sm[0:1, :]
    lb = jnp.concatenate([lb, lb], axis=-1)
    return lb + (1.0 - lb) * _sigmoid(logits)


def _proj_kernel(x_ref, mod_ref, w_ref, *rest, tm, seq, is_ctx, gla_gates, forget):
    row = _mod_row_index(pl.program_id(0), tm, seq, is_ctx)
    shift, scale = _mod_row(mod_ref, row, 3), _mod_row(mod_ref, row, 4)
    subs = [slice(r, r + tm // SUB_TILES) for r in range(0, tm, tm // SUB_TILES)]
    xms = [(_ln(x_ref[s, :]) * (1.0 + scale) + shift).astype(BF16) for s in subs]
    ps = [_dot(xm, w_ref[...]) for xm in xms]
    if not gla_gates:
        (o_ref,) = rest
        for s, p in zip(subs, ps):
            o_ref[s, :] = p
        return
    aup_ref, ab_ref, *lb_ref, o_ref = rest
    for s, p in zip(subs, ps):
        main = p.shape[1] - LANE
        o_ref[s, :main] = p[:, :main]
        if forget is not None:
            layer, c0 = forget
            o_ref[s, c0:c0 + 2 * D_MIX] = _forget_gates(lb_ref[0], layer, p[:, c0:c0 + 2 * D_MIX])
        low = p[:, main:].astype(BF16)
        o_ref[s, main:] = _log_sigmoid(_dot(low, aup_ref[...]) + ab_ref[...]) * (1.0 / GLA_TAU)


PERM_COLS = 8


def _proj_colmajor_kernel(x_ref, mod_ref, perm_ref, w_ref, lb_ref, o_ref, *, layer):
    rows, cols, _ = x_ref.shape
    row = pl.program_id(0)
    shift, scale = _mod_row(mod_ref, row, 3), _mod_row(mod_ref, row, 4)
    per_sub = cols // SUB_TILES
    for s in range(SUB_TILES):
        parts = []
        for c in range(s * per_sub, (s + 1) * per_sub, PERM_COLS):
            x = x_ref[:, c:c + PERM_COLS, :].reshape(rows * PERM_COLS, D_MODEL)
            xm = (_ln(x) * (1.0 + scale) + shift).astype(BF16)
            parts.append(_dot(perm_ref[...], xm).astype(BF16))
        p = _dot(jnp.concatenate(parts, axis=0), w_ref[...])
        c0, c1 = PH_FFL * D_MIX, (PH_FBL + 1) * D_MIX
        p = jnp.concatenate([p[:, :c0], _forget_gates(lb_ref, layer, p[:, c0:c1]), p[:, c1:]], axis=-1)
        o_ref[s * per_sub:(s + 1) * per_sub] = p.reshape(per_sub, rows, w_ref.shape[1])


def _proj_colmajor(x2, nb, seq, mod, layer, w, hg_lb, *, cols=32):
    rows = seq // GRID_W
    n_out = w.shape[1]
    n = rows * PERM_COLS
    i = lax.broadcasted_iota(jnp.int32, (n, n), 0)
    j = lax.broadcasted_iota(jnp.int32, (n, n), 1)
    perm = jnp.where(j == (i % rows) * PERM_COLS + i // rows, 1.0, 0.0).astype(BF16)
    return pl.pallas_call(
        functools.partial(_proj_colmajor_kernel, layer=layer),
        grid=(nb, GRID_W // cols),
        in_specs=[pl.BlockSpec((None, rows, cols, D_MODEL), lambda b, c: (b, 0, c, 0)),
                  _picked_spec(mod.shape, (layer,)), _const_spec(perm.shape), _const_spec(w.shape),
                  _const_spec(hg_lb.shape)],
        out_specs=pl.BlockSpec((None, cols, rows, n_out), lambda b, c: (b, c, 0, 0)),
        out_shape=jax.ShapeDtypeStruct((nb, GRID_W, rows, n_out), F32),
        compiler_params=_params("parallel", "parallel"),
        name="mixer_proj_colmajor",
    )(x2.reshape(nb, rows, GRID_W, D_MODEL), mod, perm, w, hg_lb)


def _proj(x2, mod, layer, w, *, seq, is_ctx, gates=None, forget=None, want=TOKEN_TILE):
    n = x2.shape[0]
    tm = _token_tile(n, seq, is_ctx, want)
    n_out = w.shape[1] - LANE + gates[0].shape[1] if gates is not None else w.shape[1]
    kern = functools.partial(_proj_kernel, tm=tm, seq=seq, is_ctx=is_ctx, gla_gates=gates is not None,
                             forget=None if forget is None else (layer, forget[1]))
    extra = [] if gates is None else list(gates)
    if forget is not None:
        extra.append(forget[0])
    return pl.pallas_call(
        kern,
        grid=(n // tm,),
        in_specs=[pl.BlockSpec((tm, D_MODEL), lambda i: (i, 0)), _picked_spec(mod.shape, (layer,)),
                  _const_spec(w.shape)] + [_const_spec(e.shape) for e in extra],
        out_specs=pl.BlockSpec((tm, n_out), lambda i: (i, 0)),
        out_shape=jax.ShapeDtypeStruct((n, n_out), F32),
        compiler_params=_params("parallel"),
        name="mixer_proj",
    )(x2, mod, w, *extra)


DEINT_BLOCK = 256


def _deint_perm(inverse=False):
    i = lax.broadcasted_iota(jnp.int32, (DEINT_BLOCK, DEINT_BLOCK), 0)
    j = lax.broadcasted_iota(jnp.int32, (DEINT_BLOCK, DEINT_BLOCK), 1)
    half = DEINT_BLOCK // 2
    src = 2 * (i % half) + i // half
    hit = (i == 2 * (j % half) + j // half) if inverse else (j == src)
    return jnp.where(hit, 1.0, 0.0).astype(BF16)


def _proj_deint_kernel(x_ref, mod_ref, perm_ref, w_ref, o_ref, *, tm, seq, is_ctx):
    row = _mod_row_index(pl.program_id(0), tm, seq, is_ctx)
    shift, scale = _mod_row(mod_ref, row, 3), _mod_row(mod_ref, row, 4)
    half = DEINT_BLOCK // 2
    n_sub = SUB_TILES if tm % (SUB_TILES * DEINT_BLOCK) == 0 else 1
    sub = tm // n_sub
    for s in range(n_sub):
        parts = []
        for r in range(s * sub, (s + 1) * sub, DEINT_BLOCK):
            xm = (_ln(x_ref[r:r + DEINT_BLOCK, :]) * (1.0 + scale) + shift).astype(BF16)
            parts.append(_dot(perm_ref[...], xm).astype(BF16))
        p = _dot(parts[0] if len(parts) == 1 else jnp.concatenate(parts, axis=0), w_ref[...])
        for k in range(sub // DEINT_BLOCK):
            dst = (s * sub + k * DEINT_BLOCK) // 2
            o_ref[0, dst:dst + half, :] = p[k * DEINT_BLOCK:k * DEINT_BLOCK + half]
            o_ref[1, dst:dst + half, :] = p[k * DEINT_BLOCK + half:(k + 1) * DEINT_BLOCK]


def _proj_deint(x2, nb, seq, mod, layer, w, *, is_ctx):
    n = x2.shape[0]
    tm = _token_tile(n, seq, False)
    assert tm % DEINT_BLOCK == 0
    tiles = seq // tm
    n_out = w.shape[1]
    perm = _deint_perm()
    return pl.pallas_call(
        functools.partial(_proj_deint_kernel, tm=tm, seq=seq, is_ctx=is_ctx),
        grid=(n // tm,),
        in_specs=[pl.BlockSpec((tm, D_MODEL), lambda i: (i, 0)), _picked_spec(mod.shape, (layer,)),
                  _const_spec(perm.shape), _const_spec(w.shape)],
        out_specs=pl.BlockSpec((None, 2, tm // 2, n_out), lambda i: (i // tiles, 0, i % tiles, 0)),
        out_shape=jax.ShapeDtypeStruct((nb, 2, seq // 2, n_out), F32),
        compiler_params=_params("parallel"),
        name="mixer_proj_deint",
    )(x2, mod, perm, w)


def _fourier_kernel(a_ref, cg_ref, te_ref, to_ref, o_ref, stack_ref, *, half):
    @pl.when(pl.program_id(1) == 0)
    def _():
        for par in range(2):
            p = _dot(a_ref[par].astype(BF16), cg_ref[...])
            stack_ref[par, 0:half, :] = p[:, :D_MIX].astype(BF16)
            stack_ref[par, half:2 * half, :] = p[:, D_MIX:].astype(BF16)

    scale = 1.0 / math.sqrt(2 * half * (D_FOURIER // FOURIER_GROUPS))
    ev = _dot(te_ref[...], stack_ref[0])
    od = _dot(to_ref[...], stack_ref[1])
    o_ref[0] = ((ev + od) * scale).astype(o_ref.dtype)
    o_ref[1] = ((ev - od) * scale).astype(o_ref.dtype)


def _fourier(p4, cg, te, to, *, tl):
    nb, _, half, _ = p4.shape
    out = pl.pallas_call(
        functools.partial(_fourier_kernel, half=half),
        grid=(nb, half // tl),
        in_specs=[
            pl.BlockSpec((None, 2, half, D_MIX), lambda b, j: (b, 0, 0, 0)),
            _const_spec(cg.shape),
            pl.BlockSpec((tl, 2 * half), lambda b, j: (j, 0)),
            pl.BlockSpec((tl, 2 * half), lambda b, j: (j, 0)),
        ],
        out_specs=pl.BlockSpec((None, 2, tl, D_MIX), lambda b, j: (b, 0, j, 0)),
        out_shape=jax.ShapeDtypeStruct((nb, 2, half, D_MIX), BF16),
        scratch_shapes=[pltpu.VMEM((2, 2 * half, D_MIX), BF16)],
        compiler_params=_params("parallel", "arbitrary"),
        name="fourier_mix",
    )(p4, cg, te, to)
    return out.reshape(nb * 2 * half, D_MIX)


def _hyena_filter_kernel(z_ref, w1_ref, b1_ref, w2_ref, b2_ref, w3f_ref, w3b_ref, freq_ref, delta_ref,
                         fwd_ref, bwd_ref):
    hi = lax.Precision.HIGHEST
    z = z_ref[...]
    freq = freq_ref[...]
    hdn = jnp.sin(freq * (jnp.dot(z, w1_ref[...], precision=hi, preferred_element_type=F32) + b1_ref[...]))
    hdn = jnp.sin(freq * (jnp.dot(hdn, w2_ref[...], precision=hi, preferred_element_type=F32) + b2_ref[...]))
    decay = jnp.exp(-z[:, 0:1] * delta_ref[...])
    fwd = jnp.dot(hdn, w3f_ref[...], precision=hi, preferred_element_type=F32) * decay
    bwd = jnp.dot(hdn, w3b_ref[...], precision=hi, preferred_element_type=F32) * decay
    row = lax.broadcasted_iota(jnp.int32, bwd.shape, 0)
    bwd = jnp.where(row == 0, 0.0, bwd)
    norm = jnp.sum(jnp.abs(fwd), axis=0, keepdims=True) + jnp.sum(jnp.abs(bwd), axis=0, keepdims=True) + 1e-6
    inv = 1.0 / norm
    fwd_ref[...] = fwd * inv
    bwd_ref[...] = bwd * inv


def _hyena_filters(seq, w1, b1, w2, b2, w3, freq):
    pos = jnp.concatenate([jnp.arange(0, seq, 2), jnp.arange(1, seq, 2)]).astype(F32)[:, None]
    t = pos / (seq - 1)
    bands = (HY_EMB - 1) // 2
    fr = jnp.linspace(1e-4, bands - 1, bands, dtype=F32)[None, :]
    w = 2.0 * math.pi * pos * fr / seq
    z = jnp.concatenate([t, jnp.cos(w), -jnp.sin(w), jnp.zeros((seq, LANE - HY_EMB), F32)], axis=-1)
    pad = LANE - HY_HID
    w1p = jnp.pad(w1, ((0, LANE - HY_EMB), (0, pad)))
    w2p = jnp.pad(w2, ((0, pad), (0, pad)))
    w3p = jnp.pad(w3, ((0, pad), (0, 0)))
    row = lambda v: jnp.pad(v, (0, pad)).reshape(1, LANE)
    delta = jnp.abs(jnp.linspace(math.log(HY_TARGET) / HY_SLOW_DECAY, math.log(HY_TARGET) / HY_FAST_DECAY,
                                 D_HYENA, dtype=F32)).reshape(1, D_HYENA)
    sq = lambda: _const_spec((LANE, LANE))
    vec = lambda: _const_spec((1, LANE))
    n_ord = 2
    return pl.pallas_call(
        _hyena_filter_kernel,
        grid=(n_ord,),
        in_specs=[_const_spec((seq, LANE)), sq(), vec(), sq(), vec(),
                  pl.BlockSpec((LANE, D_MIX), lambda o: (0, o)),
                  pl.BlockSpec((LANE, D_MIX), lambda o: (0, n_ord + o)),
                  vec(), _const_spec((1, D_MIX))],
        out_specs=[pl.BlockSpec((seq, D_MIX), lambda o: (0, o))] * 2,
        out_shape=[jax.ShapeDtypeStruct((seq, n_ord * D_MIX), F32)] * 2,
        compiler_params=_params("parallel"),
        name="hyena_filters",
    )(z, w1p, row(b1), w2p, row(b2), w3p, w3p, row(freq), delta)


def _alt_sum(x):
    row = lax.broadcasted_iota(jnp.int32, x.shape, 0)
    return jnp.sum(jnp.where((row & 1) == 1, -x, x), axis=0, keepdims=True)


def _alt_rows(v, shape):
    row = lax.broadcasted_iota(jnp.int32, shape, 0)
    return jnp.where((row & 1) == 1, -v, v)


def _half_spectrum(xe, xo, ce, se, co, so):
    ac, bc = _dot(ce, xe), _dot(co, xo)
    a_s, b_s = _dot(se, xe), _dot(so, xo)
    return ac + bc, a_s + b_s, ac - bc, b_s - a_s


def _hyena_spec_kernel(ce_ref, se_ref, co_ref, so_ref, fwd_ref, bwd_ref, alo_ref, blo_ref, ahi_ref, bhi_ref, mid_ref,
                       *, half):
    n = 4 * half
    tabs = (ce_ref[...], se_ref[...], co_ref[...], so_ref[...])
    fwd, bwd = fwd_ref[...], bwd_ref[...]
    fe, fo, be, bo = fwd[:half], fwd[half:], bwd[:half], bwd[half:]
    f_re_lo, f_im_lo, f_re_hi, f_im_hi = _half_spectrum(fe.astype(BF16), fo.astype(BF16), *tabs)
    b_re_lo, b_im_lo, b_re_hi, b_im_hi = _half_spectrum(be.astype(BF16), bo.astype(BF16), *tabs)
    row = lax.broadcasted_iota(jnp.int32, f_re_lo.shape, 0)
    wgt = jnp.where(row == 0, 1.0 / n, 2.0 / n)
    alo_ref[...] = wgt * (f_re_lo + b_re_lo)
    blo_ref[...] = wgt * (b_im_lo - f_im_lo)
    ahi_ref[...] = wgt * (f_re_hi + b_re_hi)
    bhi_ref[...] = wgt * (b_im_hi - f_im_hi)
    mid_ref[0:1, :] = (2.0 / n) * (_alt_sum(fe) + _alt_sum(be))
    mid_ref[1:2, :] = (2.0 / n) * (_alt_sum(bo) - _alt_sum(fo))


def _hyena_spectra(tabs, fwd, bwd, *, cb=256):
    seq, width = fwd.shape
    half = seq // 2
    blk = lambda rows: pl.BlockSpec((rows, cb), lambda j: (0, j))
    return pl.pallas_call(
        functools.partial(_hyena_spec_kernel, half=half),
        grid=(width // cb,),
        in_specs=[_const_spec((half, half))] * 4 + [blk(seq), blk(seq)],
        out_specs=[blk(half)] * 4 + [blk(2)],
        out_shape=[jax.ShapeDtypeStruct((half, width), F32)] * 4 + [jax.ShapeDtypeStruct((2, width), F32)],
        compiler_params=_params("parallel"),
        name="hyena_spectra",
    )(*tabs[:4], fwd, bwd)


def _short_conv(e, o, w, b):
    half = e.shape[0]
    row = lax.broadcasted_iota(jnp.int32, e.shape, 0)
    o_prev = jnp.where(row == 0, 0.0, pltpu.roll(o, 1, 0))
    e_next = jnp.where(row == half - 1, 0.0, pltpu.roll(e, half - 1, 0))
    w0, w1, w2 = w[0:1, :], w[1:2, :], w[2:3, :]
    return o_prev * w0 + e * w1 + o * w2 + b, e * w0 + o * w1 + e_next * w2 + b


HYENA_ROW_SPLIT = 2


def _hyena_order_kernel(z_ref, x_ref, ce_ref, se_ref, co_ref, so_ref, cot_ref, sot_ref, alo_ref, blo_ref, ahi_ref,
                        bhi_ref, mid_ref, skip_ref, wx_ref, bx_ref, *rest, order, first):
    if first:
        wz_ref, bz0_ref, o_ref = rest
        ze, zo = _short_conv(z_ref[0], z_ref[1], wz_ref[...], bz0_ref[...])
    else:
        (o_ref,) = rest
        ze, zo = z_ref[0], z_ref[1]
    half = ze.shape[0]
    groups = [slice(r, r + half // HYENA_ROW_SPLIT) for r in range(0, half, half // HYENA_ROW_SPLIT)]
    zeb, zob = ze.astype(BF16), zo.astype(BF16)
    prods = []
    for rs in groups:
        re_lo, im_lo, re_hi, im_hi = _half_spectrum(zeb, zob, ce_ref[rs, :], se_ref[rs, :], co_ref[rs, :],
                                                    so_ref[rs, :])
        alo, blo, ahi, bhi = alo_ref[rs, :], blo_ref[rs, :], ahi_ref[rs, :], bhi_ref[rs, :]
        wr_lo, wi_lo = re_lo * alo + im_lo * blo, re_lo * blo - im_lo * alo
        wr_hi, wi_hi = re_hi * ahi + im_hi * bhi, re_hi * bhi - im_hi * ahi
        prods.append(((wr_lo + wr_hi).astype(BF16), (wi_lo - wi_hi).astype(BF16),
                      (wr_lo - wr_hi).astype(BF16), (wi_lo + wi_hi).astype(BF16)))
    p_even_c, p_even_s, p_odd_c, p_odd_s = (jnp.concatenate(p, axis=0) for p in zip(*prods))
    me, mo = _short_conv(x_ref[0], x_ref[1], wx_ref[...], bx_ref[...])
    re_mid, im_mid = _alt_sum(ze), _alt_sum(zo)
    a_mid, b_mid = mid_ref[0:1, :], mid_ref[1:2, :]
    mid_e, mid_o = re_mid * a_mid + im_mid * b_mid, re_mid * b_mid - im_mid * a_mid
    skip = skip_ref[order:order + 1, :]
    for rs in groups:
        conv_e = _dot(ce_ref[rs, :], p_even_c) - _dot(se_ref[rs, :], p_even_s)
        conv_o = _dot(cot_ref[rs, :], p_odd_c) - _dot(sot_ref[rs, :], p_odd_s)
        conv_e = conv_e + _alt_rows(mid_e, conv_e.shape)
        conv_o = conv_o - _alt_rows(mid_o, conv_o.shape)
        o_ref[0, rs, :] = (me[rs] * (conv_e + ze[rs] * skip)).astype(o_ref.dtype)
        o_ref[1, rs, :] = (mo[rs] * (conv_o + zo[rs] * skip)).astype(o_ref.dtype)


def _hyena_order(z4, z_blk0, x4, x_blk0, tabs, coefs, skip, conv_w, conv_b, *, order, first, out_dtype, cb=256):
    nb, _, half, _ = x4.shape
    ncb = D_MIX // cb
    xw0 = (1 + order) * ncb
    conv_b = conv_b.reshape(1, -1)
    tok = lambda blk0: pl.BlockSpec((None, 2, half, cb), lambda j, b: (b, 0, 0, blk0 + j))
    colblk = lambda rows, blk0: pl.BlockSpec((rows, cb), lambda j, b: (0, blk0 + j), pipeline_mode=pl.Buffered(1))
    in_specs = [tok(z_blk0), tok(x_blk0)] + [_const_spec((half, half))] * 6
    in_specs += [colblk(half, order * ncb)] * 4 + [colblk(2, order * ncb), colblk(2, 0), colblk(3, xw0), colblk(1, xw0)]
    args = [z4, x4, *tabs, *coefs, skip, conv_w, conv_b]
    if first:
        in_specs += [colblk(3, 0), colblk(1, 0)]
        args += [conv_w, conv_b]
    return pl.pallas_call(
        functools.partial(_hyena_order_kernel, order=order, first=first),
        grid=(ncb, nb),
        in_specs=in_specs,
        out_specs=pl.BlockSpec((None, 2, half, cb), lambda j, b: (b, 0, 0, j)),
        out_shape=jax.ShapeDtypeStruct((nb, 2, half, D_MIX), out_dtype),
        compiler_params=_params("parallel", "parallel"),
        name="hyena_order",
    )(*args)


def _dit_tables(half, period):
    split = math.gcd(half, 64)

    def part(rows, mult, col_off):
        r = lax.broadcasted_iota(jnp.int32, (rows, half), 0) * mult
        c = 2 * lax.broadcasted_iota(jnp.int32, (rows, half), 1) + col_off
        ang = ((r * c) % period).astype(F32) * (2.0 * math.pi / period)
        return jnp.cos(ang), jnp.sin(ang)

    out = []
    for col_off in (0, 1):
        ca, sa = (t[:, None, :] for t in part(half // split, split, col_off))
        cb, sb = (t[None, :, :] for t in part(split, 1, col_off))
        out += [(ca * cb - sa * sb).reshape(half, half), (sa * cb + ca * sb).reshape(half, half)]
    return out


def _fourier_tables(seq):
    ce, se, co, so = _dit_tables(seq // 2, seq)
    te = jnp.concatenate([ce, -se], axis=1).astype(BF16)
    to = jnp.concatenate([co, -so], axis=1).astype(BF16)
    gsz = D_FOURIER // FOURIER_GROUPS
    r = lax.broadcasted_iota(jnp.int32, (gsz, gsz), 0)
    c = lax.broadcasted_iota(jnp.int32, (gsz, gsz), 1)
    ang = ((r * c) % gsz).astype(F32) * (2.0 * math.pi / gsz)
    eye = jnp.eye(FOURIER_GROUPS, dtype=F32)
    cg = jnp.concatenate([jnp.kron(eye, jnp.cos(ang)), jnp.kron(eye, jnp.sin(ang))], axis=1).astype(BF16)
    return te, to, cg


def _hyena_tables(seq):
    ce, se, co, so = (t.astype(BF16) for t in _dit_tables(seq // 2, 2 * seq))
    return ce, se, co, so, co.T, so.T


def _rms_heads(x, g):
    parts = []
    for h in range(D_MIX // LANE):
        xh = x[:, h * LANE:(h + 1) * LANE]
        parts.append(xh * lax.rsqrt(jnp.mean(xh * xh, axis=-1, keepdims=True) + LN_EPS) * g)
    return jnp.concatenate(parts, axis=-1)


def _outproj_kernel(x_ref, ya_ref, yb_ref, *rest, tm, seq, is_ctx, readout):
    if readout:
        r_ref, gd_ref, gg_ref, hg_ref, unperm_ref, mod_ref, w_ref, g_ref, b_ref, o_ref = rest
        r = r_ref[...]
        ya = (_rms_heads(ya_ref[...], gg_ref[...]) * (r * _sigmoid(r))).astype(BF16)
        yb = _rms_heads(yb_ref[...].reshape(tm, D_MIX) * _sigmoid(gd_ref[...].reshape(tm, D_MIX)), hg_ref[...])
        yb = _dot(unperm_ref[...], yb.astype(BF16)).astype(BF16)
    else:
        unperm_ref, mod_ref, w_ref, g_ref, b_ref, o_ref = rest
        ya = ya_ref[...]
        half = DEINT_BLOCK // 2
        parts = []
        for r in range(0, tm // 2, half):
            blk = jnp.concatenate([yb_ref[0, r:r + half, :], yb_ref[1, r:r + half, :]], axis=0)
            parts.append(_dot(unperm_ref[...], blk).astype(BF16))
        yb = parts[0] if len(parts) == 1 else jnp.concatenate(parts, axis=0)
    row = _mod_row_index(pl.program_id(0), tm, seq, is_ctx)
    gate = _mod_row(mod_ref, row, 5)
    y = _dot(ya, w_ref[:D_MIX, :]) + _dot(yb, w_ref[D_MIX:, :])
    z = ALPHA * x_ref[...] + gate * y
    o_ref[...] = _ln(z) * g_ref[...] + b_ref[...]


def _outproj(x2, ya, yb, mod, layer, w, ln_g, ln_b, *, seq, is_ctx, readout=None):
    n = x2.shape[0]
    tm = _token_tile(n, seq, False, want=TOKEN_TILE if readout is None else TOKEN_TILE // 2)
    tiles = seq // tm
    tok = lambda: pl.BlockSpec((tm, D_MIX), lambda i: (i, 0))
    in_specs = [pl.BlockSpec((tm, D_MODEL), lambda i: (i, 0)), tok()]
    args = [x2, ya, yb]
    if readout is None:
        assert tm % DEINT_BLOCK == 0
        unperm = _deint_perm(inverse=True)
        in_specs += [pl.BlockSpec((None, 2, tm // 2, D_MIX), lambda i: (i // tiles, 0, i % tiles, 0)),
                     _const_spec(unperm.shape)]
        args += [unperm]
    else:
        p_main, p_hg, gla_g, hg_g = readout
        rows = tm // GRID_W
        assert tm % GRID_W == 0 and rows % 8 == 0
        i = lax.broadcasted_iota(jnp.int32, (tm, tm), 0)
        j = lax.broadcasted_iota(jnp.int32, (tm, tm), 1)
        unperm = jnp.where(j == (i % GRID_W) * rows + i // GRID_W, 1.0, 0.0).astype(BF16)
        colmajor = lambda blk: pl.BlockSpec((None, GRID_W, rows, D_MIX), lambda i: (i // tiles, 0, i % tiles, blk))
        in_specs += [colmajor(0), pl.BlockSpec((tm, D_MIX), lambda i: (i, PM_R)), colmajor(PH_GD),
                     _const_spec((1, LANE)), _const_spec((1, LANE)), _const_spec((tm, tm))]
        args += [p_main, p_hg, gla_g.reshape(1, LANE), hg_g.reshape(1, LANE), unperm]
    in_specs += [_picked_spec(mod.shape, (layer,)), _const_spec(w.shape),
                 _picked_spec(ln_g.shape, (layer, 1)), _picked_spec(ln_b.shape, (layer, 1))]
    args += [mod, w, ln_g, ln_b]
    kern = functools.partial(_outproj_kernel, tm=tm, seq=seq, is_ctx=is_ctx, readout=readout is not None)
    return pl.pallas_call(
        kern,
        grid=(n // tm,),
        in_specs=in_specs,
        out_specs=pl.BlockSpec((tm, D_MODEL), lambda i: (i, 0)),
        out_shape=jax.ShapeDtypeStruct((n, D_MODEL), F32),
        compiler_params=_params("parallel"),
        name="mixer_out",
    )(*args)


def _split3(x):
    hi = x.astype(BF16)
    r1 = x - hi.astype(F32)
    mid = r1.astype(BF16)
    lo = (r1 - mid.astype(F32)).astype(BF16)
    return hi, mid, lo


SCAN_UNROLL = 8
CUMSUM_CHUNKS = 4


def _scan_unroll(n_chunks):
    return math.gcd(SCAN_UNROLL, n_chunks)


def _scan_consts(rev, unroll, heads, dk):
    n = math.gcd(unroll, CUMSUM_CHUNKS) * CHUNK
    i = lax.broadcasted_iota(jnp.int32, (n, n), 0)
    j = lax.broadcasted_iota(jnp.int32, (n, n), 1)
    same_chunk = (i // CHUNK) == (j // CHUNK)
    tri = jnp.where(same_chunk & ((j >= i) if rev else (j <= i)), 1.0, 0.0).astype(BF16)
    r = lax.broadcasted_iota(jnp.int32, (CHUNK, heads * CHUNK), 0)
    c = lax.broadcasted_iota(jnp.int32, (CHUNK, heads * CHUNK), 1) & (CHUNK - 1)
    causal = (c >= r) if rev else (c <= r)
    klane_head = lax.broadcasted_iota(jnp.int32, (1, heads * dk), 1) // dk
    return tri, causal, klane_head


def _block_diag(blocks):
    z = jnp.zeros_like(blocks[0])
    return jnp.concatenate([jnp.concatenate([b if c == h else z for c in range(len(blocks))], axis=1)
                            for h, b in enumerate(blocks)], axis=0)


def _scan_local(q, k, g, vb, consts, *, rev, heads, dk, need_out):
    tri, causal, klane_head = consts
    kt = heads * dk
    parts = _split3(g)
    span = tri.shape[0]
    gsum = jnp.concatenate([sum(_dot(tri, part[r:r + span]) for part in parts) for r in range(0, g.shape[0], span)],
                           axis=0)
    staged = []
    for u in range(g.shape[0] // CHUNK):
        rows = slice(u * CHUNK, (u + 1) * CHUNK)
        gs = gsum[rows]
        gt = gs[0:1] if rev else gs[CHUNK - 1:CHUNK]
        ku, vu = k[rows], vb[rows]
        q_in = o_intra = None
        if need_out:
            q_in = (q[rows] * jnp.exp(gs)).astype(BF16)
            k_dec = ku * jnp.exp(-gs)
            k_in = k_dec.astype(BF16)
            k_out = (k_dec * jnp.exp(gt)).astype(BF16)
            if dk % LANE == 0:
                k4 = _block_diag([k_in[:, h * dk:(h + 1) * dk] for h in range(heads)])
            else:
                k4 = jnp.concatenate([jnp.where(klane_head == h, k_in, jnp.zeros_like(k_in)) for h in range(heads)],
                                     axis=0)
            att = jnp.where(causal, _dot_nt(q_in, k4), 0.0).astype(BF16)
            o_intra = _dot(att, _block_diag([vu[:, h * LANE:(h + 1) * LANE] for h in range(heads)]))
        else:
            k_out = (ku * jnp.exp(gt - gs)).astype(BF16)
        if dk % LANE == 0:
            inc = [_dot_tn(k_out[:, h * dk:(h + 1) * dk], vu[:, h * LANE:(h + 1) * LANE]) for h in range(heads)]
        else:
            k_t = jnp.transpose(jnp.concatenate([k_out.astype(F32), jnp.zeros((LANE - CHUNK, kt), F32)], axis=0))
            k_t = k_t.astype(BF16)
            v_pad = jnp.concatenate([vu, jnp.zeros((LANE - CHUNK, heads * LANE), BF16)], axis=0)
            inc = [_dot(k_t[h * dk:(h + 1) * dk, :], v_pad[:, h * LANE:(h + 1) * LANE]) for h in range(heads)]
        dec = jnp.transpose(jnp.broadcast_to(jnp.exp(gt), (LANE, kt)))
        staged.append((q_in, o_intra, inc, [dec[h * dk:(h + 1) * dk] for h in range(heads)]))
    return staged


def _scan_states(s_ref, base, staged):
    heads = len(staged[0][2])
    state = [s_ref[base + h] for h in range(heads)]
    outs = []
    for q_in, o_intra, inc, decay in staged:
        if q_in is not None:
            outs.append(o_intra + _dot(q_in, _block_diag([s.astype(BF16) for s in state])))
        state = [s * d + i for s, d, i in zip(state, decay, inc)]
    for h in range(heads):
        s_ref[base + h] = state[h]
    return outs


def _gla_kernel(qk_ref, v_ref, g_ref, cqk_ref, cv_ref, cg_ref, o_ref, s_ref, ob_ref, *, seq, ctx_len):
    heads, dk = GLA_HEADS, GLA_DK
    kt = heads * dk
    s_ref[...] = jnp.zeros_like(s_ref)

    def make_run(qk, vv, gg, n_chunks, need_out):
        unroll = _scan_unroll(n_chunks)
        rows = unroll * CHUNK
        consts = [_scan_consts(rev, unroll, heads, dk) for rev in (False, True)]

        def run(n):
            work = []
            for d, rev in enumerate((False, True)):
                start = (n_chunks * CHUNK - rows - n * rows) if rev else n * rows
                sl = pl.ds(pl.multiple_of(start, rows), rows)
                q = qk[sl, 0:kt] * (dk ** -0.5) if need_out else None
                staged = _scan_local(q, qk[sl, kt:2 * kt], gg[sl, d * kt:(d + 1) * kt], vv[sl, :].astype(BF16),
                                     consts[d], rev=rev, heads=heads, dk=dk, need_out=need_out)
                work.append((d, rev, sl, staged[::-1] if rev else staged))
            for d, rev, sl, staged in work:
                outs = _scan_states(s_ref, d * heads, staged)
                if need_out:
                    (ob_ref if rev else o_ref)[sl, :] = jnp.concatenate(outs[::-1] if rev else outs, axis=0)

        return run, n_chunks // unroll

    run, steps = make_run(cqk_ref, cv_ref, cg_ref, ctx_len // CHUNK, False)
    for n in range(steps):
        run(n)
    run, steps = make_run(qk_ref, v_ref, g_ref, seq // CHUNK, True)

    def body(n, carry):
        run(n)
        return carry

    lax.fori_loop(0, steps, body, 0)
    o_ref[...] = o_ref[...] + ob_ref[...]


def _gla(p3, pc3):
    nb, seq, _ = p3.shape
    ctx_len = pc3.shape[1]
    blk = lambda length, j: pl.BlockSpec((None, length, D_MIX), lambda b: (b, 0, j))
    return pl.pallas_call(
        functools.partial(_gla_kernel, seq=seq, ctx_len=ctx_len),
        grid=(nb,),
        in_specs=[blk(seq, PM_QK), blk(seq, PM_V), blk(seq, PM_GATE),
                  blk(ctx_len, P_QK), blk(ctx_len, P_V), blk(ctx_len, P_GATE)],
        out_specs=pl.BlockSpec((None, seq, D_MIX), lambda b: (b, 0, 0)),
        out_shape=jax.ShapeDtypeStruct((nb, seq, D_MIX), F32),
        scratch_shapes=[pltpu.VMEM((2 * GLA_HEADS, GLA_DK, GLA_DV), F32),
                        pltpu.VMEM((seq, D_MIX), F32)],
        compiler_params=_params("parallel"),
        name="gla_scan",
    )(p3, p3, p3, pc3, pc3, pc3)


def _hgrn_kernel(q_ref, ff_ref, fb_ref, v_ref, cff_ref, cfb_ref, cv_ref, o_ref, s_ref, *, seq, ctx_len):
    heads, dk = HG_HEADS, HG_D
    s_ref[...] = jnp.zeros_like(s_ref)

    def make_run(qq, gates, vv, n_chunks, need_out):
        unroll = _scan_unroll(n_chunks)
        rows = unroll * CHUNK
        consts = [_scan_consts(rev, unroll, heads, dk) for rev in (False, True)]

        def run(n):
            work = []
            for d, rev in enumerate((False, True)):
                start = (n_chunks * CHUNK - rows - n * rows) if rev else n * rows
                sl = pl.ds(pl.multiple_of(start, rows), rows)
                f = gates[d][sl, :]
                staged = _scan_local(qq[sl, :] if need_out else None, 1.0 - f, jnp.log(f), vv[sl, :].astype(BF16), consts[d],
                                     rev=rev, heads=heads, dk=dk, need_out=need_out)
                work.append((d, rev, sl, staged[::-1] if rev else staged))
            for d, rev, sl, staged in work:
                outs = _scan_states(s_ref, d * heads, staged)
                if need_out:
                    o_ref[sl, :] += jnp.concatenate(outs[::-1] if rev else outs, axis=0)

        return run, n_chunks // unroll

    run, steps = make_run(None, (cff_ref, cfb_ref), cv_ref, ctx_len // CHUNK, False)
    for n in range(steps):
        run(n)
    o_ref[...] = jnp.zeros_like(o_ref)
    run, steps = make_run(q_ref, (ff_ref, fb_ref), v_ref, seq // CHUNK, True)

    def body(n, carry):
        run(n)
        return carry

    lax.fori_loop(0, steps, body, 0)


def _hgrn(ph3, pc3):
    nb, seq, _ = ph3.shape
    ctx_len = pc3.shape[1]
    blk = lambda length, j: pl.BlockSpec((None, length, D_MIX), lambda b: (b, 0, j))
    return pl.pallas_call(
        functools.partial(_hgrn_kernel, seq=seq, ctx_len=ctx_len),
        grid=(nb,),
        in_specs=[blk(seq, PH_QD), blk(seq, PH_FFL), blk(seq, PH_FBL), blk(seq, PH_IDD),
                  blk(ctx_len, P_FFL), blk(ctx_len, P_FBL), blk(ctx_len, P_IDD)],
        out_specs=pl.BlockSpec((None, seq, D_MIX), lambda b: (b, 0, 0)),
        out_shape=jax.ShapeDtypeStruct((nb, seq, D_MIX), F32),
        scratch_shapes=[pltpu.VMEM((2 * HG_HEADS, HG_D, HG_D), F32)],
        compiler_params=_params("parallel"),
        name="hgrn_scan",
    )(ph3, ph3, ph3, ph3, pc3, pc3, pc3)


def _even_mixer(x2, nb, seq, mod, layer, is_ctx, w_in, w_out, conv_w, conv_b, filt, skip, ln_g, ln_b):
    p4 = _proj_deint(x2, nb, seq, mod, layer, w_in, is_ctx=is_ctx)
    te, to, cg = _fourier_tables(seq)
    ya = _fourier(p4, cg, te, to, tl=min(seq // 2, 512))
    tabs = _hyena_tables(seq)
    fwd, bwd = _hyena_filters(seq, *filt)
    coefs = _hyena_spectra(tabs, fwd, bwd)
    cb = 256
    u0 = D_FOURIER // cb
    per = D_HYENA // cb
    y1 = _hyena_order(p4, u0, p4, u0 + per, tabs, coefs, skip, conv_w, conv_b, order=0, first=True, out_dtype=F32,
                      cb=cb)
    y2 = _hyena_order(y1, 0, p4, u0 + 2 * per, tabs, coefs, skip, conv_w, conv_b, order=1, first=False,
                      out_dtype=BF16, cb=cb)
    return _outproj(x2, ya, y2, mod, layer, w_out, ln_g, ln_b, seq=seq, is_ctx=is_ctx)


def _odd_proj_weights(w_in, a_up, a_b):
    kq = GLA_HEADS * GLA_DK
    low0 = 2 * kq + 2 * D_MIX
    low1 = low0 + 2 * GLA_RANK
    w = jnp.concatenate([w_in[:, :low0], w_in[:, low1:], w_in[:, low0:low1],
                         jnp.zeros((D_MODEL, LANE - 2 * GLA_RANK), w_in.dtype)], axis=1).astype(BF16)
    up = jnp.zeros((LANE, 2 * kq), F32)
    up = up.at[0:GLA_RANK, 0:kq].set(a_up[0]).at[GLA_RANK:2 * GLA_RANK, kq:].set(a_up[1]).astype(BF16)
    return w, up, a_b.reshape(1, 2 * kq)


def kernel(x, c, ctx, c_ctx, mod_w, mod_b, ffn_w_in, ffn_w_out, ln_g, ln_b, ev_w_in, ev_w_out, hy_conv_w, hy_conv_b, hy_w1, hy_b1, hy_w2, hy_b2, hy_w3, hy_freq, hy_skip, od_w_in, od_w_out, gla_a_up, gla_a_b, gla_norm_g, hg_lb, hg_norm_g):
    nb, seq, d = x.shape
    ctx_len = ctx.shape[1]
    assert d == D_MODEL and seq % 512 == 0 and ctx_len % CHUNK == 0 and nb + 1 <= MOD_ROWS - 7
    assert mod_w.shape[0] == DEPTH == 2

    c_all = jnp.zeros((MOD_ROWS, D_MODEL), F32).at[:nb].set(c).at[MOD_ROWS - 8].set(c_ctx)
    mod = _mod_table(c_all, mod_w, mod_b)
    w_in_b = ffn_w_in.astype(BF16)
    w_out_b = ffn_w_out.astype(BF16)

    xs = x.reshape(nb * seq, D_MODEL)
    xc = ctx.reshape(nb * ctx_len, D_MODEL)
    streams = [(xs, seq, False), (xc, ctx_len, True)]

    ln_g4 = ln_g.reshape(DEPTH, 3, 1, D_MODEL)
    ln_b4 = ln_b.reshape(DEPTH, 3, 1, D_MODEL)

    def ffn(stream, layer, half):
        arr, length, is_ctx = stream
        return (_ffn(arr, mod, layer, half, w_in_b, w_out_b, ln_g4, ln_b4, seq=length, is_ctx=is_ctx), length, is_ctx)

    streams = [ffn(s, 0, 0) for s in streams]
    filt = (hy_w1[0], hy_b1[0], hy_w2[0], hy_b2[0], hy_w3[0], hy_freq[0])
    ev_in_b, ev_out_b = ev_w_in[0].astype(BF16), ev_w_out[0].astype(BF16)
    streams = [(_even_mixer(arr, nb, length, mod, 0, is_ctx, ev_in_b, ev_out_b, hy_conv_w[0], hy_conv_b[0], filt,
                            hy_skip[0], ln_g4, ln_b4), length, is_ctx) for arr, length, is_ctx in streams]
    streams = [ffn(s, 0, 1) for s in streams]

    streams = [ffn(s, 1, 0) for s in streams]
    (xs, _, _), (xc, _, _) = streams
    w_proj, up, up_b = _odd_proj_weights(od_w_in[0], gla_a_up[0], gla_a_b[0])
    n_gla, n_hg = (P_R + 1) * D_MIX, (PH_GD + 1) * D_MIX
    w_main = jnp.concatenate([w_proj[:, :n_gla], w_proj[:, n_gla + n_hg:]], axis=1)
    pc = _proj(xc, mod, 1, w_proj, seq=ctx_len, is_ctx=True, gates=(up, up_b), forget=(hg_lb, P_FFL * D_MIX),
               want=TOKEN_TILE // 2)
    pm = _proj(xs, mod, 1, w_main, seq=seq, is_ctx=False, gates=(up, up_b))
    ph = _proj_colmajor(xs, nb, seq, mod, 1, w_proj[:, n_gla:n_gla + n_hg], hg_lb)
    pc3 = pc.reshape(nb, ctx_len, -1)
    o_gla = _gla(pm.reshape(nb, seq, -1), pc3).reshape(nb * seq, D_MIX)
    o_hg = _hgrn(ph.reshape(nb, seq, -1), pc3).reshape(ph.shape[:3] + (D_MIX,))
    xs = _outproj(xs, o_gla, o_hg, mod, 1, od_w_out[0].astype(BF16), ln_g4, ln_b4, seq=seq, is_ctx=False,
                  readout=(pm, ph, gla_norm_g[0], hg_norm_g[0]))
    xs, _, _ = ffn((xs, seq, False), 1, 1)
    return xs.reshape(nb, seq, D_MODEL)
```

```python
import functools
import math

import jax
import jax.numpy as jnp
from jax import lax
from jax.experimental import pallas as pl
from jax.experimental.pallas import tpu as pltpu

F32 = jnp.float32
BF16 = jnp.bfloat16

D_MODEL = 1024
DEPTH = 2
N_MOD = 9
D_FF = 2816
LN_EPS = 1e-6
ALPHA = (2.0 * DEPTH) ** 0.25
GRID_W = 64
D_FOURIER = 512
FOURIER_GROUPS = 8
D_HYENA = 512
HY_EMB = 33
HY_HID = 64
HY_FAST_DECAY = 0.3
HY_SLOW_DECAY = 1.5
HY_TARGET = 1e-2
CHUNK = 64
GLA_HEADS = 4
GLA_DK = 64
GLA_DV = 128
GLA_RANK = 16
GLA_TAU = 16.0
HG_HEADS = 4
HG_D = 128
D_MIX = 512

LANE = 128
MOD_ROWS = 24
VMEM_LIMIT = 56 * 1024 * 1024

P_QK, P_V, P_R, P_QD, P_FFL, P_FBL, P_IDD, P_GD, P_GATE = range(9)
PM_QK, PM_V, PM_R, PM_GATE = range(4)
PH_QD, PH_FFL, PH_FBL, PH_IDD, PH_GD = range(5)


def _params(*sem):
    return pltpu.CompilerParams(dimension_semantics=sem, vmem_limit_bytes=VMEM_LIMIT)


def _dot(a, b):
    return jnp.dot(a, b, preferred_element_type=F32)


def _dot_nt(a, b):
    return lax.dot_general(a, b, (((1,), (1,)), ((), ())), preferred_element_type=F32)


def _dot_tn(a, b):
    return lax.dot_general(a, b, (((0,), (0,)), ((), ())), preferred_element_type=F32)


def _ln(x):
    mu = jnp.mean(x, axis=-1, keepdims=True)
    xc = x - mu
    var = jnp.mean(xc * xc, axis=-1, keepdims=True)
    return xc * lax.rsqrt(var + LN_EPS)


def _sigmoid(x):
    return 1.0 / (1.0 + jnp.exp(-x))


def _mod_row(mod_ref, row, k):
    return mod_ref[pl.ds(row, 1), k * D_MODEL:(k + 1) * D_MODEL]


def _mod_row_index(tile, tm, seq, is_ctx):
    if is_ctx:
        return MOD_ROWS - 8
    return (tile * tm) // seq


def _const_spec(shape):
    return pl.BlockSpec(shape, lambda *_: (0,) * len(shape), pipeline_mode=pl.Buffered(1))


def _picked_spec(shape, lead):
    tail = tuple(shape[len(lead):])
    return pl.BlockSpec((None,) * len(lead) + tail, lambda *_: tuple(lead) + (0,) * len(tail),
                        pipeline_mode=pl.Buffered(1))


TOKEN_TILE = 1024
SUB_TILES = 2


def _token_tile(n, seq, is_ctx, want=TOKEN_TILE):
    tm = min(want, n if is_ctx else seq)
    while n % tm or (not is_ctx and seq % tm):
        tm //= 2
    return tm


def _mod_kernel(c_ref, w_ref, b_ref, o_ref):
    c = c_ref[...]
    s = (c * _sigmoid(c)).astype(BF16)
    o_ref[...] = _dot(s, w_ref[...].astype(BF16)) + b_ref[...]


def _mod_table(c_all, mod_w, mod_b):
    return pl.pallas_call(
        _mod_kernel,
        grid=(DEPTH, N_MOD),
        in_specs=[
            _const_spec((MOD_ROWS, D_MODEL)),
            pl.BlockSpec((None, D_MODEL, D_MODEL), lambda l, j: (l, 0, j)),
            pl.BlockSpec((None, 1, D_MODEL), lambda l, j: (l, 0, j)),
        ],
        out_specs=pl.BlockSpec((None, MOD_ROWS, D_MODEL), lambda l, j: (l, 0, j)),
        out_shape=jax.ShapeDtypeStruct((DEPTH, MOD_ROWS, N_MOD * D_MODEL), F32),
        compiler_params=_params("parallel", "parallel"),
        name="mod_table",
    )(c_all, mod_w, mod_b.reshape(DEPTH, 1, N_MOD * D_MODEL))


FFN_CHUNK = 256


def _swiglu(xm, win_ref, wout_ref):
    acc = None
    for j in range(D_FF // FFN_CHUNK):
        lo = j * FFN_CHUNK
        a = _dot(xm, win_ref[:, lo:lo + FFN_CHUNK])
        u = _dot(xm, win_ref[:, D_FF + lo:D_FF + lo + FFN_CHUNK])
        h = (a * _sigmoid(a) * u).astype(BF16)
        y = _dot(h, wout_ref[lo:lo + FFN_CHUNK, :])
        acc = y if acc is None else acc + y
    return acc


def _ffn_kernel(x_ref, mod_ref, win_ref, wout_ref, g_ref, b_ref, o_ref, *, k0, tm, seq, is_ctx):
    row = _mod_row_index(pl.program_id(0), tm, seq, is_ctx)
    shift, scale, gate = (_mod_row(mod_ref, row, k0 + t) for t in range(3))
    subs = [slice(r, r + tm // SUB_TILES) for r in range(0, tm, tm // SUB_TILES)]
    xms = [(_ln(x_ref[s, :]) * (1.0 + scale) + shift).astype(BF16) for s in subs]
    accs = [_swiglu(xm, win_ref, wout_ref) for xm in xms]
    for s, acc in zip(subs, accs):
        z = ALPHA * x_ref[s, :] + (0.5 * gate) * acc
        o_ref[s, :] = _ln(z) * g_ref[...] + b_ref[...]


def _ffn(x2, mod, layer, half, w_in, w_out, ln_g, ln_b, *, seq, is_ctx):
    n = x2.shape[0]
    tm = _token_tile(n, seq, is_ctx)
    kern = functools.partial(_ffn_kernel, k0=6 * half, tm=tm, seq=seq, is_ctx=is_ctx)
    return pl.pallas_call(
        kern,
        grid=(n // tm,),
        in_specs=[
            pl.BlockSpec((tm, D_MODEL), lambda i: (i, 0)),
            _picked_spec(mod.shape, (layer,)),
            _picked_spec(w_in.shape, (layer, half)),
            _picked_spec(w_out.shape, (layer, half)),
            _picked_spec(ln_g.shape, (layer, 2 * half)),
            _picked_spec(ln_b.shape, (layer, 2 * half)),
        ],
        out_specs=pl.BlockSpec((tm, D_MODEL), lambda i: (i, 0)),
        out_shape=jax.ShapeDtypeStruct((n, D_MODEL), F32),
        compiler_params=_params("parallel"),
        name="ffn",
    )(x2, mod, w_in, w_out, ln_g, ln_b)


def _log_sigmoid(x):
    return jnp.minimum(x, 0.0) - jnp.log(1.0 + jnp.exp(-jnp.abs(x)))


def _forget_gates(lb_ref, layer, logits):
    raw = lb_ref[...]
    e = jnp.exp(raw - jnp.max(raw, axis=0, keepdims=True))
    sm = e / jnp.sum(e, axis=0, keepdims=True)
    lb = jnp.sum(sm[0:layer + 1, :], axis=0, keepdims=True) - sm[0:1, :]
    lb = jnp.concatenate([lb, lb], axis=-1)
    return lb + (1.0 - lb) * _sigmoid(logits)


def _proj_kernel(x_ref, mod_ref, w_ref, *rest, tm, seq, is_ctx, gla_gates, forget):
    row = _mod_row_index(pl.program_id(0), tm, seq, is_ctx)
    shift, scale = _mod_row(mod_ref, row, 3), _mod_row(mod_ref, row, 4)
    subs = [slice(r, r + tm // SUB_TILES) for r in range(0, tm, tm // SUB_TILES)]
    xms = [(_ln(x_ref[s, :]) * (1.0 + scale) + shift).astype(BF16) for s in subs]
    ps = [_dot(xm, w_ref[...]) for xm in xms]
    if not gla_gates:
        (o_ref,) = rest
        for s, p in zip(subs, ps):
            o_ref[s, :] = p
        return
    aup_ref, ab_ref, *lb_ref, o_ref = rest
    for s, p in zip(subs, ps):
        main = p.shape[1] - LANE
        o_ref[s, :main] = p[:, :main]
        if forget is not None:
            layer, c0 = forget
            o_ref[s, c0:c0 + 2 * D_MIX] = _forget_gates(lb_ref[0], layer, p[:, c0:c0 + 2 * D_MIX])
        low = p[:, main:].astype(BF16)
        o_ref[s, main:] = _log_sigmoid(_dot(low, aup_ref[...]) + ab_ref[...]) * (1.0 / GLA_TAU)


PERM_COLS = 8


def _proj_colmajor_kernel(x_ref, mod_ref, perm_ref, w_ref, lb_ref, o_ref, *, layer):
    rows, cols, _ = x_ref.shape
    row = pl.program_id(0)
    shift, scale = _mod_row(mod_ref, row, 3), _mod_row(mod_ref, row, 4)
    per_sub = cols // SUB_TILES
    for s in range(SUB_TILES):
        parts = []
        for c in range(s * per_sub, (s + 1) * per_sub, PERM_COLS):
            x = x_ref[:, c:c + PERM_COLS, :].reshape(rows * PERM_COLS, D_MODEL)
            xm = (_ln(x) * (1.0 + scale) + shift).astype(BF16)
            parts.append(_dot(perm_ref[...], xm).astype(BF16))
        p = _dot(jnp.concatenate(parts, axis=0), w_ref[...])
        c0, c1 = PH_FFL * D_MIX, (PH_FBL + 1) * D_MIX
        p = jnp.concatenate([p[:, :c0], _forget_gates(lb_ref, layer, p[:, c0:c1]), p[:, c1:]], axis=-1)
        o_ref[s * per_sub:(s + 1) * per_sub] = p.reshape(per_sub, rows, w_ref.shape[1])


def _proj_colmajor(x2, nb, seq, mod, layer, w, hg_lb, *, cols=32):
    rows = seq // GRID_W
    n_out = w.shape[1]
    n = rows * PERM_COLS
    i = lax.broadcasted_iota(jnp.int32, (n, n), 0)
    j = lax.broadcasted_iota(jnp.int32, (n, n), 1)
    perm = jnp.where(j == (i % rows) * PERM_COLS + i // rows, 1.0, 0.0).astype(BF16)
    return pl.pallas_call(
        functools.partial(_proj_colmajor_kernel, layer=layer),
        grid=(nb, GRID_W // cols),
        in_specs=[pl.BlockSpec((None, rows, cols, D_MODEL), lambda b, c: (b, 0, c, 0)),
                  _picked_spec(mod.shape, (layer,)), _const_spec(perm.shape), _const_spec(w.shape),
                  _const_spec(hg_lb.shape)],
        out_specs=pl.BlockSpec((None, cols, rows, n_out), lambda b, c: (b, c, 0, 0)),
        out_shape=jax.ShapeDtypeStruct((nb, GRID_W, rows, n_out), F32),
        compiler_params=_params("parallel", "parallel"),
        name="mixer_proj_colmajor",
    )(x2.reshape(nb, rows, GRID_W, D_MODEL), mod, perm, w, hg_lb)


def _proj(x2, mod, layer, w, *, seq, is_ctx, gates=None, forget=None, want=TOKEN_TILE):
    n = x2.shape[0]
    tm = _token_tile(n, seq, is_ctx, want)
    n_out = w.shape[1] - LANE + gates[0].shape[1] if gates is not None else w.shape[1]
    kern = functools.partial(_proj_kernel, tm=tm, seq=seq, is_ctx=is_ctx, gla_gates=gates is not None,
                             forget=None if forget is None else (layer, forget[1]))
    extra = [] if gates is None else list(gates)
    if forget is not None:
        extra.append(forget[0])
    return pl.pallas_call(
        kern,
        grid=(n // tm,),
        in_specs=[pl.BlockSpec((tm, D_MODEL), lambda i: (i, 0)), _picked_spec(mod.shape, (layer,)),
                  _const_spec(w.shape)] + [_const_spec(e.shape) for e in extra],
        out_specs=pl.BlockSpec((tm, n_out), lambda i: (i, 0)),
        out_shape=jax.ShapeDtypeStruct((n, n_out), F32),
        compiler_params=_params("parallel"),
        name="mixer_proj",
    )(x2, mod, w, *extra)


DEINT_BLOCK = 256


def _deint_perm(inverse=False):
    i = lax.broadcasted_iota(jnp.int32, (DEINT_BLOCK, DEINT_BLOCK), 0)
    j = lax.broadcasted_iota(jnp.int32, (DEINT_BLOCK, DEINT_BLOCK), 1)
    half = DEINT_BLOCK // 2
    src = 2 * (i % half) + i // half
    hit = (i == 2 * (j % half) + j // half) if inverse else (j == src)
    return jnp.where(hit, 1.0, 0.0).astype(BF16)


def _proj_deint_kernel(x_ref, mod_ref, perm_ref, w_ref, o_ref, *, tm, seq, is_ctx):
    row = _mod_row_index(pl.program_id(0), tm, seq, is_ctx)
    shift, scale = _mod_row(mod_ref, row, 3), _mod_row(mod_ref, row, 4)
    half = DEINT_BLOCK // 2
    n_sub = SUB_TILES if tm % (SUB_TILES * DEINT_BLOCK) == 0 else 1
    sub = tm // n_sub
    for s in range(n_sub):
        parts = []
        for r in range(s * sub, (s + 1) * sub, DEINT_BLOCK):
            xm = (_ln(x_ref[r:r + DEINT_BLOCK, :]) * (1.0 + scale) + shift).astype(BF16)
            parts.append(_dot(perm_ref[...], xm).astype(BF16))
        p = _dot(parts[0] if len(parts) == 1 else jnp.concatenate(parts, axis=0), w_ref[...])
        for k in range(sub // DEINT_BLOCK):
            dst = (s * sub + k * DEINT_BLOCK) // 2
            o_ref[0, dst:dst + half, :] = p[k * DEINT_BLOCK:k * DEINT_BLOCK + half]
            o_ref[1, dst:dst + half, :] = p[k * DEINT_BLOCK + half:(k + 1) * DEINT_BLOCK]


def _proj_deint(x2, nb, seq, mod, layer, w, *, is_ctx):
    n = x2.shape[0]
    tm = _token_tile(n, seq, False)
    assert tm % DEINT_BLOCK == 0
    tiles = seq // tm
    n_out = w.shape[1]
    perm = _deint_perm()
    return pl.pallas_call(
        functools.partial(_proj_deint_kernel, tm=tm, seq=seq, is_ctx=is_ctx),
        grid=(n // tm,),
        in_specs=[pl.BlockSpec((tm, D_MODEL), lambda i: (i, 0)), _picked_spec(mod.shape, (layer,)),
                  _const_spec(perm.shape), _const_spec(w.shape)],
        out_specs=pl.BlockSpec((None, 2, tm // 2, n_out), lambda i: (i // tiles, 0, i % tiles, 0)),
        out_shape=jax.ShapeDtypeStruct((nb, 2, seq // 2, n_out), F32),
        compiler_params=_params("parallel"),
        name="mixer_proj_deint",
    )(x2, mod, perm, w)


def _fourier_kernel(a_ref, cg_ref, te_ref, to_ref, o_ref, stack_ref, *, half):
    @pl.when(pl.program_id(1) == 0)
    def _():
        for par in range(2):
            p = _dot(a_ref[par].astype(BF16), cg_ref[...])
            stack_ref[par, 0:half, :] = p[:, :D_MIX].astype(BF16)
            stack_ref[par, half:2 * half, :] = p[:, D_MIX:].astype(BF16)

    scale = 1.0 / math.sqrt(2 * half * (D_FOURIER // FOURIER_GROUPS))
    ev = _dot(te_ref[...], stack_ref[0])
    od = _dot(to_ref[...], stack_ref[1])
    o_ref[0] = ((ev + od) * scale).astype(o_ref.dtype)
    o_ref[1] = ((ev - od) * scale).astype(o_ref.dtype)


def _fourier(p4, cg, te, to, *, tl):
    nb, _, half, _ = p4.shape
    out = pl.pallas_call(
        functools.partial(_fourier_kernel, half=half),
        grid=(nb, half // tl),
        in_specs=[
            pl.BlockSpec((None, 2, half, D_MIX), lambda b, j: (b, 0, 0, 0)),
            _const_spec(cg.shape),
            pl.BlockSpec((tl, 2 * half), lambda b, j: (j, 0)),
            pl.BlockSpec((tl, 2 * half), lambda b, j: (j, 0)),
        ],
        out_specs=pl.BlockSpec((None, 2, tl, D_MIX), lambda b, j: (b, 0, j, 0)),
        out_shape=jax.ShapeDtypeStruct((nb, 2, half, D_MIX), BF16),
        scratch_shapes=[pltpu.VMEM((2, 2 * half, D_MIX), BF16)],
        compiler_params=_params("parallel", "arbitrary"),
        name="fourier_mix",
    )(p4, cg, te, to)
    return out.reshape(nb * 2 * half, D_MIX)


def _hyena_filter_kernel(z_ref, w1_ref, b1_ref, w2_ref, b2_ref, w3f_ref, w3b_ref, freq_ref, delta_ref,
                         fwd_ref, bwd_ref, hdn_ref):
    hi = lax.Precision.HIGHEST
    z = z_ref[...]

    @pl.when(pl.program_id(0) == 0)
    def _():
        freq = freq_ref[...]
        h1 = jnp.sin(freq * (jnp.dot(z, w1_ref[...], precision=hi, preferred_element_type=F32) + b1_ref[...]))
        hdn_ref[...] = jnp.sin(freq * (jnp.dot(h1, w2_ref[...], precision=hi, preferred_element_type=F32)
                                       + b2_ref[...]))

    hdn = hdn_ref[...]
    decay = jnp.exp(-z[:, 0:1] * delta_ref[...])
    fwd = jnp.dot(hdn, w3f_ref[...], precision=hi, preferred_element_type=F32) * decay
    bwd = jnp.dot(hdn, w3b_ref[...], precision=hi, preferred_element_type=F32) * decay
    row = lax.broadcasted_iota(jnp.int32, bwd.shape, 0)
    bwd = jnp.where(row == 0, 0.0, bwd)
    norm = jnp.sum(jnp.abs(fwd), axis=0, keepdims=True) + jnp.sum(jnp.abs(bwd), axis=0, keepdims=True) + 1e-6
    inv = 1.0 / norm
    fwd_ref[...] = fwd * inv
    bwd_ref[...] = bwd * inv


def _hyena_filters(seq, w1, b1, w2, b2, w3, freq):
    pos = jnp.concatenate([jnp.arange(0, seq, 2), jnp.arange(1, seq, 2)]).astype(F32)[:, None]
    t = pos / (seq - 1)
    bands = (HY_EMB - 1) // 2
    fr = jnp.linspace(1e-4, bands - 1, bands, dtype=F32)[None, :]
    w = 2.0 * math.pi * pos * fr / seq
    z = jnp.concatenate([t, jnp.cos(w), -jnp.sin(w), jnp.zeros((seq, LANE - HY_EMB), F32)], axis=-1)
    pad = LANE - HY_HID
    w1p = jnp.pad(w1, ((0, LANE - HY_EMB), (0, pad)))
    w2p = jnp.pad(w2, ((0, pad), (0, pad)))
    w3p = jnp.pad(w3, ((0, pad), (0, 0)))
    row = lambda v: jnp.pad(v, (0, pad)).reshape(1, LANE)
    delta = jnp.abs(jnp.linspace(math.log(HY_TARGET) / HY_SLOW_DECAY, math.log(HY_TARGET) / HY_FAST_DECAY,
                                 D_HYENA, dtype=F32)).reshape(1, D_HYENA)
    sq = lambda: _const_spec((LANE, LANE))
    vec = lambda: _const_spec((1, LANE))
    n_ord = 2
    return pl.pallas_call(
        _hyena_filter_kernel,
        grid=(n_ord,),
        in_specs=[_const_spec((seq, LANE)), sq(), vec(), sq(), vec(),
                  pl.BlockSpec((LANE, D_MIX), lambda o: (0, o)),
                  pl.BlockSpec((LANE, D_MIX), lambda o: (0, n_ord + o)),
                  vec(), _const_spec((1, D_MIX))],
        out_specs=[pl.BlockSpec((seq, D_MIX), lambda o: (0, o))] * 2,
        out_shape=[jax.ShapeDtypeStruct((seq, n_ord * D_MIX), F32)] * 2,
        scratch_shapes=[pltpu.VMEM((seq, LANE), F32)],
        compiler_params=_params("arbitrary"),
        name="hyena_filters",
    )(z, w1p, row(b1), w2p, row(b2), w3p, w3p, row(freq), delta)


def _alt_sum(x):
    row = lax.broadcasted_iota(jnp.int32, x.shape, 0)
    return jnp.sum(jnp.where((row & 1) == 1, -x, x), axis=0, keepdims=True)


def _alt_rows(v, shape):
    row = lax.broadcasted_iota(jnp.int32, shape, 0)
    return jnp.where((row & 1) == 1, -v, v)


def _half_spectrum(xe, xo, ce, se, co, so):
    ac, bc = _dot(ce, xe), _dot(co, xo)
    a_s, b_s = _dot(se, xe), _dot(so, xo)
    return ac + bc, a_s + b_s, ac - bc, b_s - a_s


def _hyena_spec_kernel(ce_ref, se_ref, co_ref, so_ref, fwd_ref, bwd_ref, alo_ref, blo_ref, ahi_ref, bhi_ref, mid_ref,
                       *, half):
    n = 4 * half
    tabs = (ce_ref[...], se_ref[...], co_ref[...], so_ref[...])
    fwd, bwd = fwd_ref[...], bwd_ref[...]
    fe, fo, be, bo = fwd[:half], fwd[half:], bwd[:half], bwd[half:]
    ce, se, co, so = tabs
    se_, so_ = (fe + be).astype(BF16), (fo + bo).astype(BF16)
    de_, do_ = (be - fe).astype(BF16), (bo - fo).astype(BF16)
    ac, bc = _dot(ce, se_), _dot(co, so_)
    a_s, b_s = _dot(se, de_), _dot(so, do_)
    row = lax.broadcasted_iota(jnp.int32, ac.shape, 0)
    wgt = jnp.where(row == 0, 1.0 / n, 2.0 / n)
    alo_ref[...] = wgt * (ac + bc)
    blo_ref[...] = wgt * (a_s + b_s)
    ahi_ref[...] = wgt * (ac - bc)
    bhi_ref[...] = wgt * (b_s - a_s)
    mid_ref[0:1, :] = (2.0 / n) * (_alt_sum(fe) + _alt_sum(be))
    mid_ref[1:2, :] = (2.0 / n) * (_alt_sum(bo) - _alt_sum(fo))


def _hyena_spectra(tabs, fwd, bwd, *, cb=256):
    seq, width = fwd.shape
    half = seq // 2
    blk = lambda rows: pl.BlockSpec((rows, cb), lambda j: (0, j))
    return pl.pallas_call(
        functools.partial(_hyena_spec_kernel, half=half),
        grid=(width // cb,),
        in_specs=[_const_spec((half, half))] * 4 + [blk(seq), blk(seq)],
        out_specs=[blk(half)] * 4 + [blk(2)],
        out_shape=[jax.ShapeDtypeStruct((half, width), F32)] * 4 + [jax.ShapeDtypeStruct((2, width), F32)],
        compiler_params=_params("parallel"),
        name="hyena_spectra",
    )(*tabs[:4], fwd, bwd)


def _short_conv(e, o, w, b):
    half = e.shape[0]
    row = lax.broadcasted_iota(jnp.int32, e.shape, 0)
    o_prev = jnp.where(row == 0, 0.0, pltpu.roll(o, 1, 0))
    e_next = jnp.where(row == half - 1, 0.0, pltpu.roll(e, half - 1, 0))
    w0, w1, w2 = w[0:1, :], w[1:2, :], w[2:3, :]
    return o_prev * w0 + e * w1 + o * w2 + b, e * w0 + o * w1 + e_next * w2 + b


HYENA_ROW_SPLIT = 2


def _hyena_order_kernel(z_ref, x_ref, ce_ref, se_ref, co_ref, so_ref, cot_ref, sot_ref, alo_ref, blo_ref, ahi_ref,
                        bhi_ref, mid_ref, skip_ref, wx_ref, bx_ref, *rest, order, first):
    if first:
        wz_ref, bz0_ref, o_ref = rest
        ze, zo = _short_conv(z_ref[0], z_ref[1], wz_ref[...], bz0_ref[...])
    else:
        (o_ref,) = rest
        ze, zo = z_ref[0], z_ref[1]
    half = ze.shape[0]
    groups = [slice(r, r + half // HYENA_ROW_SPLIT) for r in range(0, half, half // HYENA_ROW_SPLIT)]
    zeb, zob = ze.astype(BF16), zo.astype(BF16)
    prods = []
    for rs in groups:
        re_lo, im_lo, re_hi, im_hi = _half_spectrum(zeb, zob, ce_ref[rs, :], se_ref[rs, :], co_ref[rs, :],
                                                    so_ref[rs, :])
        alo, blo, ahi, bhi = alo_ref[rs, :], blo_ref[rs, :], ahi_ref[rs, :], bhi_ref[rs, :]
        wr_lo, wi_lo = re_lo * alo + im_lo * blo, re_lo * blo - im_lo * alo
        wr_hi, wi_hi = re_hi * ahi + im_hi * bhi, re_hi * bhi - im_hi * ahi
        prods.append(((wr_lo + wr_hi).astype(BF16), (wi_lo - wi_hi).astype(BF16),
                      (wr_lo - wr_hi).astype(BF16), (wi_lo + wi_hi).astype(BF16)))
    p_even_c, p_even_s, p_odd_c, p_odd_s = (jnp.concatenate(p, axis=0) for p in zip(*prods))
    me, mo = _short_conv(x_ref[0], x_ref[1], wx_ref[...], bx_ref[...])
    re_mid, im_mid = _alt_sum(ze), _alt_sum(zo)
    a_mid, b_mid = mid_ref[0:1, :], mid_ref[1:2, :]
    mid_e, mid_o = re_mid * a_mid + im_mid * b_mid, re_mid * b_mid - im_mid * a_mid
    skip = skip_ref[order:order + 1, :]
    for rs in groups:
        conv_e = _dot(ce_ref[rs, :], p_even_c) - _dot(se_ref[rs, :], p_even_s)
        conv_o = _dot(cot_ref[rs, :], p_odd_c) - _dot(sot_ref[rs, :], p_odd_s)
        conv_e = conv_e + _alt_rows(mid_e, conv_e.shape)
        conv_o = conv_o - _alt_rows(mid_o, conv_o.shape)
        o_ref[0, rs, :] = (me[rs] * (conv_e + ze[rs] * skip)).astype(o_ref.dtype)
        o_ref[1, rs, :] = (mo[rs] * (conv_o + zo[rs] * skip)).astype(o_ref.dtype)


def _hyena_order(z4, z_blk0, x4, x_blk0, tabs, coefs, skip, conv_w, conv_b, *, order, first, out_dtype, cb=256):
    nb, _, half, _ = x4.shape
    ncb = D_MIX // cb
    xw0 = (1 + order) * ncb
    conv_b = conv_b.reshape(1, -1)
    tok = lambda blk0: pl.BlockSpec((None, 2, half, cb), lambda j, b: (b, 0, 0, blk0 + j))
    colblk = lambda rows, blk0: pl.BlockSpec((rows, cb), lambda j, b: (0, blk0 + j), pipeline_mode=pl.Buffered(1))
    in_specs = [tok(z_blk0), tok(x_blk0)] + [_const_spec((half, half))] * 6
    in_specs += [colblk(half, order * ncb)] * 4 + [colblk(2, order * ncb), colblk(2, 0), colblk(3, xw0), colblk(1, xw0)]
    args = [z4, x4, *tabs, *coefs, skip, conv_w, conv_b]
    if first:
        in_specs += [colblk(3, 0), colblk(1, 0)]
        args += [conv_w, conv_b]
    return pl.pallas_call(
        functools.partial(_hyena_order_kernel, order=order, first=first),
        grid=(ncb, nb),
        in_specs=in_specs,
        out_specs=pl.BlockSpec((None, 2, half, cb), lambda j, b: (b, 0, 0, j)),
        out_shape=jax.ShapeDtypeStruct((nb, 2, half, D_MIX), out_dtype),
        compiler_params=_params("parallel", "parallel"),
        name="hyena_order",
    )(*args)


def _dit_tables(half, period):
    split = math.gcd(half, 64)

    def part(rows, mult, col_off):
        r = lax.broadcasted_iota(jnp.int32, (rows, half), 0) * mult
        c = 2 * lax.broadcasted_iota(jnp.int32, (rows, half), 1) + col_off
        ang = ((r * c) % period).astype(F32) * (2.0 * math.pi / period)
        return jnp.cos(ang), jnp.sin(ang)

    out = []
    for col_off in (0, 1):
        ca, sa = (t[:, None, :] for t in part(half // split, split, col_off))
        cb, sb = (t[None, :, :] for t in part(split, 1, col_off))
        out += [(ca * cb - sa * sb).reshape(half, half), (sa * cb + ca * sb).reshape(half, half)]
    return out


def _fourier_tables(seq):
    ce, se, co, so = _dit_tables(seq // 2, seq)
    te = jnp.concatenate([ce, -se], axis=1).astype(BF16)
    to = jnp.concatenate([co, -so], axis=1).astype(BF16)
    gsz = D_FOURIER // FOURIER_GROUPS
    r = lax.broadcasted_iota(jnp.int32, (gsz, gsz), 0)
    c = lax.broadcasted_iota(jnp.int32, (gsz, gsz), 1)
    ang = ((r * c) % gsz).astype(F32) * (2.0 * math.pi / gsz)
    eye = jnp.eye(FOURIER_GROUPS, dtype=F32)
    cg = jnp.concatenate([jnp.kron(eye, jnp.cos(ang)), jnp.kron(eye, jnp.sin(ang))], axis=1).astype(BF16)
    return te, to, cg


def _hyena_tables(seq):
    ce, se, co, so = (t.astype(BF16) for t in _dit_tables(seq // 2, 2 * seq))
    return ce, se, co, so, co.T, so.T


def _rms_heads(x, g):
    parts = []
    for h in range(D_MIX // LANE):
        xh = x[:, h * LANE:(h + 1) * LANE]
        parts.append(xh * lax.rsqrt(jnp.mean(xh * xh, axis=-1, keepdims=True) + LN_EPS) * g)
    return jnp.concatenate(parts, axis=-1)


def _outproj_kernel(x_ref, ya_ref, yb_ref, *rest, tm, seq, is_ctx, readout):
    if readout:
        r_ref, gd_ref, gg_ref, hg_ref, unperm_ref, mod_ref, w_ref, g_ref, b_ref, o_ref = rest
        r = r_ref[...]
        ya = (_rms_heads(ya_ref[...], gg_ref[...]) * (r * _sigmoid(r))).astype(BF16)
        yb = _rms_heads(yb_ref[...].reshape(tm, D_MIX) * _sigmoid(gd_ref[...].reshape(tm, D_MIX)), hg_ref[...])
        yb = _dot(unperm_ref[...], yb.astype(BF16)).astype(BF16)
    else:
        unperm_ref, mod_ref, w_ref, g_ref, b_ref, o_ref = rest
        ya = ya_ref[...]
        half = DEINT_BLOCK // 2
        parts = []
        for r in range(0, tm // 2, half):
            blk = jnp.concatenate([yb_ref[0, r:r + half, :], yb_ref[1, r:r + half, :]], axis=0)
            parts.append(_dot(unperm_ref[...], blk).astype(BF16))
        yb = parts[0] if len(parts) == 1 else jnp.concatenate(parts, axis=0)
    row = _mod_row_index(pl.program_id(0), tm, seq, is_ctx)
    gate = _mod_row(mod_ref, row, 5)
    y = _dot(ya, w_ref[:D_MIX, :]) + _dot(yb, w_ref[D_MIX:, :])
    z = ALPHA * x_ref[...] + gate * y
    o_ref[...] = _ln(z) * g_ref[...] + b_ref[...]


def _outproj(x2, ya, yb, mod, layer, w, ln_g, ln_b, *, seq, is_ctx, readout=None):
    n = x2.shape[0]
    tm = _token_tile(n, seq, False, want=TOKEN_TILE if readout is None else TOKEN_TILE // 2)
    tiles = seq // tm
    tok = lambda: pl.BlockSpec((tm, D_MIX), lambda i: (i, 0))
    in_specs = [pl.BlockSpec((tm, D_MODEL), lambda i: (i, 0)), tok()]
    args = [x2, ya, yb]
    if readout is None:
        assert tm % DEINT_BLOCK == 0
        unperm = _deint_perm(inverse=True)
        in_specs += [pl.BlockSpec((None, 2, tm // 2, D_MIX), lambda i: (i // tiles, 0, i % tiles, 0)),
                     _const_spec(unperm.shape)]
        args += [unperm]
    else:
        p_main, p_hg, gla_g, hg_g = readout
        rows = tm // GRID_W
        assert tm % GRID_W == 0 and rows % 8 == 0
        i = lax.broadcasted_iota(jnp.int32, (tm, tm), 0)
        j = lax.broadcasted_iota(jnp.int32, (tm, tm), 1)
        unperm = jnp.where(j == (i % GRID_W) * rows + i // GRID_W, 1.0, 0.0).astype(BF16)
        colmajor = lambda blk: pl.BlockSpec((None, GRID_W, rows, D_MIX), lambda i: (i // tiles, 0, i % tiles, blk))
        in_specs += [colmajor(0), pl.BlockSpec((tm, D_MIX), lambda i: (i, PM_R)), colmajor(PH_GD),
                     _const_spec((1, LANE)), _const_spec((1, LANE)), _const_spec((tm, tm))]
        args += [p_main, p_hg, gla_g.reshape(1, LANE), hg_g.reshape(1, LANE), unperm]
    in_specs += [_picked_spec(mod.shape, (layer,)), _const_spec(w.shape),
                 _picked_spec(ln_g.shape, (layer, 1)), _picked_spec(ln_b.shape, (layer, 1))]
    args += [mod, w, ln_g, ln_b]
    kern = functools.partial(_outproj_kernel, tm=tm, seq=seq, is_ctx=is_ctx, readout=readout is not None)
    return pl.pallas_call(
        kern,
        grid=(n // tm,),
        in_specs=in_specs,
        out_specs=pl.BlockSpec((tm, D_MODEL), lambda i: (i, 0)),
        out_shape=jax.ShapeDtypeStruct((n, D_MODEL), F32),
        compiler_params=_params("parallel"),
        name="mixer_out",
    )(*args)


def _split3(x):
    hi = x.astype(BF16)
    r1 = x - hi.astype(F32)
    mid = r1.astype(BF16)
    lo = (r1 - mid.astype(F32)).astype(BF16)
    return hi, mid, lo


SCAN_UNROLL = 8
CUMSUM_CHUNKS = 4


def _scan_unroll(n_chunks):
    return math.gcd(SCAN_UNROLL, n_chunks)


def _scan_consts(rev, unroll, heads, dk):
    n = math.gcd(unroll, CUMSUM_CHUNKS) * CHUNK
    i = lax.broadcasted_iota(jnp.int32, (n, n), 0)
    j = lax.broadcasted_iota(jnp.int32, (n, n), 1)
    same_chunk = (i // CHUNK) == (j // CHUNK)
    tri = jnp.where(same_chunk & ((j >= i) if rev else (j <= i)), 1.0, 0.0).astype(BF16)
    r = lax.broadcasted_iota(jnp.int32, (CHUNK, heads * CHUNK), 0)
    c = lax.broadcasted_iota(jnp.int32, (CHUNK, heads * CHUNK), 1) & (CHUNK - 1)
    causal = (c >= r) if rev else (c <= r)
    klane_head = lax.broadcasted_iota(jnp.int32, (1, heads * dk), 1) // dk
    return tri, causal, klane_head


def _block_diag(blocks):
    z = jnp.zeros_like(blocks[0])
    return jnp.concatenate([jnp.concatenate([b if c == h else z for c in range(len(blocks))], axis=1)
                            for h, b in enumerate(blocks)], axis=0)


def _scan_local(q, k, g, vb, consts, *, rev, heads, dk, need_out):
    tri, causal, klane_head = consts
    kt = heads * dk
    parts = _split3(g)
    span = tri.shape[0]
    gsum = jnp.concatenate([sum(_dot(tri, part[r:r + span]) for part in parts) for r in range(0, g.shape[0], span)],
                           axis=0)
    staged = []
    for u in range(g.shape[0] // CHUNK):
        rows = slice(u * CHUNK, (u + 1) * CHUNK)
        gs = gsum[rows]
        gt = gs[0:1] if rev else gs[CHUNK - 1:CHUNK]
        ku, vu = k[rows], vb[rows]
        q_in = o_intra = None
        if need_out:
            q_in = (q[rows] * jnp.exp(gs)).astype(BF16)
            k_dec = ku * jnp.exp(-gs)
            k_in = k_dec.astype(BF16)
            k_out = (k_dec * jnp.exp(gt)).astype(BF16)
            if dk % LANE == 0:
                k4 = _block_diag([k_in[:, h * dk:(h + 1) * dk] for h in range(heads)])
            else:
                k4 = jnp.concatenate([jnp.where(klane_head == h, k_in, jnp.zeros_like(k_in)) for h in range(heads)],
                                     axis=0)
            att = jnp.where(causal, _dot_nt(q_in, k4), 0.0).astype(BF16)
            o_intra = _dot(att, _block_diag([vu[:, h * LANE:(h + 1) * LANE] for h in range(heads)]))
        else:
            k_out = (ku * jnp.exp(gt - gs)).astype(BF16)
        if dk % LANE == 0:
            inc = [_dot_tn(k_out[:, h * dk:(h + 1) * dk], vu[:, h * LANE:(h + 1) * LANE]) for h in range(heads)]
        else:
            k_t = jnp.transpose(jnp.concatenate([k_out.astype(F32), jnp.zeros((LANE - CHUNK, kt), F32)], axis=0))
            k_t = k_t.astype(BF16)
            v_pad = jnp.concatenate([vu, jnp.zeros((LANE - CHUNK, heads * LANE), BF16)], axis=0)
            inc = [_dot(k_t[h * dk:(h + 1) * dk, :], v_pad[:, h * LANE:(h + 1) * LANE]) for h in range(heads)]
        dec = jnp.transpose(jnp.broadcast_to(jnp.exp(gt), (LANE, kt)))
        staged.append((q_in, o_intra, inc, [dec[h * dk:(h + 1) * dk] for h in range(heads)]))
    return staged


def _scan_states(s_ref, base, staged):
    heads = len(staged[0][2])
    state = [s_ref[base + h] for h in range(heads)]
    outs = []
    for q_in, o_intra, inc, decay in staged:
        if q_in is not None:
            outs.append(o_intra + _dot(q_in, _block_diag([s.astype(BF16) for s in state])))
        state = [s * d + i for s, d, i in zip(state, decay, inc)]
    for h in range(heads):
        s_ref[base + h] = state[h]
    return outs


def _gla_kernel(qk_ref, v_ref, g_ref, cqk_ref, cv_ref, cg_ref, o_ref, s_ref, ob_ref, *, seq, ctx_len):
    heads, dk = GLA_HEADS, GLA_DK
    kt = heads * dk
    s_ref[...] = jnp.zeros_like(s_ref)

    def make_run(qk, vv, gg, n_chunks, need_out):
        unroll = _scan_unroll(n_chunks)
        rows = unroll * CHUNK
        consts = [_scan_consts(rev, unroll, heads, dk) for rev in (False, True)]

        def run(n):
            work = []
            for d, rev in enumerate((False, True)):
                start = (n_chunks * CHUNK - rows - n * rows) if rev else n * rows
                sl = pl.ds(pl.multiple_of(start, rows), rows)
                q = qk[sl, 0:kt] * (dk ** -0.5) if need_out else None
                staged = _scan_local(q, qk[sl, kt:2 * kt], gg[sl, d * kt:(d + 1) * kt], vv[sl, :].astype(BF16),
                                     consts[d], rev=rev, heads=heads, dk=dk, need_out=need_out)
                work.append((d, rev, sl, staged[::-1] if rev else staged))
            for d, rev, sl, staged in work:
                outs = _scan_states(s_ref, d * heads, staged)
                if need_out:
                    (ob_ref if rev else o_ref)[sl, :] = jnp.concatenate(outs[::-1] if rev else outs, axis=0)

        return run, n_chunks // unroll

    run, steps = make_run(cqk_ref, cv_ref, cg_ref, ctx_len // CHUNK, False)
    for n in range(steps):
        run(n)
    run, steps = make_run(qk_ref, v_ref, g_ref, seq // CHUNK, True)

    def body(n, carry):
        run(n)
        return carry

    lax.fori_loop(0, steps, body, 0)
    o_ref[...] = o_ref[...] + ob_ref[...]


def _gla(p3, pc3):
    nb, seq, _ = p3.shape
    ctx_len = pc3.shape[1]
    blk = lambda length, j: pl.BlockSpec((None, length, D_MIX), lambda b: (b, 0, j))
    return pl.pallas_call(
        functools.partial(_gla_kernel, seq=seq, ctx_len=ctx_len),
        grid=(nb,),
        in_specs=[blk(seq, PM_QK), blk(seq, PM_V), blk(seq, PM_GATE),
                  blk(ctx_len, P_QK), blk(ctx_len, P_V), blk(ctx_len, P_GATE)],
        out_specs=pl.BlockSpec((None, seq, D_MIX), lambda b: (b, 0, 0)),
        out_shape=jax.ShapeDtypeStruct((nb, seq, D_MIX), F32),
        scratch_shapes=[pltpu.VMEM((2 * GLA_HEADS, GLA_DK, GLA_DV), F32),
                        pltpu.VMEM((seq, D_MIX), F32)],
        compiler_params=_params("parallel"),
        name="gla_scan",
    )(p3, p3, p3, pc3, pc3, pc3)


def _hgrn_kernel(q_ref, ff_ref, fb_ref, v_ref, cff_ref, cfb_ref, cv_ref, o_ref, s_ref, *, seq, ctx_len):
    heads, dk = HG_HEADS, HG_D
    s_ref[...] = jnp.zeros_like(s_ref)

    def make_run(qq, gates, vv, n_chunks, need_out):
        unroll = _scan_unroll(n_chunks)
        rows = unroll * CHUNK
        consts = [_scan_consts(rev, unroll, heads, dk) for rev in (False, True)]

        def run(n):
            work = []
            for d, rev in enumerate((False, True)):
                start = (n_chunks * CHUNK - rows - n * rows) if rev else n * rows
                sl = pl.ds(pl.multiple_of(start, rows), rows)
                f = gates[d][sl, :]
                staged = _scan_local(qq[sl, :] if need_out else None, 1.0 - f, jnp.log(f), vv[sl, :].astype(BF16), consts[d],
                                     rev=rev, heads=heads, dk=dk, need_out=need_out)
                work.append((d, rev, sl, staged[::-1] if rev else staged))
            for d, rev, sl, staged in work:
                outs = _scan_states(s_ref, d * heads, staged)
                if need_out:
                    o_ref[sl, :] += jnp.concatenate(outs[::-1] if rev else outs, axis=0)

        return run, n_chunks // unroll

    run, steps = make_run(None, (cff_ref, cfb_ref), cv_ref, ctx_len // CHUNK, False)
    for n in range(steps):
        run(n)
    o_ref[...] = jnp.zeros_like(o_ref)
    run, steps = make_run(q_ref, (ff_ref, fb_ref), v_ref, seq // CHUNK, True)

    def body(n, carry):
        run(n)
        return carry

    lax.fori_loop(0, steps, body, 0)


def _hgrn(ph3, pc3):
    nb, seq, _ = ph3.shape
    ctx_len = pc3.shape[1]
    blk = lambda length, j: pl.BlockSpec((None, length, D_MIX), lambda b: (b, 0, j))
    return pl.pallas_call(
        functools.partial(_hgrn_kernel, seq=seq, ctx_len=ctx_len),
        grid=(nb,),
        in_specs=[blk(seq, PH_QD), blk(seq, PH_FFL), blk(seq, PH_FBL), blk(seq, PH_IDD),
                  blk(ctx_len, P_FFL), blk(ctx_len, P_FBL), blk(ctx_len, P_IDD)],
        out_specs=pl.BlockSpec((None, seq, D_MIX), lambda b: (b, 0, 0)),
        out_shape=jax.ShapeDtypeStruct((nb, seq, D_MIX), F32),
        scratch_shapes=[pltpu.VMEM((2 * HG_HEADS, HG_D, HG_D), F32)],
        compiler_params=_params("parallel"),
        name="hgrn_scan",
    )(ph3, ph3, ph3, ph3, pc3, pc3, pc3)


def _even_mixer(x2, nb, seq, mod, layer, is_ctx, w_in, w_out, conv_w, conv_b, filt, skip, ln_g, ln_b):
    p4 = _proj_deint(x2, nb, seq, mod, layer, w_in, is_ctx=is_ctx)
    te, to, cg = _fourier_tables(seq)
    ya = _fourier(p4, cg, te, to, tl=min(seq // 2, 512))
    tabs = _hyena_tables(seq)
    fwd, bwd = _hyena_filters(seq, *filt)
    coefs = _hyena_spectra(tabs, fwd, bwd)
    cb = 256
    u0 = D_FOURIER // cb
    per = D_HYENA // cb
    y1 = _hyena_order(p4, u0, p4, u0 + per, tabs, coefs, skip, conv_w, conv_b, order=0, first=True, out_dtype=F32,
                      cb=cb)
    y2 = _hyena_order(y1, 0, p4, u0 + 2 * per, tabs, coefs, skip, conv_w, conv_b, order=1, first=False,
                      out_dtype=BF16, cb=cb)
    return _outproj(x2, ya, y2, mod, layer, w_out, ln_g, ln_b, seq=seq, is_ctx=is_ctx)


def _odd_proj_weights(w_in, a_up, a_b):
    kq = GLA_HEADS * GLA_DK
    low0 = 2 * kq + 2 * D_MIX
    low1 = low0 + 2 * GLA_RANK
    w = jnp.concatenate([w_in[:, :low0], w_in[:, low1:], w_in[:, low0:low1],
                         jnp.zeros((D_MODEL, LANE - 2 * GLA_RANK), w_in.dtype)], axis=1).astype(BF16)
    up = jnp.zeros((LANE, 2 * kq), F32)
    up = up.at[0:GLA_RANK, 0:kq].set(a_up[0]).at[GLA_RANK:2 * GLA_RANK, kq:].set(a_up[1]).astype(BF16)
    return w, up, a_b.reshape(1, 2 * kq)


def kernel(x, c, ctx, c_ctx, mod_w, mod_b, ffn_w_in, ffn_w_out, ln_g, ln_b, ev_w_in, ev_w_out, hy_conv_w, hy_conv_b, hy_w1, hy_b1, hy_w2, hy_b2, hy_w3, hy_freq, hy_skip, od_w_in, od_w_out, gla_a_up, gla_a_b, gla_norm_g, hg_lb, hg_norm_g):
    nb, seq, d = x.shape
    ctx_len = ctx.shape[1]
    assert d == D_MODEL and seq % 512 == 0 and ctx_len % CHUNK == 0 and nb + 1 <= MOD_ROWS - 7
    assert mod_w.shape[0] == DEPTH == 2

    c_all = jnp.zeros((MOD_ROWS, D_MODEL), F32).at[:nb].set(c).at[MOD_ROWS - 8].set(c_ctx)
    mod = _mod_table(c_all, mod_w, mod_b)
    w_in_b = ffn_w_in.astype(BF16)
    w_out_b = ffn_w_out.astype(BF16)

    xs = x.reshape(nb * seq, D_MODEL)
    xc = ctx.reshape(nb * ctx_len, D_MODEL)
    streams = [(xs, seq, False), (xc, ctx_len, True)]

    ln_g4 = ln_g.reshape(DEPTH, 3, 1, D_MODEL)
    ln_b4 = ln_b.reshape(DEPTH, 3, 1, D_MODEL)

    def ffn(stream, layer, half):
        arr, length, is_ctx = stream
        return (_ffn(arr, mod, layer, half, w_in_b, w_out_b, ln_g4, ln_b4, seq=length, is_ctx=is_ctx), length, is_ctx)

    streams = [ffn(s, 0, 0) for s in streams]
    filt = (hy_w1[0], hy_b1[0], hy_w2[0], hy_b2[0], hy_w3[0], hy_freq[0])
    ev_in_b, ev_out_b = ev_w_in[0].astype(BF16), ev_w_out[0].astype(BF16)
    streams = [(_even_mixer(arr, nb, length, mod, 0, is_ctx, ev_in_b, ev_out_b, hy_conv_w[0], hy_conv_b[0], filt,
                            hy_skip[0], ln_g4, ln_b4), length, is_ctx) for arr, length, is_ctx in streams]
    streams = [ffn(s, 0, 1) for s in streams]

    streams = [ffn(s, 1, 0) for s in streams]
    (xs, _, _), (xc, _, _) = streams
    w_proj, up, up_b = _odd_proj_weights(od_w_in[0], gla_a_up[0], gla_a_b[0])
    n_gla, n_hg = (P_R + 1) * D_MIX, (PH_GD + 1) * D_MIX
    w_main = jnp.concatenate([w_proj[:, :n_gla], w_proj[:, n_gla + n_hg:]], axis=1)
    pc = _proj(xc, mod, 1, w_proj, seq=ctx_len, is_ctx=True, gates=(up, up_b), forget=(hg_lb, P_FFL * D_MIX),
               want=TOKEN_TILE // 2)
    pm = _proj(xs, mod, 1, w_main, seq=seq, is_ctx=False, gates=(up, up_b))
    ph = _proj_colmajor(xs, nb, seq, mod, 1, w_proj[:, n_gla:n_gla + n_hg], hg_lb)
    pc3 = pc.reshape(nb, ctx_len, -1)
    o_gla = _gla(pm.reshape(nb, seq, -1), pc3).reshape(nb * seq, D_MIX)
    o_hg = _hgrn(ph.reshape(nb, seq, -1), pc3).reshape(ph.shape[:3] + (D_MIX,))
    xs = _outproj(xs, o_gla, o_hg, mod, 1, od_w_out[0].astype(BF16), ln_g4, ln_b4, seq=seq, is_ctx=False,
                  readout=(pm, ph, gla_norm_g[0], hg_norm_g[0]))
    xs, _, _ = ffn((xs, seq, False), 1, 1)
    return xs.reshape(nb, seq, D_MODEL)
```

```python
import functools
import math

import jax
import jax.numpy as jnp
from jax import lax
from jax.experimental import pallas as pl
from jax.experimental.pallas import tpu as pltpu

F32 = jnp.float32
BF16 = jnp.bfloat16

D_MODEL = 1024
DEPTH = 2
N_MOD = 9
D_FF = 2816
LN_EPS = 1e-6
ALPHA = (2.0 * DEPTH) ** 0.25
GRID_W = 64
D_FOURIER = 512
FOURIER_GROUPS = 8
D_HYENA = 512
HY_EMB = 33
HY_HID = 64
HY_FAST_DECAY = 0.3
HY_SLOW_DECAY = 1.5
HY_TARGET = 1e-2
CHUNK = 64
GLA_HEADS = 4
GLA_DK = 64
GLA_DV = 128
GLA_RANK = 16
GLA_TAU = 16.0
HG_HEADS = 4
HG_D = 128
D_MIX = 512

LANE = 128
MOD_ROWS = 24
VMEM_LIMIT = 56 * 1024 * 1024

P_QK, P_V, P_R, P_QD, P_FFL, P_FBL, P_IDD, P_GD, P_GATE = range(9)
PM_QK, PM_V, PM_R, PM_GATE = range(4)
PH_QD, PH_FFL, PH_FBL, PH_IDD, PH_GD = range(5)


def _params(*sem):
    return pltpu.CompilerParams(dimension_semantics=sem, vmem_limit_bytes=VMEM_LIMIT)


def _dot(a, b):
    return jnp.dot(a, b, preferred_element_type=F32)


def _dot_nt(a, b):
    return lax.dot_general(a, b, (((1,), (1,)), ((), ())), preferred_element_type=F32)


def _dot_tn(a, b):
    return lax.dot_general(a, b, (((0,), (0,)), ((), ())), preferred_element_type=F32)


def _ln(x):
    mu = jnp.mean(x, axis=-1, keepdims=True)
    xc = x - mu
    var = jnp.mean(xc * xc, axis=-1, keepdims=True)
    return xc * lax.rsqrt(var + LN_EPS)


def _sigmoid(x):
    return 1.0 / (1.0 + jnp.exp(-x))


def _mod_row(mod_ref, row, k):
    return mod_ref[pl.ds(row, 1), k * D_MODEL:(k + 1) * D_MODEL]


def _mod_row_index(tile, tm, seq, is_ctx):
    if is_ctx:
        return MOD_ROWS - 8
    return (tile * tm) // seq


def _const_spec(shape):
    return pl.BlockSpec(shape, lambda *_: (0,) * len(shape), pipeline_mode=pl.Buffered(1))


def _picked_spec(shape, lead):
    tail = tuple(shape[len(lead):])
    return pl.BlockSpec((None,) * len(lead) + tail, lambda *_: tuple(lead) + (0,) * len(tail),
                        pipeline_mode=pl.Buffered(1))


TOKEN_TILE = 1024
SUB_TILES = 2


def _token_tile(n, seq, is_ctx, want=TOKEN_TILE):
    tm = min(want, n if is_ctx else seq)
    while n % tm or (not is_ctx and seq % tm):
        tm //= 2
    return tm


def _mod_kernel(c_ref, w_ref, b_ref, o_ref):
    c = c_ref[...]
    s = (c * _sigmoid(c)).astype(BF16)
    o_ref[...] = _dot(s, w_ref[...].astype(BF16)) + b_ref[...]


def _mod_table(c_all, mod_w, mod_b):
    return pl.pallas_call(
        _mod_kernel,
        grid=(DEPTH, N_MOD),
        in_specs=[
            _const_spec((MOD_ROWS, D_MODEL)),
            pl.BlockSpec((None, D_MODEL, D_MODEL), lambda l, j: (l, 0, j)),
            pl.BlockSpec((None, 1, D_MODEL), lambda l, j: (l, 0, j)),
        ],
        out_specs=pl.BlockSpec((None, MOD_ROWS, D_MODEL), lambda l, j: (l, 0, j)),
        out_shape=jax.ShapeDtypeStruct((DEPTH, MOD_ROWS, N_MOD * D_MODEL), F32),
        compiler_params=_params("parallel", "parallel"),
        name="mod_table",
    )(c_all, mod_w, mod_b.reshape(DEPTH, 1, N_MOD * D_MODEL))


FFN_CHUNK = 256


def _swiglu(xm, win_ref, wout_ref):
    acc = None
    for j in range(D_FF // FFN_CHUNK):
        lo = j * FFN_CHUNK
        a = _dot(xm, win_ref[:, lo:lo + FFN_CHUNK])
        u = _dot(xm, win_ref[:, D_FF + lo:D_FF + lo + FFN_CHUNK])
        h = (a * _sigmoid(a) * u).astype(BF16)
        y = _dot(h, wout_ref[lo:lo + FFN_CHUNK, :])
        acc = y if acc is None else acc + y
    return acc


def _ffn_kernel(x_ref, mod_ref, win_ref, wout_ref, g_ref, b_ref, o_ref, *, k0, tm, seq, is_ctx):
    row = _mod_row_index(pl.program_id(0), tm, seq, is_ctx)
    shift, scale, gate = (_mod_row(mod_ref, row, k0 + t) for t in range(3))
    subs = [slice(r, r + tm // SUB_TILES) for r in range(0, tm, tm // SUB_TILES)]
    xms = [(_ln(x_ref[s, :]) * (1.0 + scale) + shift).astype(BF16) for s in subs]
    accs = [_swiglu(xm, win_ref, wout_ref) for xm in xms]
    for s, acc in zip(subs, accs):
        z = ALPHA * x_ref[s, :] + (0.5 * gate) * acc
        o_ref[s, :] = _ln(z) * g_ref[...] + b_ref[...]


def _ffn(x2, mod, layer, half, w_in, w_out, ln_g, ln_b, *, seq, is_ctx):
    n = x2.shape[0]
    tm = _token_tile(n, seq, is_ctx)
    kern = functools.partial(_ffn_kernel, k0=6 * half, tm=tm, seq=seq, is_ctx=is_ctx)
    return pl.pallas_call(
        kern,
        grid=(n // tm,),
        in_specs=[
            pl.BlockSpec((tm, D_MODEL), lambda i: (i, 0)),
            _picked_spec(mod.shape, (layer,)),
            _picked_spec(w_in.shape, (layer, half)),
            _picked_spec(w_out.shape, (layer, half)),
            _picked_spec(ln_g.shape, (layer, 2 * half)),
            _picked_spec(ln_b.shape, (layer, 2 * half)),
        ],
        out_specs=pl.BlockSpec((tm, D_MODEL), lambda i: (i, 0)),
        out_shape=jax.ShapeDtypeStruct((n, D_MODEL), F32),
        compiler_params=_params("parallel"),
        name="ffn",
    )(x2, mod, w_in, w_out, ln_g, ln_b)


def _log_sigmoid(x):
    return jnp.minimum(x, 0.0) - jnp.log(1.0 + jnp.exp(-jnp.abs(x)))


def _forget_gates(lb_ref, layer, logits):
    raw = lb_ref[...]
    e = jnp.exp(raw - jnp.max(raw, axis=0, keepdims=True))
    sm = e / jnp.sum(e, axis=0, keepdims=True)
    lb = jnp.sum(sm[0:layer + 1, :], axis=0, keepdims=True) - sm[0:1, :]
    lb = jnp.concatenate([lb, lb], axis=-1)
    return lb + (1.0 - lb) * _sigmoid(logits)


def _proj_kernel(x_ref, mod_ref, w_ref, *rest, tm, seq, is_ctx, gla_gates, forget):
    row = _mod_row_index(pl.program_id(0), tm, seq, is_ctx)
    shift, scale = _mod_row(mod_ref, row, 3), _mod_row(mod_ref, row, 4)
    subs = [slice(r, r + tm // SUB_TILES) for r in range(0, tm, tm // SUB_TILES)]
    xms = [(_ln(x_ref[s, :]) * (1.0 + scale) + shift).astype(BF16) for s in subs]
    ps = [_dot(xm, w_ref[...]) for xm in xms]
    if not gla_gates:
        (o_ref,) = rest
        for s, p in zip(subs, ps):
            o_ref[s, :] = p
        return
    aup_ref, ab_ref, *lb_ref, o_ref = rest
    for s, p in zip(subs, ps):
        main = p.shape[1] - LANE
        o_ref[s, :main] = p[:, :main]
        if forget is not None:
            layer, c0 = forget
            o_ref[s, c0:c0 + 2 * D_MIX] = _forget_gates(lb_ref[0], layer, p[:, c0:c0 + 2 * D_MIX])
        low = p[:, main:].astype(BF16)
        o_ref[s, main:] = _log_sigmoid(_dot(low, aup_ref[...]) + ab_ref[...]) * (1.0 / GLA_TAU)


PERM_COLS = 8


def _proj_colmajor_kernel(x_ref, mod_ref, perm_ref, w_ref, lb_ref, o_ref, *, layer):
    rows, cols, _ = x_ref.shape
    row = pl.program_id(0)
    shift, scale = _mod_row(mod_ref, row, 3), _mod_row(mod_ref, row, 4)
    per_sub = cols // SUB_TILES
    for s in range(SUB_TILES):
        parts = []
        for c in range(s * per_sub, (s + 1) * per_sub, PERM_COLS):
            x = x_ref[:, c:c + PERM_COLS, :].reshape(rows * PERM_COLS, D_MODEL)
            xm = (_ln(x) * (1.0 + scale) + shift).astype(BF16)
            parts.append(_dot(perm_ref[...], xm).astype(BF16))
        p = _dot(jnp.concatenate(parts, axis=0), w_ref[...])
        c0, c1 = PH_FFL * D_MIX, (PH_FBL + 1) * D_MIX
        p = jnp.concatenate([p[:, :c0], _forget_gates(lb_ref, layer, p[:, c0:c1]), p[:, c1:]], axis=-1)
        o_ref[s * per_sub:(s + 1) * per_sub] = p.reshape(per_sub, rows, w_ref.shape[1])


def _proj_colmajor(x2, nb, seq, mod, layer, w, hg_lb, *, cols=32):
    rows = seq // GRID_W
    n_out = w.shape[1]
    n = rows * PERM_COLS
    i = lax.broadcasted_iota(jnp.int32, (n, n), 0)
    j = lax.broadcasted_iota(jnp.int32, (n, n), 1)
    perm = jnp.where(j == (i % rows) * PERM_COLS + i // rows, 1.0, 0.0).astype(BF16)
    return pl.pallas_call(
        functools.partial(_proj_colmajor_kernel, layer=layer),
        grid=(nb, GRID_W // cols),
        in_specs=[pl.BlockSpec((None, rows, cols, D_MODEL), lambda b, c: (b, 0, c, 0)),
                  _picked_spec(mod.shape, (layer,)), _const_spec(perm.shape), _const_spec(w.shape),
                  _const_spec(hg_lb.shape)],
        out_specs=pl.BlockSpec((None, cols, rows, n_out), lambda b, c: (b, c, 0, 0)),
        out_shape=jax.ShapeDtypeStruct((nb, GRID_W, rows, n_out), F32),
        compiler_params=_params("parallel", "parallel"),
        name="mixer_proj_colmajor",
    )(x2.reshape(nb, rows, GRID_W, D_MODEL), mod, perm, w, hg_lb)


def _proj(x2, mod, layer, w, *, seq, is_ctx, gates=None, forget=None, want=TOKEN_TILE):
    n = x2.shape[0]
    tm = _token_tile(n, seq, is_ctx, want)
    n_out = w.shape[1] - LANE + gates[0].shape[1] if gates is not None else w.shape[1]
    kern = functools.partial(_proj_kernel, tm=tm, seq=seq, is_ctx=is_ctx, gla_gates=gates is not None,
                             forget=None if forget is None else (layer, forget[1]))
    extra = [] if gates is None else list(gates)
    if forget is not None:
        extra.append(forget[0])
    return pl.pallas_call(
        kern,
        grid=(n // tm,),
        in_specs=[pl.BlockSpec((tm, D_MODEL), lambda i: (i, 0)), _picked_spec(mod.shape, (layer,)),
                  _const_spec(w.shape)] + [_const_spec(e.shape) for e in extra],
        out_specs=pl.BlockSpec((tm, n_out), lambda i: (i, 0)),
        out_shape=jax.ShapeDtypeStruct((n, n_out), F32),
        compiler_params=_params("parallel"),
        name="mixer_proj",
    )(x2, mod, w, *extra)


DEINT_BLOCK = 256


def _deint_perm(inverse=False):
    i = lax.broadcasted_iota(jnp.int32, (DEINT_BLOCK, DEINT_BLOCK), 0)
    j = lax.broadcasted_iota(jnp.int32, (DEINT_BLOCK, DEINT_BLOCK), 1)
    half = DEINT_BLOCK // 2
    src = 2 * (i % half) + i // half
    hit = (i == 2 * (j % half) + j // half) if inverse else (j == src)
    return jnp.where(hit, 1.0, 0.0).astype(BF16)


def _proj_deint_kernel(x_ref, mod_ref, perm_ref, w_ref, o_ref, *, tm, seq, is_ctx):
    row = _mod_row_index(pl.program_id(0), tm, seq, is_ctx)
    shift, scale = _mod_row(mod_ref, row, 3), _mod_row(mod_ref, row, 4)
    half = DEINT_BLOCK // 2
    n_sub = SUB_TILES if tm % (SUB_TILES * DEINT_BLOCK) == 0 else 1
    sub = tm // n_sub
    for s in range(n_sub):
        parts = []
        for r in range(s * sub, (s + 1) * sub, DEINT_BLOCK):
            xm = (_ln(x_ref[r:r + DEINT_BLOCK, :]) * (1.0 + scale) + shift).astype(BF16)
            parts.append(_dot(perm_ref[...], xm).astype(BF16))
        p = _dot(parts[0] if len(parts) == 1 else jnp.concatenate(parts, axis=0), w_ref[...])
        for k in range(sub // DEINT_BLOCK):
            dst = (s * sub + k * DEINT_BLOCK) // 2
            o_ref[0, dst:dst + half, :] = p[k * DEINT_BLOCK:k * DEINT_BLOCK + half]
            o_ref[1, dst:dst + half, :] = p[k * DEINT_BLOCK + half:(k + 1) * DEINT_BLOCK]


def _proj_deint(x2, nb, seq, mod, layer, w, *, is_ctx):
    n = x2.shape[0]
    tm = _token_tile(n, seq, False)
    assert tm % DEINT_BLOCK == 0
    tiles = seq // tm
    n_out = w.shape[1]
    perm = _deint_perm()
    return pl.pallas_call(
        functools.partial(_proj_deint_kernel, tm=tm, seq=seq, is_ctx=is_ctx),
        grid=(n // tm,),
        in_specs=[pl.BlockSpec((tm, D_MODEL), lambda i: (i, 0)), _picked_spec(mod.shape, (layer,)),
                  _const_spec(perm.shape), _const_spec(w.shape)],
        out_specs=pl.BlockSpec((None, 2, tm // 2, n_out), lambda i: (i // tiles, 0, i % tiles, 0)),
        out_shape=jax.ShapeDtypeStruct((nb, 2, seq // 2, n_out), F32),
        compiler_params=_params("parallel"),
        name="mixer_proj_deint",
    )(x2, mod, perm, w)


def _fourier_kernel(a_ref, cg_ref, te_ref, to_ref, o_ref, stack_ref, *, half):
    @pl.when(pl.program_id(1) == 0)
    def _():
        for par in range(2):
            p = _dot(a_ref[par].astype(BF16), cg_ref[...])
            stack_ref[par, 0:half, :] = p[:, :D_MIX].astype(BF16)
            stack_ref[par, half:2 * half, :] = p[:, D_MIX:].astype(BF16)

    scale = 1.0 / math.sqrt(2 * half * (D_FOURIER // FOURIER_GROUPS))
    ev = _dot(te_ref[...], stack_ref[0])
    od = _dot(to_ref[...], stack_ref[1])
    o_ref[0] = ((ev + od) * scale).astype(o_ref.dtype)
    o_ref[1] = ((ev - od) * scale).astype(o_ref.dtype)


def _fourier(p4, cg, te, to, *, tl):
    nb, _, half, _ = p4.shape
    out = pl.pallas_call(
        functools.partial(_fourier_kernel, half=half),
        grid=(nb, half // tl),
        in_specs=[
            pl.BlockSpec((None, 2, half, D_MIX), lambda b, j: (b, 0, 0, 0)),
            _const_spec(cg.shape),
            pl.BlockSpec((tl, 2 * half), lambda b, j: (j, 0)),
            pl.BlockSpec((tl, 2 * half), lambda b, j: (j, 0)),
        ],
        out_specs=pl.BlockSpec((None, 2, tl, D_MIX), lambda b, j: (b, 0, j, 0)),
        out_shape=jax.ShapeDtypeStruct((nb, 2, half, D_MIX), BF16),
        scratch_shapes=[pltpu.VMEM((2, 2 * half, D_MIX), BF16)],
        compiler_params=_params("parallel", "arbitrary"),
        name="fourier_mix",
    )(p4, cg, te, to)
    return out.reshape(nb * 2 * half, D_MIX)


def _hyena_filter_kernel(z_ref, w1_ref, b1_ref, w2_ref, b2_ref, w3f_ref, w3b_ref, freq_ref, delta_ref,
                         fwd_ref, bwd_ref, hdn_ref):
    hi = lax.Precision.HIGHEST
    z = z_ref[...]

    @pl.when(pl.program_id(0) == 0)
    def _():
        freq = freq_ref[...]
        h1 = jnp.sin(freq * (jnp.dot(z, w1_ref[...], precision=hi, preferred_element_type=F32) + b1_ref[...]))
        hdn_ref[...] = jnp.sin(freq * (jnp.dot(h1, w2_ref[...], precision=hi, preferred_element_type=F32)
                                       + b2_ref[...]))

    hdn = hdn_ref[...]
    decay = jnp.exp(-z[:, 0:1] * delta_ref[...])
    fwd = jnp.dot(hdn, w3f_ref[...], precision=hi, preferred_element_type=F32) * decay
    bwd = jnp.dot(hdn, w3b_ref[...], precision=hi, preferred_element_type=F32) * decay
    row = lax.broadcasted_iota(jnp.int32, bwd.shape, 0)
    bwd = jnp.where(row == 0, 0.0, bwd)
    norm = jnp.sum(jnp.abs(fwd), axis=0, keepdims=True) + jnp.sum(jnp.abs(bwd), axis=0, keepdims=True) + 1e-6
    inv = 1.0 / norm
    fwd_ref[...] = fwd * inv
    bwd_ref[...] = bwd * inv


def _hyena_filters(seq, w1, b1, w2, b2, w3, freq):
    pos = jnp.concatenate([jnp.arange(0, seq, 2), jnp.arange(1, seq, 2)]).astype(F32)[:, None]
    t = pos / (seq - 1)
    bands = (HY_EMB - 1) // 2
    fr = jnp.linspace(1e-4, bands - 1, bands, dtype=F32)[None, :]
    w = 2.0 * math.pi * pos * fr / seq
    z = jnp.concatenate([t, jnp.cos(w), -jnp.sin(w), jnp.zeros((seq, LANE - HY_EMB), F32)], axis=-1)
    pad = LANE - HY_HID
    w1p = jnp.pad(w1, ((0, LANE - HY_EMB), (0, pad)))
    w2p = jnp.pad(w2, ((0, pad), (0, pad)))
    w3p = jnp.pad(w3, ((0, pad), (0, 0)))
    row = lambda v: jnp.pad(v, (0, pad)).reshape(1, LANE)
    delta = jnp.abs(jnp.linspace(math.log(HY_TARGET) / HY_SLOW_DECAY, math.log(HY_TARGET) / HY_FAST_DECAY,
                                 D_HYENA, dtype=F32)).reshape(1, D_HYENA)
    sq = lambda: _const_spec((LANE, LANE))
    vec = lambda: _const_spec((1, LANE))
    n_ord = 2
    return pl.pallas_call(
        _hyena_filter_kernel,
        grid=(n_ord,),
        in_specs=[_const_spec((seq, LANE)), sq(), vec(), sq(), vec(),
                  pl.BlockSpec((LANE, D_MIX), lambda o: (0, o)),
                  pl.BlockSpec((LANE, D_MIX), lambda o: (0, n_ord + o)),
                  vec(), _const_spec((1, D_MIX))],
        out_specs=[pl.BlockSpec((seq, D_MIX), lambda o: (0, o))] * 2,
        out_shape=[jax.ShapeDtypeStruct((seq, n_ord * D_MIX), F32)] * 2,
        scratch_shapes=[pltpu.VMEM((seq, LANE), F32)],
        compiler_params=_params("arbitrary"),
        name="hyena_filters",
    )(z, w1p, row(b1), w2p, row(b2), w3p, w3p, row(freq), delta)


def _alt_sum(x):
    row = lax.broadcasted_iota(jnp.int32, x.shape, 0)
    return jnp.sum(jnp.where((row & 1) == 1, -x, x), axis=0, keepdims=True)


def _alt_rows(v, shape):
    row = lax.broadcasted_iota(jnp.int32, shape, 0)
    return jnp.where((row & 1) == 1, -v, v)


def _half_spectrum(xe, xo, ce, se, co, so):
    ac, bc = _dot(ce, xe), _dot(co, xo)
    a_s, b_s = _dot(se, xe), _dot(so, xo)
    return ac + bc, a_s + b_s, ac - bc, b_s - a_s


def _hyena_spec_kernel(ce_ref, se_ref, co_ref, so_ref, fwd_ref, bwd_ref, alo_ref, blo_ref, ahi_ref, bhi_ref, mid_ref,
                       *, half):
    n = 4 * half
    tabs = (ce_ref[...], se_ref[...], co_ref[...], so_ref[...])
    fwd, bwd = fwd_ref[...], bwd_ref[...]
    fe, fo, be, bo = fwd[:half], fwd[half:], bwd[:half], bwd[half:]
    ce, se, co, so = tabs
    se_, so_ = (fe + be).astype(BF16), (fo + bo).astype(BF16)
    de_, do_ = (be - fe).astype(BF16), (bo - fo).astype(BF16)
    ac, bc = _dot(ce, se_), _dot(co, so_)
    a_s, b_s = _dot(se, de_), _dot(so, do_)
    row = lax.broadcasted_iota(jnp.int32, ac.shape, 0)
    wgt = jnp.where(row == 0, 1.0 / n, 2.0 / n)
    alo_ref[...] = wgt * (ac + bc)
    blo_ref[...] = wgt * (a_s + b_s)
    ahi_ref[...] = wgt * (ac - bc)
    bhi_ref[...] = wgt * (b_s - a_s)
    mid_ref[0:1, :] = (2.0 / n) * (_alt_sum(fe) + _alt_sum(be))
    mid_ref[1:2, :] = (2.0 / n) * (_alt_sum(bo) - _alt_sum(fo))


def _hyena_spectra(tabs, fwd, bwd, *, cb=256):
    seq, width = fwd.shape
    half = seq // 2
    blk = lambda rows: pl.BlockSpec((rows, cb), lambda j: (0, j))
    return pl.pallas_call(
        functools.partial(_hyena_spec_kernel, half=half),
        grid=(width // cb,),
        in_specs=[_const_spec((half, half))] * 4 + [blk(seq), blk(seq)],
        out_specs=[blk(half)] * 4 + [blk(2)],
        out_shape=[jax.ShapeDtypeStruct((half, width), F32)] * 4 + [jax.ShapeDtypeStruct((2, width), F32)],
        compiler_params=_params("parallel"),
        name="hyena_spectra",
    )(*tabs[:4], fwd, bwd)


def _short_conv(e, o, w, b):
    half = e.shape[0]
    row = lax.broadcasted_iota(jnp.int32, e.shape, 0)
    o_prev = jnp.where(row == 0, 0.0, pltpu.roll(o, 1, 0))
    e_next = jnp.where(row == half - 1, 0.0, pltpu.roll(e, half - 1, 0))
    w0, w1, w2 = w[0:1, :], w[1:2, :], w[2:3, :]
    return o_prev * w0 + e * w1 + o * w2 + b, e * w0 + o * w1 + e_next * w2 + b


HYENA_ROW_SPLIT = 2


def _hyena_order_kernel(z_ref, x_ref, ce_ref, se_ref, co_ref, so_ref, cot_ref, sot_ref, alo_ref, blo_ref, ahi_ref,
                        bhi_ref, mid_ref, skip_ref, wx_ref, bx_ref, *rest, order, first):
    if first:
        wz_ref, bz0_ref, o_ref = rest
        ze, zo = _short_conv(z_ref[0], z_ref[1], wz_ref[...], bz0_ref[...])
    else:
        (o_ref,) = rest
        ze, zo = z_ref[0], z_ref[1]
    half = ze.shape[0]
    groups = [slice(r, r + half // HYENA_ROW_SPLIT) for r in range(0, half, half // HYENA_ROW_SPLIT)]
    zeb, zob = ze.astype(BF16), zo.astype(BF16)
    prods = []
    for rs in groups:
        re_lo, im_lo, re_hi, im_hi = _half_spectrum(zeb, zob, ce_ref[rs, :], se_ref[rs, :], co_ref[rs, :],
                                                    so_ref[rs, :])
        alo, blo, ahi, bhi = alo_ref[rs, :], blo_ref[rs, :], ahi_ref[rs, :], bhi_ref[rs, :]
        wr_lo, wi_lo = re_lo * alo + im_lo * blo, re_lo * blo - im_lo * alo
        wr_hi, wi_hi = re_hi * ahi + im_hi * bhi, re_hi * bhi - im_hi * ahi
        prods.append(((wr_lo + wr_hi).astype(BF16), (wi_lo - wi_hi).astype(BF16),
                      (wr_lo - wr_hi).astype(BF16), (wi_lo + wi_hi).astype(BF16)))
    p_even_c, p_even_s, p_odd_c, p_odd_s = (jnp.concatenate(p, axis=0) for p in zip(*prods))
    me, mo = _short_conv(x_ref[0], x_ref[1], wx_ref[...], bx_ref[...])
    re_mid, im_mid = _alt_sum(ze), _alt_sum(zo)
    a_mid, b_mid = mid_ref[0:1, :], mid_ref[1:2, :]
    mid_e, mid_o = re_mid * a_mid + im_mid * b_mid, re_mid * b_mid - im_mid * a_mid
    skip = skip_ref[order:order + 1, :]
    for rs in groups:
        conv_e = _dot(ce_ref[rs, :], p_even_c) - _dot(se_ref[rs, :], p_even_s)
        conv_o = _dot(cot_ref[rs, :], p_odd_c) - _dot(sot_ref[rs, :], p_odd_s)
        conv_e = conv_e + _alt_rows(mid_e, conv_e.shape)
        conv_o = conv_o - _alt_rows(mid_o, conv_o.shape)
        o_ref[0, rs, :] = (me[rs] * (conv_e + ze[rs] * skip)).astype(o_ref.dtype)
        o_ref[1, rs, :] = (mo[rs] * (conv_o + zo[rs] * skip)).astype(o_ref.dtype)


def _hyena_order(z4, z_blk0, x4, x_blk0, tabs, coefs, skip, conv_w, conv_b, *, order, first, out_dtype, cb=256):
    nb, _, half, _ = x4.shape
    ncb = D_MIX // cb
    xw0 = (1 + order) * ncb
    conv_b = conv_b.reshape(1, -1)
    tok = lambda blk0: pl.BlockSpec((None, 2, half, cb), lambda j, b: (b, 0, 0, blk0 + j))
    colblk = lambda rows, blk0: pl.BlockSpec((rows, cb), lambda j, b: (0, blk0 + j), pipeline_mode=pl.Buffered(1))
    in_specs = [tok(z_blk0), tok(x_blk0)] + [_const_spec((half, half))] * 6
    in_specs += [colblk(half, order * ncb)] * 4 + [colblk(2, order * ncb), colblk(2, 0), colblk(3, xw0), colblk(1, xw0)]
    args = [z4, x4, *tabs, *coefs, skip, conv_w, conv_b]
    if first:
        in_specs += [colblk(3, 0), colblk(1, 0)]
        args += [conv_w, conv_b]
    return pl.pallas_call(
        functools.partial(_hyena_order_kernel, order=order, first=first),
        grid=(ncb, nb),
        in_specs=in_specs,
        out_specs=pl.BlockSpec((None, 2, half, cb), lambda j, b: (b, 0, 0, j)),
        out_shape=jax.ShapeDtypeStruct((nb, 2, half, D_MIX), out_dtype),
        compiler_params=_params("parallel", "parallel"),
        name="hyena_order",
    )(*args)


def _dit_tables(half, period):
    split = math.gcd(half, 64)

    def part(rows, mult, col_off):
        r = lax.broadcasted_iota(jnp.int32, (rows, half), 0) * mult
        c = 2 * lax.broadcasted_iota(jnp.int32, (rows, half), 1) + col_off
        ang = ((r * c) % period).astype(F32) * (2.0 * math.pi / period)
        return jnp.cos(ang), jnp.sin(ang)

    out = []
    for col_off in (0, 1):
        ca, sa = (t[:, None, :] for t in part(half // split, split, col_off))
        cb, sb = (t[None, :, :] for t in part(split, 1, col_off))
        out += [(ca * cb - sa * sb).reshape(half, half), (sa * cb + ca * sb).reshape(half, half)]
    return out


def _fourier_tables(seq):
    ce, se, co, so = _dit_tables(seq // 2, seq)
    te = jnp.concatenate([ce, -se], axis=1).astype(BF16)
    to = jnp.concatenate([co, -so], axis=1).astype(BF16)
    gsz = D_FOURIER // FOURIER_GROUPS
    r = lax.broadcasted_iota(jnp.int32, (gsz, gsz), 0)
    c = lax.broadcasted_iota(jnp.int32, (gsz, gsz), 1)
    ang = ((r * c) % gsz).astype(F32) * (2.0 * math.pi / gsz)
    eye = jnp.eye(FOURIER_GROUPS, dtype=F32)
    cg = jnp.concatenate([jnp.kron(eye, jnp.cos(ang)), jnp.kron(eye, jnp.sin(ang))], axis=1).astype(BF16)
    return te, to, cg


def _hyena_tables(seq):
    ce, se, co, so = (t.astype(BF16) for t in _dit_tables(seq // 2, 2 * seq))
    return ce, se, co, so, co.T, so.T


def _rms_heads(x, g):
    parts = []
    for h in range(D_MIX // LANE):
        xh = x[:, h * LANE:(h + 1) * LANE]
        parts.append(xh * lax.rsqrt(jnp.mean(xh * xh, axis=-1, keepdims=True) + LN_EPS) * g)
    return jnp.concatenate(parts, axis=-1)


def _outproj_kernel(x_ref, ya_ref, yb_ref, *rest, tm, seq, is_ctx, readout):
    if readout:
        r_ref, gd_ref, gg_ref, hg_ref, unperm_ref, mod_ref, w_ref, g_ref, b_ref, o_ref = rest
        r = r_ref[...]
        ya = (_rms_heads(ya_ref[...], gg_ref[...]) * (r * _sigmoid(r))).astype(BF16)
        yb = _rms_heads(yb_ref[...].reshape(tm, D_MIX) * _sigmoid(gd_ref[...].reshape(tm, D_MIX)), hg_ref[...])
        yb = _dot(unperm_ref[...], yb.astype(BF16)).astype(BF16)
    else:
        unperm_ref, mod_ref, w_ref, g_ref, b_ref, o_ref = rest
        ya = ya_ref[...]
        half = DEINT_BLOCK // 2
        parts = []
        for r in range(0, tm // 2, half):
            blk = jnp.concatenate([yb_ref[0, r:r + half, :], yb_ref[1, r:r + half, :]], axis=0)
            parts.append(_dot(unperm_ref[...], blk).astype(BF16))
        yb = parts[0] if len(parts) == 1 else jnp.concatenate(parts, axis=0)
    row = _mod_row_index(pl.program_id(0), tm, seq, is_ctx)
    gate = _mod_row(mod_ref, row, 5)
    y = _dot(ya, w_ref[:D_MIX, :]) + _dot(yb, w_ref[D_MIX:, :])
    z = ALPHA * x_ref[...] + gate * y
    o_ref[...] = _ln(z) * g_ref[...] + b_ref[...]


def _outproj(x2, ya, yb, mod, layer, w, ln_g, ln_b, *, seq, is_ctx, readout=None):
    n = x2.shape[0]
    tm = _token_tile(n, seq, False, want=TOKEN_TILE if readout is None else TOKEN_TILE // 2)
    tiles = seq // tm
    tok = lambda: pl.BlockSpec((tm, D_MIX), lambda i: (i, 0))
    in_specs = [pl.BlockSpec((tm, D_MODEL), lambda i: (i, 0)), tok()]
    args = [x2, ya, yb]
    if readout is None:
        assert tm % DEINT_BLOCK == 0
        unperm = _deint_perm(inverse=True)
        in_specs += [pl.BlockSpec((None, 2, tm // 2, D_MIX), lambda i: (i // tiles, 0, i % tiles, 0)),
                     _const_spec(unperm.shape)]
        args += [unperm]
    else:
        p_main, p_hg, gla_g, hg_g = readout
        rows = tm // GRID_W
        assert tm % GRID_W == 0 and rows % 8 == 0
        i = lax.broadcasted_iota(jnp.int32, (tm, tm), 0)
        j = lax.broadcasted_iota(jnp.int32, (tm, tm), 1)
        unperm = jnp.where(j == (i % GRID_W) * rows + i // GRID_W, 1.0, 0.0).astype(BF16)
        colmajor = lambda blk: pl.BlockSpec((None, GRID_W, rows, D_MIX), lambda i: (i // tiles, 0, i % tiles, blk))
        in_specs += [colmajor(0), pl.BlockSpec((tm, D_MIX), lambda i: (i, PM_R)), colmajor(PH_GD),
                     _const_spec((1, LANE)), _const_spec((1, LANE)), _const_spec((tm, tm))]
        args += [p_main, p_hg, gla_g.reshape(1, LANE), hg_g.reshape(1, LANE), unperm]
    in_specs += [_picked_spec(mod.shape, (layer,)), _const_spec(w.shape),
                 _picked_spec(ln_g.shape, (layer, 1)), _picked_spec(ln_b.shape, (layer, 1))]
    args += [mod, w, ln_g, ln_b]
    kern = functools.partial(_outproj_kernel, tm=tm, seq=seq, is_ctx=is_ctx, readout=readout is not None)
    return pl.pallas_call(
        kern,
        grid=(n // tm,),
        in_specs=in_specs,
        out_specs=pl.BlockSpec((tm, D_MODEL), lambda i: (i, 0)),
        out_shape=jax.ShapeDtypeStruct((n, D_MODEL), F32),
        compiler_params=_params("parallel"),
        name="mixer_out",
    )(*args)


def _split3(x):
    hi = x.astype(BF16)
    r1 = x - hi.astype(F32)
    mid = r1.astype(BF16)
    lo = (r1 - mid.astype(F32)).astype(BF16)
    return hi, mid, lo


GLA_UNROLL = 16
HGRN_UNROLL = 8
CUMSUM_CHUNKS = 4


def _scan_unroll(n_chunks, want):
    return math.gcd(want, n_chunks)


def _scan_consts(rev, unroll, heads, dk):
    n = math.gcd(unroll, CUMSUM_CHUNKS) * CHUNK
    i = lax.broadcasted_iota(jnp.int32, (n, n), 0)
    j = lax.broadcasted_iota(jnp.int32, (n, n), 1)
    same_chunk = (i // CHUNK) == (j // CHUNK)
    tri = jnp.where(same_chunk & ((j >= i) if rev else (j <= i)), 1.0, 0.0).astype(BF16)
    r = lax.broadcasted_iota(jnp.int32, (CHUNK, heads * CHUNK), 0)
    c = lax.broadcasted_iota(jnp.int32, (CHUNK, heads * CHUNK), 1) & (CHUNK - 1)
    causal = (c >= r) if rev else (c <= r)
    klane_head = lax.broadcasted_iota(jnp.int32, (1, heads * dk), 1) // dk
    return tri, causal, klane_head


def _block_diag(blocks):
    z = jnp.zeros_like(blocks[0])
    return jnp.concatenate([jnp.concatenate([b if c == h else z for c in range(len(blocks))], axis=1)
                            for h, b in enumerate(blocks)], axis=0)


def _scan_local(q, k, g, vb, consts, *, rev, heads, dk, need_out):
    tri, causal, klane_head = consts
    kt = heads * dk
    parts = _split3(g)
    span = tri.shape[0]
    gsum = jnp.concatenate([sum(_dot(tri, part[r:r + span]) for part in parts) for r in range(0, g.shape[0], span)],
                           axis=0)
    staged = []
    for u in range(g.shape[0] // CHUNK):
        rows = slice(u * CHUNK, (u + 1) * CHUNK)
        gs = gsum[rows]
        gt = gs[0:1] if rev else gs[CHUNK - 1:CHUNK]
        ku, vu = k[rows], vb[rows]
        q_in = o_intra = None
        if need_out:
            q_in = (q[rows] * jnp.exp(gs)).astype(BF16)
            k_dec = ku * jnp.exp(-gs)
            k_in = k_dec.astype(BF16)
            k_out = (k_dec * jnp.exp(gt)).astype(BF16)
            if dk % LANE == 0:
                k4 = _block_diag([k_in[:, h * dk:(h + 1) * dk] for h in range(heads)])
            else:
                k4 = jnp.concatenate([jnp.where(klane_head == h, k_in, jnp.zeros_like(k_in)) for h in range(heads)],
                                     axis=0)
            att = jnp.where(causal, _dot_nt(q_in, k4), 0.0).astype(BF16)
            o_intra = _dot(att, _block_diag([vu[:, h * LANE:(h + 1) * LANE] for h in range(heads)]))
        else:
            k_out = (ku * jnp.exp(gt - gs)).astype(BF16)
        if dk % LANE == 0:
            inc = [_dot_tn(k_out[:, h * dk:(h + 1) * dk], vu[:, h * LANE:(h + 1) * LANE]) for h in range(heads)]
        else:
            k_t = jnp.transpose(jnp.concatenate([k_out.astype(F32), jnp.zeros((LANE - CHUNK, kt), F32)], axis=0))
            k_t = k_t.astype(BF16)
            v_pad = jnp.concatenate([vu, jnp.zeros((LANE - CHUNK, heads * LANE), BF16)], axis=0)
            inc = [_dot(k_t[h * dk:(h + 1) * dk, :], v_pad[:, h * LANE:(h + 1) * LANE]) for h in range(heads)]
        dec = jnp.transpose(jnp.broadcast_to(jnp.exp(gt), (LANE, kt)))
        staged.append((q_in, o_intra, inc, [dec[h * dk:(h + 1) * dk] for h in range(heads)]))
    return staged


def _scan_states(s_ref, base, staged):
    heads = len(staged[0][2])
    state = [s_ref[base + h] for h in range(heads)]
    outs = []
    for q_in, o_intra, inc, decay in staged:
        if q_in is not None:
            outs.append(o_intra + _dot(q_in, _block_diag([s.astype(BF16) for s in state])))
        state = [s * d + i for s, d, i in zip(state, decay, inc)]
    for h in range(heads):
        s_ref[base + h] = state[h]
    return outs


def _gla_kernel(qk_ref, v_ref, g_ref, cqk_ref, cv_ref, cg_ref, o_ref, s_ref, ob_ref, *, seq, ctx_len):
    heads, dk = GLA_HEADS, GLA_DK
    kt = heads * dk
    s_ref[...] = jnp.zeros_like(s_ref)

    def make_run(qk, vv, gg, n_chunks, need_out):
        unroll = _scan_unroll(n_chunks, GLA_UNROLL)
        rows = unroll * CHUNK
        consts = [_scan_consts(rev, unroll, heads, dk) for rev in (False, True)]

        def run(n):
            work = []
            for d, rev in enumerate((False, True)):
                start = (n_chunks * CHUNK - rows - n * rows) if rev else n * rows
                sl = pl.ds(pl.multiple_of(start, rows), rows)
                q = qk[sl, 0:kt] * (dk ** -0.5) if need_out else None
                staged = _scan_local(q, qk[sl, kt:2 * kt], gg[sl, d * kt:(d + 1) * kt], vv[sl, :].astype(BF16),
                                     consts[d], rev=rev, heads=heads, dk=dk, need_out=need_out)
                work.append((d, rev, sl, staged[::-1] if rev else staged))
            for d, rev, sl, staged in work:
                outs = _scan_states(s_ref, d * heads, staged)
                if need_out:
                    (ob_ref if rev else o_ref)[sl, :] = jnp.concatenate(outs[::-1] if rev else outs, axis=0)

        return run, n_chunks // unroll

    run, steps = make_run(cqk_ref, cv_ref, cg_ref, ctx_len // CHUNK, False)
    for n in range(steps):
        run(n)
    run, steps = make_run(qk_ref, v_ref, g_ref, seq // CHUNK, True)

    def body(n, carry):
        run(n)
        return carry

    lax.fori_loop(0, steps, body, 0)
    o_ref[...] = o_ref[...] + ob_ref[...]


def _gla(p3, pc3):
    nb, seq, _ = p3.shape
    ctx_len = pc3.shape[1]
    blk = lambda length, j: pl.BlockSpec((None, length, D_MIX), lambda b: (b, 0, j))
    return pl.pallas_call(
        functools.partial(_gla_kernel, seq=seq, ctx_len=ctx_len),
        grid=(nb,),
        in_specs=[blk(seq, PM_QK), blk(seq, PM_V), blk(seq, PM_GATE),
                  blk(ctx_len, P_QK), blk(ctx_len, P_V), blk(ctx_len, P_GATE)],
        out_specs=pl.BlockSpec((None, seq, D_MIX), lambda b: (b, 0, 0)),
        out_shape=jax.ShapeDtypeStruct((nb, seq, D_MIX), F32),
        scratch_shapes=[pltpu.VMEM((2 * GLA_HEADS, GLA_DK, GLA_DV), F32),
                        pltpu.VMEM((seq, D_MIX), F32)],
        compiler_params=_params("parallel"),
        name="gla_scan",
    )(p3, p3, p3, pc3, pc3, pc3)


def _hgrn_kernel(q_ref, ff_ref, fb_ref, v_ref, cff_ref, cfb_ref, cv_ref, o_ref, s_ref, *, seq, ctx_len):
    heads, dk = HG_HEADS, HG_D
    s_ref[...] = jnp.zeros_like(s_ref)

    def make_run(qq, gates, vv, n_chunks, need_out):
        unroll = _scan_unroll(n_chunks, HGRN_UNROLL)
        rows = unroll * CHUNK
        consts = [_scan_consts(rev, unroll, heads, dk) for rev in (False, True)]

        def run(n):
            work = []
            for d, rev in enumerate((False, True)):
                start = (n_chunks * CHUNK - rows - n * rows) if rev else n * rows
                sl = pl.ds(pl.multiple_of(start, rows), rows)
                f = gates[d][sl, :]
                staged = _scan_local(qq[sl, :] if need_out else None, 1.0 - f, jnp.log(f), vv[sl, :].astype(BF16), consts[d],
                                     rev=rev, heads=heads, dk=dk, need_out=need_out)
                work.append((d, rev, sl, staged[::-1] if rev else staged))
            for d, rev, sl, staged in work:
                outs = _scan_states(s_ref, d * heads, staged)
                if need_out:
                    o_ref[sl, :] += jnp.concatenate(outs[::-1] if rev else outs, axis=0)

        return run, n_chunks // unroll

    run, steps = make_run(None, (cff_ref, cfb_ref), cv_ref, ctx_len // CHUNK, False)
    for n in range(steps):
        run(n)
    o_ref[...] = jnp.zeros_like(o_ref)
    run, steps = make_run(q_ref, (ff_ref, fb_ref), v_ref, seq // CHUNK, True)

    def body(n, carry):
        run(n)
        return carry

    lax.fori_loop(0, steps, body, 0)


def _hgrn(ph3, pc3):
    nb, seq, _ = ph3.shape
    ctx_len = pc3.shape[1]
    blk = lambda length, j: pl.BlockSpec((None, length, D_MIX), lambda b: (b, 0, j))
    return pl.pallas_call(
        functools.partial(_hgrn_kernel, seq=seq, ctx_len=ctx_len),
        grid=(nb,),
        in_specs=[blk(seq, PH_QD), blk(seq, PH_FFL), blk(seq, PH_FBL), blk(seq, PH_IDD),
                  blk(ctx_len, P_FFL), blk(ctx_len, P_FBL), blk(ctx_len, P_IDD)],
        out_specs=pl.BlockSpec((None, seq, D_MIX), lambda b: (b, 0, 0)),
        out_shape=jax.ShapeDtypeStruct((nb, seq, D_MIX), F32),
        scratch_shapes=[pltpu.VMEM((2 * HG_HEADS, HG_D, HG_D), F32)],
        compiler_params=_params("parallel"),
        name="hgrn_scan",
    )(ph3, ph3, ph3, ph3, pc3, pc3, pc3)


def _even_mixer(x2, nb, seq, mod, layer, is_ctx, w_in, w_out, conv_w, conv_b, filt, skip, ln_g, ln_b):
    p4 = _proj_deint(x2, nb, seq, mod, layer, w_in, is_ctx=is_ctx)
    te, to, cg = _fourier_tables(seq)
    ya = _fourier(p4, cg, te, to, tl=min(seq // 2, 512))
    tabs = _hyena_tables(seq)
    fwd, bwd = _hyena_filters(seq, *filt)
    coefs = _hyena_spectra(tabs, fwd, bwd)
    cb = 256
    u0 = D_FOURIER // cb
    per = D_HYENA // cb
    y1 = _hyena_order(p4, u0, p4, u0 + per, tabs, coefs, skip, conv_w, conv_b, order=0, first=True, out_dtype=F32,
                      cb=cb)
    y2 = _hyena_order(y1, 0, p4, u0 + 2 * per, tabs, coefs, skip, conv_w, conv_b, order=1, first=False,
                      out_dtype=BF16, cb=cb)
    return _outproj(x2, ya, y2, mod, layer, w_out, ln_g, ln_b, seq=seq, is_ctx=is_ctx)


def _odd_proj_weights(w_in, a_up, a_b):
    kq = GLA_HEADS * GLA_DK
    low0 = 2 * kq + 2 * D_MIX
    low1 = low0 + 2 * GLA_RANK
    w = jnp.concatenate([w_in[:, :low0], w_in[:, low1:], w_in[:, low0:low1],
                         jnp.zeros((D_MODEL, LANE - 2 * GLA_RANK), w_in.dtype)], axis=1).astype(BF16)
    up = jnp.zeros((LANE, 2 * kq), F32)
    up = up.at[0:GLA_RANK, 0:kq].set(a_up[0]).at[GLA_RANK:2 * GLA_RANK, kq:].set(a_up[1]).astype(BF16)
    return w, up, a_b.reshape(1, 2 * kq)


def kernel(x, c, ctx, c_ctx, mod_w, mod_b, ffn_w_in, ffn_w_out, ln_g, ln_b, ev_w_in, ev_w_out, hy_conv_w, hy_conv_b, hy_w1, hy_b1, hy_w2, hy_b2, hy_w3, hy_freq, hy_skip, od_w_in, od_w_out, gla_a_up, gla_a_b, gla_norm_g, hg_lb, hg_norm_g):
    nb, seq, d = x.shape
    ctx_len = ctx.shape[1]
    assert d == D_MODEL and seq % 512 == 0 and ctx_len % CHUNK == 0 and nb + 1 <= MOD_ROWS - 7
    assert mod_w.shape[0] == DEPTH == 2

    c_all = jnp.zeros((MOD_ROWS, D_MODEL), F32).at[:nb].set(c).at[MOD_ROWS - 8].set(c_ctx)
    mod = _mod_table(c_all, mod_w, mod_b)
    w_in_b = ffn_w_in.astype(BF16)
    w_out_b = ffn_w_out.astype(BF16)

    xs = x.reshape(nb * seq, D_MODEL)
    xc = ctx.reshape(nb * ctx_len, D_MODEL)
    streams = [(xs, seq, False), (xc, ctx_len, True)]

    ln_g4 = ln_g.reshape(DEPTH, 3, 1, D_MODEL)
    ln_b4 = ln_b.reshape(DEPTH, 3, 1, D_MODEL)

    def ffn(stream, layer, half):
        arr, length, is_ctx = stream
        return (_ffn(arr, mod, layer, half, w_in_b, w_out_b, ln_g4, ln_b4, seq=length, is_ctx=is_ctx), length, is_ctx)

    streams = [ffn(s, 0, 0) for s in streams]
    filt = (hy_w1[0], hy_b1[0], hy_w2[0], hy_b2[0], hy_w3[0], hy_freq[0])
    ev_in_b, ev_out_b = ev_w_in[0].astype(BF16), ev_w_out[0].astype(BF16)
    streams = [(_even_mixer(arr, nb, length, mod, 0, is_ctx, ev_in_b, ev_out_b, hy_conv_w[0], hy_conv_b[0], filt,
                            hy_skip[0], ln_g4, ln_b4), length, is_ctx) for arr, length, is_ctx in streams]
    streams = [ffn(s, 0, 1) for s in streams]

    streams = [ffn(s, 1, 0) for s in streams]
    (xs, _, _), (xc, _, _) = streams
    w_proj, up, up_b = _odd_proj_weights(od_w_in[0], gla_a_up[0], gla_a_b[0])
    n_gla, n_hg = (P_R + 1) * D_MIX, (PH_GD + 1) * D_MIX
    w_main = jnp.concatenate([w_proj[:, :n_gla], w_proj[:, n_gla + n_hg:]], axis=1)
    pc = _proj(xc, mod, 1, w_proj, seq=ctx_len, is_ctx=True, gates=(up, up_b), forget=(hg_lb, P_FFL * D_MIX),
               want=TOKEN_TILE // 2)
    pm = _proj(xs, mod, 1, w_main, seq=seq, is_ctx=False, gates=(up, up_b))
    ph = _proj_colmajor(xs, nb, seq, mod, 1, w_proj[:, n_gla:n_gla + n_hg], hg_lb)
    pc3 = pc.reshape(nb, ctx_len, -1)
    o_gla = _gla(pm.reshape(nb, seq, -1), pc3).reshape(nb * seq, D_MIX)
    o_hg = _hgrn(ph.reshape(nb, seq, -1), pc3).reshape(ph.shape[:3] + (D_MIX,))
    xs = _outproj(xs, o_gla, o_hg, mod, 1, od_w_out[0].astype(BF16), ln_g4, ln_b4, seq=seq, is_ctx=False,
                  readout=(pm, ph, gla_norm_g[0], hg_norm_g[0]))
    xs, _, _ = ffn((xs, seq, False), 1, 1)
    return xs.reshape(nb, seq, D_MODEL)
```

```python
import functools
import math

import jax
import jax.numpy as jnp
from jax import lax
from jax.experimental import pallas as pl
from jax.experimental.pallas import tpu as pltpu

F32 = jnp.float32
BF16 = jnp.bfloat16

D_MODEL = 1024
DEPTH = 2
N_MOD = 9
D_FF = 2816
LN_EPS = 1e-6
ALPHA = (2.0 * DEPTH) ** 0.25
GRID_W = 64
D_FOURIER = 512
FOURIER_GROUPS = 8
D_HYENA = 512
HY_EMB = 33
HY_HID = 64
HY_FAST_DECAY = 0.3
HY_SLOW_DECAY = 1.5
HY_TARGET = 1e-2
CHUNK = 64
GLA_HEADS = 4
GLA_DK = 64
GLA_DV = 128
GLA_RANK = 16
GLA_TAU = 16.0
HG_HEADS = 4
HG_D = 128
D_MIX = 512

LANE = 128
MOD_ROWS = 24
VMEM_LIMIT = 56 * 1024 * 1024

P_QK, P_V, P_R, P_QD, P_FFL, P_FBL, P_IDD, P_GD, P_GATE = range(9)
PM_QK, PM_V, PM_R, PM_GATE = range(4)
PH_QD, PH_FFL, PH_FBL, PH_IDD, PH_GD = range(5)


def _params(*sem):
    return pltpu.CompilerParams(dimension_semantics=sem, vmem_limit_bytes=VMEM_LIMIT)


def _dot(a, b):
    return jnp.dot(a, b, preferred_element_type=F32)


def _dot_nt(a, b):
    return lax.dot_general(a, b, (((1,), (1,)), ((), ())), preferred_element_type=F32)


def _dot_tn(a, b):
    return lax.dot_general(a, b, (((0,), (0,)), ((), ())), preferred_element_type=F32)


def _ln(x):
    mu = jnp.mean(x, axis=-1, keepdims=True)
    xc = x - mu
    var = jnp.mean(xc * xc, axis=-1, keepdims=True)
    return xc * lax.rsqrt(var + LN_EPS)


def _sigmoid(x):
    return 1.0 / (1.0 + jnp.exp(-x))


def _mod_row(mod_ref, row, k):
    return mod_ref[pl.ds(row, 1), k * D_MODEL:(k + 1) * D_MODEL]


def _mod_row_index(tile, tm, seq, is_ctx):
    if is_ctx:
        return MOD_ROWS - 8
    return (tile * tm) // seq


def _const_spec(shape):
    return pl.BlockSpec(shape, lambda *_: (0,) * len(shape), pipeline_mode=pl.Buffered(1))


def _picked_spec(shape, lead):
    tail = tuple(shape[len(lead):])
    return pl.BlockSpec((None,) * len(lead) + tail, lambda *_: tuple(lead) + (0,) * len(tail),
                        pipeline_mode=pl.Buffered(1))


TOKEN_TILE = 1024
SUB_TILES = 2


def _token_tile(n, seq, is_ctx, want=TOKEN_TILE):
    tm = min(want, n if is_ctx else seq)
    while n % tm or (not is_ctx and seq % tm):
        tm //= 2
    return tm


def _mod_kernel(c_ref, w_ref, b_ref, o_ref):
    c = c_ref[...]
    s = (c * _sigmoid(c)).astype(BF16)
    o_ref[...] = _dot(s, w_ref[...].astype(BF16)) + b_ref[...]


def _mod_table(c_all, mod_w, mod_b):
    return pl.pallas_call(
        _mod_kernel,
        grid=(DEPTH, N_MOD),
        in_specs=[
            _const_spec((MOD_ROWS, D_MODEL)),
            pl.BlockSpec((None, D_MODEL, D_MODEL), lambda l, j: (l, 0, j)),
            pl.BlockSpec((None, 1, D_MODEL), lambda l, j: (l, 0, j)),
        ],
        out_specs=pl.BlockSpec((None, MOD_ROWS, D_MODEL), lambda l, j: (l, 0, j)),
        out_shape=jax.ShapeDtypeStruct((DEPTH, MOD_ROWS, N_MOD * D_MODEL), F32),
        compiler_params=_params("parallel", "parallel"),
        name="mod_table",
    )(c_all, mod_w, mod_b.reshape(DEPTH, 1, N_MOD * D_MODEL))


FFN_CHUNK = 256


def _swiglu(xm, win_ref, wout_ref):
    acc = None
    for j in range(D_FF // FFN_CHUNK):
        lo = j * FFN_CHUNK
        a = _dot(xm, win_ref[:, lo:lo + FFN_CHUNK])
        u = _dot(xm, win_ref[:, D_FF + lo:D_FF + lo + FFN_CHUNK])
        h = (a * _sigmoid(a) * u).astype(BF16)
        y = _dot(h, wout_ref[lo:lo + FFN_CHUNK, :])
        acc = y if acc is None else acc + y
    return acc


def _ffn_kernel(x_ref, mod_ref, win_ref, wout_ref, g_ref, b_ref, o_ref, *, k0, tm, seq, is_ctx):
    row = _mod_row_index(pl.program_id(0), tm, seq, is_ctx)
    shift, scale, gate = (_mod_row(mod_ref, row, k0 + t) for t in range(3))
    subs = [slice(r, r + tm // SUB_TILES) for r in range(0, tm, tm // SUB_TILES)]
    xms = [(_ln(x_ref[s, :]) * (1.0 + scale) + shift).astype(BF16) for s in subs]
    accs = [_swiglu(xm, win_ref, wout_ref) for xm in xms]
    for s, acc in zip(subs, accs):
        z = ALPHA * x_ref[s, :] + (0.5 * gate) * acc
        o_ref[s, :] = _ln(z) * g_ref[...] + b_ref[...]


def _ffn(x2, mod, layer, half, w_in, w_out, ln_g, ln_b, *, seq, is_ctx):
    n = x2.shape[0]
    tm = _token_tile(n, seq, is_ctx)
    kern = functools.partial(_ffn_kernel, k0=6 * half, tm=tm, seq=seq, is_ctx=is_ctx)
    return pl.pallas_call(
        kern,
        grid=(n // tm,),
        in_specs=[
            pl.BlockSpec((tm, D_MODEL), lambda i: (i, 0)),
            _picked_spec(mod.shape, (layer,)),
            _picked_spec(w_in.shape, (layer, half)),
            _picked_spec(w_out.shape, (layer, half)),
            _picked_spec(ln_g.shape, (layer, 2 * half)),
            _picked_spec(ln_b.shape, (layer, 2 * half)),
        ],
        out_specs=pl.BlockSpec((tm, D_MODEL), lambda i: (i, 0)),
        out_shape=jax.ShapeDtypeStruct((n, D_MODEL), F32),
        compiler_params=_params("parallel"),
        name="ffn",
    )(x2, mod, w_in, w_out, ln_g, ln_b)


def _log_sigmoid(x):
    return jnp.minimum(x, 0.0) - jnp.log(1.0 + jnp.exp(-jnp.abs(x)))


def _forget_gates(lb_ref, layer, logits):
    raw = lb_ref[...]
    e = jnp.exp(raw - jnp.max(raw, axis=0, keepdims=True))
    sm = e / jnp.sum(e, axis=0, keepdims=True)
    lb = jnp.sum(sm[0:layer + 1, :], axis=0, keepdims=True) - sm[0:1, :]
    lb = jnp.concatenate([lb, lb], axis=-1)
    return lb + (1.0 - lb) * _sigmoid(logits)


def _proj_kernel(x_ref, mod_ref, w_ref, *rest, tm, seq, is_ctx, gla_gates, forget):
    row = _mod_row_index(pl.program_id(0), tm, seq, is_ctx)
    shift, scale = _mod_row(mod_ref, row, 3), _mod_row(mod_ref, row, 4)
    subs = [slice(r, r + tm // SUB_TILES) for r in range(0, tm, tm // SUB_TILES)]
    xms = [(_ln(x_ref[s, :]) * (1.0 + scale) + shift).astype(BF16) for s in subs]
    ps = [_dot(xm, w_ref[...]) for xm in xms]
    if not gla_gates:
        (o_ref,) = rest
        for s, p in zip(subs, ps):
            o_ref[s, :] = p
        return
    aup_ref, ab_ref, *lb_ref, o_ref = rest
    for s, p in zip(subs, ps):
        main = p.shape[1] - LANE
        o_ref[s, :main] = p[:, :main]
        if forget is not None:
            layer, c0 = forget
            o_ref[s, c0:c0 + 2 * D_MIX] = _forget_gates(lb_ref[0], layer, p[:, c0:c0 + 2 * D_MIX])
        low = p[:, main:].astype(BF16)
        o_ref[s, main:] = _log_sigmoid(_dot(low, aup_ref[...]) + ab_ref[...]) * (1.0 / GLA_TAU)


PERM_COLS = 8


def _proj_colmajor_kernel(x_ref, mod_ref, perm_ref, w_ref, lb_ref, o_ref, *, layer):
    rows, cols, _ = x_ref.shape
    row = pl.program_id(0)
    shift, scale = _mod_row(mod_ref, row, 3), _mod_row(mod_ref, row, 4)
    per_sub = cols // SUB_TILES
    for s in range(SUB_TILES):
        parts = []
        for c in range(s * per_sub, (s + 1) * per_sub, PERM_COLS):
            x = x_ref[:, c:c + PERM_COLS, :].reshape(rows * PERM_COLS, D_MODEL)
            xm = (_ln(x) * (1.0 + scale) + shift).astype(BF16)
            parts.append(_dot(perm_ref[...], xm).astype(BF16))
        p = _dot(jnp.concatenate(parts, axis=0), w_ref[...])
        c0, c1 = PH_FFL * D_MIX, (PH_FBL + 1) * D_MIX
        p = jnp.concatenate([p[:, :c0], _forget_gates(lb_ref, layer, p[:, c0:c1]), p[:, c1:]], axis=-1)
        o_ref[s * per_sub:(s + 1) * per_sub] = p.reshape(per_sub, rows, w_ref.shape[1])


def _proj_colmajor(x2, nb, seq, mod, layer, w, hg_lb, *, cols=32):
    rows = seq // GRID_W
    n_out = w.shape[1]
    n = rows * PERM_COLS
    i = lax.broadcasted_iota(jnp.int32, (n, n), 0)
    j = lax.broadcasted_iota(jnp.int32, (n, n), 1)
    perm = jnp.where(j == (i % rows) * PERM_COLS + i // rows, 1.0, 0.0).astype(BF16)
    return pl.pallas_call(
        functools.partial(_proj_colmajor_kernel, layer=layer),
        grid=(nb, GRID_W // cols),
        in_specs=[pl.BlockSpec((None, rows, cols, D_MODEL), lambda b, c: (b, 0, c, 0)),
                  _picked_spec(mod.shape, (layer,)), _const_spec(perm.shape), _const_spec(w.shape),
                  _const_spec(hg_lb.shape)],
        out_specs=pl.BlockSpec((None, cols, rows, n_out), lambda b, c: (b, c, 0, 0)),
        out_shape=jax.ShapeDtypeStruct((nb, GRID_W, rows, n_out), F32),
        compiler_params=_params("parallel", "parallel"),
        name="mixer_proj_colmajor",
    )(x2.reshape(nb, rows, GRID_W, D_MODEL), mod, perm, w, hg_lb)


def _proj(x2, mod, layer, w, *, seq, is_ctx, gates=None, forget=None, want=TOKEN_TILE):
    n = x2.shape[0]
    tm = _token_tile(n, seq, is_ctx, want)
    n_out = w.shape[1] - LANE + gates[0].shape[1] if gates is not None else w.shape[1]
    kern = functools.partial(_proj_kernel, tm=tm, seq=seq, is_ctx=is_ctx, gla_gates=gates is not None,
                             forget=None if forget is None else (layer, forget[1]))
    extra = [] if gates is None else list(gates)
    if forget is not None:
        extra.append(forget[0])
    return pl.pallas_call(
        kern,
        grid=(n // tm,),
        in_specs=[pl.BlockSpec((tm, D_MODEL), lambda i: (i, 0)), _picked_spec(mod.shape, (layer,)),
                  _const_spec(w.shape)] + [_const_spec(e.shape) for e in extra],
        out_specs=pl.BlockSpec((tm, n_out), lambda i: (i, 0)),
        out_shape=jax.ShapeDtypeStruct((n, n_out), F32),
        compiler_params=_params("parallel"),
        name="mixer_proj",
    )(x2, mod, w, *extra)


DEINT_BLOCK = 256


def _deint_perm(inverse=False):
    i = lax.broadcasted_iota(jnp.int32, (DEINT_BLOCK, DEINT_BLOCK), 0)
    j = lax.broadcasted_iota(jnp.int32, (DEINT_BLOCK, DEINT_BLOCK), 1)
    half = DEINT_BLOCK // 2
    src = 2 * (i % half) + i // half
    hit = (i == 2 * (j % half) + j // half) if inverse else (j == src)
    return jnp.where(hit, 1.0, 0.0).astype(BF16)


def _proj_deint_kernel(x_ref, mod_ref, perm_ref, w_ref, o_ref, *, tm, seq, is_ctx):
    row = _mod_row_index(pl.program_id(0), tm, seq, is_ctx)
    shift, scale = _mod_row(mod_ref, row, 3), _mod_row(mod_ref, row, 4)
    half = DEINT_BLOCK // 2
    n_sub = SUB_TILES if tm % (SUB_TILES * DEINT_BLOCK) == 0 else 1
    sub = tm // n_sub
    for s in range(n_sub):
        parts = []
        for r in range(s * sub, (s + 1) * sub, DEINT_BLOCK):
            xm = (_ln(x_ref[r:r + DEINT_BLOCK, :]) * (1.0 + scale) + shift).astype(BF16)
            parts.append(_dot(perm_ref[...], xm).astype(BF16))
        p = _dot(parts[0] if len(parts) == 1 else jnp.concatenate(parts, axis=0), w_ref[...])
        for k in range(sub // DEINT_BLOCK):
            dst = (s * sub + k * DEINT_BLOCK) // 2
            o_ref[0, dst:dst + half, :] = p[k * DEINT_BLOCK:k * DEINT_BLOCK + half]
            o_ref[1, dst:dst + half, :] = p[k * DEINT_BLOCK + half:(k + 1) * DEINT_BLOCK]


def _proj_deint(x2, nb, seq, mod, layer, w, *, is_ctx):
    n = x2.shape[0]
    tm = _token_tile(n, seq, False)
    assert tm % DEINT_BLOCK == 0
    tiles = seq // tm
    n_out = w.shape[1]
    perm = _deint_perm()
    return pl.pallas_call(
        functools.partial(_proj_deint_kernel, tm=tm, seq=seq, is_ctx=is_ctx),
        grid=(n // tm,),
        in_specs=[pl.BlockSpec((tm, D_MODEL), lambda i: (i, 0)), _picked_spec(mod.shape, (layer,)),
                  _const_spec(perm.shape), _const_spec(w.shape)],
        out_specs=pl.BlockSpec((None, 2, tm // 2, n_out), lambda i: (i // tiles, 0, i % tiles, 0)),
        out_shape=jax.ShapeDtypeStruct((nb, 2, seq // 2, n_out), F32),
        compiler_params=_params("parallel"),
        name="mixer_proj_deint",
    )(x2, mod, perm, w)


def _fourier_kernel(a_ref, cg_ref, te_ref, to_ref, o_ref, stack_ref, *, half):
    @pl.when(pl.program_id(1) == 0)
    def _():
        for par in range(2):
            for c0 in range(0, D_MIX, LANE):
                p = _dot(a_ref[par, :, c0:c0 + LANE].astype(BF16), cg_ref[...])
                stack_ref[par, 0:half, c0:c0 + LANE] = p[:, :LANE].astype(BF16)
                stack_ref[par, half:2 * half, c0:c0 + LANE] = p[:, LANE:].astype(BF16)

    scale = 1.0 / math.sqrt(2 * half * (D_FOURIER // FOURIER_GROUPS))
    ev = _dot(te_ref[...], stack_ref[0])
    od = _dot(to_ref[...], stack_ref[1])
    o_ref[0] = ((ev + od) * scale).astype(o_ref.dtype)
    o_ref[1] = ((ev - od) * scale).astype(o_ref.dtype)


def _fourier(p4, cg, te, to, *, tl):
    nb, _, half, _ = p4.shape
    out = pl.pallas_call(
        functools.partial(_fourier_kernel, half=half),
        grid=(nb, half // tl),
        in_specs=[
            pl.BlockSpec((None, 2, half, D_MIX), lambda b, j: (b, 0, 0, 0)),
            _const_spec(cg.shape),
            pl.BlockSpec((tl, 2 * half), lambda b, j: (j, 0)),
            pl.BlockSpec((tl, 2 * half), lambda b, j: (j, 0)),
        ],
        out_specs=pl.BlockSpec((None, 2, tl, D_MIX), lambda b, j: (b, 0, j, 0)),
        out_shape=jax.ShapeDtypeStruct((nb, 2, half, D_MIX), BF16),
        scratch_shapes=[pltpu.VMEM((2, 2 * half, D_MIX), BF16)],
        compiler_params=_params("parallel", "arbitrary"),
        name="fourier_mix",
    )(p4, cg, te, to)
    return out.reshape(nb * 2 * half, D_MIX)


def _hyena_filter_kernel(z_ref, w1_ref, b1_ref, w2_ref, b2_ref, w3f_ref, w3b_ref, freq_ref, delta_ref,
                         fwd_ref, bwd_ref, hdn_ref):
    hi = lax.Precision.HIGHEST
    z = z_ref[...]

    @pl.when(pl.program_id(0) == 0)
    def _():
        freq = freq_ref[...]
        h1 = jnp.sin(freq * (jnp.dot(z, w1_ref[...], precision=hi, preferred_element_type=F32) + b1_ref[...]))
        hdn_ref[...] = jnp.sin(freq * (jnp.dot(h1, w2_ref[...], precision=hi, preferred_element_type=F32)
                                       + b2_ref[...]))

    hdn = hdn_ref[...]
    decay = jnp.exp(-z[:, 0:1] * delta_ref[...])
    fwd = jnp.dot(hdn, w3f_ref[...], precision=hi, preferred_element_type=F32) * decay
    bwd = jnp.dot(hdn, w3b_ref[...], precision=hi, preferred_element_type=F32) * decay
    row = lax.broadcasted_iota(jnp.int32, bwd.shape, 0)
    bwd = jnp.where(row == 0, 0.0, bwd)
    norm = jnp.sum(jnp.abs(fwd), axis=0, keepdims=True) + jnp.sum(jnp.abs(bwd), axis=0, keepdims=True) + 1e-6
    inv = 1.0 / norm
    fwd_ref[...] = fwd * inv
    bwd_ref[...] = bwd * inv


def _hyena_filters(seq, w1, b1, w2, b2, w3, freq):
    pos = jnp.concatenate([jnp.arange(0, seq, 2), jnp.arange(1, seq, 2)]).astype(F32)[:, None]
    t = pos / (seq - 1)
    bands = (HY_EMB - 1) // 2
    fr = jnp.linspace(1e-4, bands - 1, bands, dtype=F32)[None, :]
    w = 2.0 * math.pi * pos * fr / seq
    z = jnp.concatenate([t, jnp.cos(w), -jnp.sin(w), jnp.zeros((seq, LANE - HY_EMB), F32)], axis=-1)
    pad = LANE - HY_HID
    w1p = jnp.pad(w1, ((0, LANE - HY_EMB), (0, pad)))
    w2p = jnp.pad(w2, ((0, pad), (0, pad)))
    w3p = jnp.pad(w3, ((0, pad), (0, 0)))
    row = lambda v: jnp.pad(v, (0, pad)).reshape(1, LANE)
    delta = jnp.abs(jnp.linspace(math.log(HY_TARGET) / HY_SLOW_DECAY, math.log(HY_TARGET) / HY_FAST_DECAY,
                                 D_HYENA, dtype=F32)).reshape(1, D_HYENA)
    sq = lambda: _const_spec((LANE, LANE))
    vec = lambda: _const_spec((1, LANE))
    n_ord = 2
    return pl.pallas_call(
        _hyena_filter_kernel,
        grid=(n_ord,),
        in_specs=[_const_spec((seq, LANE)), sq(), vec(), sq(), vec(),
                  pl.BlockSpec((LANE, D_MIX), lambda o: (0, o)),
                  pl.BlockSpec((LANE, D_MIX), lambda o: (0, n_ord + o)),
                  vec(), _const_spec((1, D_MIX))],
        out_specs=[pl.BlockSpec((seq, D_MIX), lambda o: (0, o))] * 2,
        out_shape=[jax.ShapeDtypeStruct((seq, n_ord * D_MIX), F32)] * 2,
        scratch_shapes=[pltpu.VMEM((seq, LANE), F32)],
        compiler_params=_params("arbitrary"),
        name="hyena_filters",
    )(z, w1p, row(b1), w2p, row(b2), w3p, w3p, row(freq), delta)


def _alt_sum(x):
    row = lax.broadcasted_iota(jnp.int32, x.shape, 0)
    return jnp.sum(jnp.where((row & 1) == 1, -x, x), axis=0, keepdims=True)


def _alt_rows(v, shape):
    row = lax.broadcasted_iota(jnp.int32, shape, 0)
    return jnp.where((row & 1) == 1, -v, v)


def _half_spectrum(xe, xo, ce, se, co, so):
    ac, bc = _dot(ce, xe), _dot(co, xo)
    a_s, b_s = _dot(se, xe), _dot(so, xo)
    return ac + bc, a_s + b_s, ac - bc, b_s - a_s


def _hyena_spec_kernel(ce_ref, se_ref, co_ref, so_ref, fwd_ref, bwd_ref, alo_ref, blo_ref, ahi_ref, bhi_ref, mid_ref,
                       *, half):
    n = 4 * half
    tabs = (ce_ref[...], se_ref[...], co_ref[...], so_ref[...])
    fwd, bwd = fwd_ref[...], bwd_ref[...]
    fe, fo, be, bo = fwd[:half], fwd[half:], bwd[:half], bwd[half:]
    ce, se, co, so = tabs
    se_, so_ = (fe + be).astype(BF16), (fo + bo).astype(BF16)
    de_, do_ = (be - fe).astype(BF16), (bo - fo).astype(BF16)
    ac, bc = _dot(ce, se_), _dot(co, so_)
    a_s, b_s = _dot(se, de_), _dot(so, do_)
    row = lax.broadcasted_iota(jnp.int32, ac.shape, 0)
    wgt = jnp.where(row == 0, 1.0 / n, 2.0 / n)
    alo_ref[...] = wgt * (ac + bc)
    blo_ref[...] = wgt * (a_s + b_s)
    ahi_ref[...] = wgt * (ac - bc)
    bhi_ref[...] = wgt * (b_s - a_s)
    mid_ref[0:1, :] = (2.0 / n) * (_alt_sum(fe) + _alt_sum(be))
    mid_ref[1:2, :] = (2.0 / n) * (_alt_sum(bo) - _alt_sum(fo))


def _hyena_spectra(tabs, fwd, bwd, *, cb=256):
    seq, width = fwd.shape
    half = seq // 2
    blk = lambda rows: pl.BlockSpec((rows, cb), lambda j: (0, j))
    return pl.pallas_call(
        functools.partial(_hyena_spec_kernel, half=half),
        grid=(width // cb,),
        in_specs=[_const_spec((half, half))] * 4 + [blk(seq), blk(seq)],
        out_specs=[blk(half)] * 4 + [blk(2)],
        out_shape=[jax.ShapeDtypeStruct((half, width), F32)] * 4 + [jax.ShapeDtypeStruct((2, width), F32)],
        compiler_params=_params("parallel"),
        name="hyena_spectra",
    )(*tabs[:4], fwd, bwd)


def _short_conv(e, o, w, b):
    half = e.shape[0]
    row = lax.broadcasted_iota(jnp.int32, e.shape, 0)
    o_prev = jnp.where(row == 0, 0.0, pltpu.roll(o, 1, 0))
    e_next = jnp.where(row == half - 1, 0.0, pltpu.roll(e, half - 1, 0))
    w0, w1, w2 = w[0:1, :], w[1:2, :], w[2:3, :]
    return o_prev * w0 + e * w1 + o * w2 + b, e * w0 + o * w1 + e_next * w2 + b


HYENA_ROW_SPLIT = 2


def _hyena_order_kernel(z_ref, x_ref, ce_ref, se_ref, co_ref, so_ref, cot_ref, sot_ref, alo_ref, blo_ref, ahi_ref,
                        bhi_ref, mid_ref, skip_ref, wx_ref, bx_ref, *rest, order, first):
    if first:
        wz_ref, bz0_ref, o_ref = rest
        ze, zo = _short_conv(z_ref[0], z_ref[1], wz_ref[...], bz0_ref[...])
    else:
        (o_ref,) = rest
        ze, zo = z_ref[0], z_ref[1]
    half = ze.shape[0]
    groups = [slice(r, r + half // HYENA_ROW_SPLIT) for r in range(0, half, half // HYENA_ROW_SPLIT)]
    zeb, zob = ze.astype(BF16), zo.astype(BF16)
    prods = []
    for rs in groups:
        re_lo, im_lo, re_hi, im_hi = _half_spectrum(zeb, zob, ce_ref[rs, :], se_ref[rs, :], co_ref[rs, :],
                                                    so_ref[rs, :])
        alo, blo, ahi, bhi = alo_ref[rs, :], blo_ref[rs, :], ahi_ref[rs, :], bhi_ref[rs, :]
        wr_lo, wi_lo = re_lo * alo + im_lo * blo, re_lo * blo - im_lo * alo
        wr_hi, wi_hi = re_hi * ahi + im_hi * bhi, re_hi * bhi - im_hi * ahi
        prods.append(((wr_lo + wr_hi).astype(BF16), (wi_lo - wi_hi).astype(BF16),
                      (wr_lo - wr_hi).astype(BF16), (wi_lo + wi_hi).astype(BF16)))
    p_even_c, p_even_s, p_odd_c, p_odd_s = (jnp.concatenate(p, axis=0) for p in zip(*prods))
    me, mo = _short_conv(x_ref[0], x_ref[1], wx_ref[...], bx_ref[...])
    re_mid, im_mid = _alt_sum(ze), _alt_sum(zo)
    a_mid, b_mid = mid_ref[0:1, :], mid_ref[1:2, :]
    mid_e, mid_o = re_mid * a_mid + im_mid * b_mid, re_mid * b_mid - im_mid * a_mid
    skip = skip_ref[order:order + 1, :]
    for rs in groups:
        conv_e = _dot(ce_ref[rs, :], p_even_c) - _dot(se_ref[rs, :], p_even_s)
        conv_o = _dot(cot_ref[rs, :], p_odd_c) - _dot(sot_ref[rs, :], p_odd_s)
        conv_e = conv_e + _alt_rows(mid_e, conv_e.shape)
        conv_o = conv_o - _alt_rows(mid_o, conv_o.shape)
        o_ref[0, rs, :] = (me[rs] * (conv_e + ze[rs] * skip)).astype(o_ref.dtype)
        o_ref[1, rs, :] = (mo[rs] * (conv_o + zo[rs] * skip)).astype(o_ref.dtype)


def _hyena_order(z4, z_blk0, x4, x_blk0, tabs, coefs, skip, conv_w, conv_b, *, order, first, out_dtype, cb=256):
    nb, _, half, _ = x4.shape
    ncb = D_MIX // cb
    xw0 = (1 + order) * ncb
    conv_b = conv_b.reshape(1, -1)
    tok = lambda blk0: pl.BlockSpec((None, 2, half, cb), lambda j, b: (b, 0, 0, blk0 + j))
    colblk = lambda rows, blk0: pl.BlockSpec((rows, cb), lambda j, b: (0, blk0 + j), pipeline_mode=pl.Buffered(1))
    in_specs = [tok(z_blk0), tok(x_blk0)] + [_const_spec((half, half))] * 6
    in_specs += [colblk(half, order * ncb)] * 4 + [colblk(2, order * ncb), colblk(2, 0), colblk(3, xw0), colblk(1, xw0)]
    args = [z4, x4, *tabs, *coefs, skip, conv_w, conv_b]
    if first:
        in_specs += [colblk(3, 0), colblk(1, 0)]
        args += [conv_w, conv_b]
    return pl.pallas_call(
        functools.partial(_hyena_order_kernel, order=order, first=first),
        grid=(ncb, nb),
        in_specs=in_specs,
        out_specs=pl.BlockSpec((None, 2, half, cb), lambda j, b: (b, 0, 0, j)),
        out_shape=jax.ShapeDtypeStruct((nb, 2, half, D_MIX), out_dtype),
        compiler_params=_params("parallel", "parallel"),
        name="hyena_order",
    )(*args)


def _dit_tables(half, period):
    split = math.gcd(half, 64)

    def part(rows, mult, col_off):
        r = lax.broadcasted_iota(jnp.int32, (rows, half), 0) * mult
        c = 2 * lax.broadcasted_iota(jnp.int32, (rows, half), 1) + col_off
        ang = ((r * c) % period).astype(F32) * (2.0 * math.pi / period)
        return jnp.cos(ang), jnp.sin(ang)

    out = []
    for col_off in (0, 1):
        ca, sa = (t[:, None, :] for t in part(half // split, split, col_off))
        cb, sb = (t[None, :, :] for t in part(split, 1, col_off))
        out += [(ca * cb - sa * sb).reshape(half, half), (sa * cb + ca * sb).reshape(half, half)]
    return out


def _fourier_tables(seq):
    ce, se, co, so = _dit_tables(seq // 2, seq)
    te = jnp.concatenate([ce, -se], axis=1).astype(BF16)
    to = jnp.concatenate([co, -so], axis=1).astype(BF16)
    gsz = D_FOURIER // FOURIER_GROUPS
    r = lax.broadcasted_iota(jnp.int32, (gsz, gsz), 0)
    c = lax.broadcasted_iota(jnp.int32, (gsz, gsz), 1)
    ang = ((r * c) % gsz).astype(F32) * (2.0 * math.pi / gsz)
    eye = jnp.eye(LANE // gsz, dtype=F32)
    cg = jnp.concatenate([jnp.kron(eye, jnp.cos(ang)), jnp.kron(eye, jnp.sin(ang))], axis=1).astype(BF16)
    return te, to, cg


def _hyena_tables(seq):
    ce, se, co, so = (t.astype(BF16) for t in _dit_tables(seq // 2, 2 * seq))
    return ce, se, co, so, co.T, so.T


def _rms_heads(x, g):
    parts = []
    for h in range(D_MIX // LANE):
        xh = x[:, h * LANE:(h + 1) * LANE]
        parts.append(xh * lax.rsqrt(jnp.mean(xh * xh, axis=-1, keepdims=True) + LN_EPS) * g)
    return jnp.concatenate(parts, axis=-1)


def _outproj_kernel(x_ref, ya_ref, yb_ref, *rest, tm, seq, is_ctx, readout):
    if readout:
        r_ref, gd_ref, gg_ref, hg_ref, unperm_ref, mod_ref, w_ref, g_ref, b_ref, o_ref = rest
        r = r_ref[...]
        ya = (_rms_heads(ya_ref[...], gg_ref[...]) * (r * _sigmoid(r))).astype(BF16)
        yb = _rms_heads(yb_ref[...].reshape(tm, D_MIX) * _sigmoid(gd_ref[...].reshape(tm, D_MIX)), hg_ref[...])
        yb = _dot(unperm_ref[...], yb.astype(BF16)).astype(BF16)
    else:
        unperm_ref, mod_ref, w_ref, g_ref, b_ref, o_ref = rest
        ya = ya_ref[...]
        half = DEINT_BLOCK // 2
        parts = []
        for r in range(0, tm // 2, half):
            blk = jnp.concatenate([yb_ref[0, r:r + half, :], yb_ref[1, r:r + half, :]], axis=0)
            parts.append(_dot(unperm_ref[...], blk).astype(BF16))
        yb = parts[0] if len(parts) == 1 else jnp.concatenate(parts, axis=0)
    row = _mod_row_index(pl.program_id(0), tm, seq, is_ctx)
    gate = _mod_row(mod_ref, row, 5)
    y = _dot(ya, w_ref[:D_MIX, :]) + _dot(yb, w_ref[D_MIX:, :])
    z = ALPHA * x_ref[...] + gate * y
    o_ref[...] = _ln(z) * g_ref[...] + b_ref[...]


def _outproj(x2, ya, yb, mod, layer, w, ln_g, ln_b, *, seq, is_ctx, readout=None):
    n = x2.shape[0]
    tm = _token_tile(n, seq, False, want=TOKEN_TILE if readout is None else TOKEN_TILE // 2)
    tiles = seq // tm
    tok = lambda: pl.BlockSpec((tm, D_MIX), lambda i: (i, 0))
    in_specs = [pl.BlockSpec((tm, D_MODEL), lambda i: (i, 0)), tok()]
    args = [x2, ya, yb]
    if readout is None:
        assert tm % DEINT_BLOCK == 0
        unperm = _deint_perm(inverse=True)
        in_specs += [pl.BlockSpec((None, 2, tm // 2, D_MIX), lambda i: (i // tiles, 0, i % tiles, 0)),
                     _const_spec(unperm.shape)]
        args += [unperm]
    else:
        p_main, p_hg, gla_g, hg_g = readout
        rows = tm // GRID_W
        assert tm % GRID_W == 0 and rows % 8 == 0
        i = lax.broadcasted_iota(jnp.int32, (tm, tm), 0)
        j = lax.broadcasted_iota(jnp.int32, (tm, tm), 1)
        unperm = jnp.where(j == (i % GRID_W) * rows + i // GRID_W, 1.0, 0.0).astype(BF16)
        colmajor = lambda blk: pl.BlockSpec((None, GRID_W, rows, D_MIX), lambda i: (i // tiles, 0, i % tiles, blk))
        in_specs += [colmajor(0), pl.BlockSpec((tm, D_MIX), lambda i: (i, PM_R)), colmajor(PH_GD),
                     _const_spec((1, LANE)), _const_spec((1, LANE)), _const_spec((tm, tm))]
        args += [p_main, p_hg, gla_g.reshape(1, LANE), hg_g.reshape(1, LANE), unperm]
    in_specs += [_picked_spec(mod.shape, (layer,)), _const_spec(w.shape),
                 _picked_spec(ln_g.shape, (layer, 1)), _picked_spec(ln_b.shape, (layer, 1))]
    args += [mod, w, ln_g, ln_b]
    kern = functools.partial(_outproj_kernel, tm=tm, seq=seq, is_ctx=is_ctx, readout=readout is not None)
    return pl.pallas_call(
        kern,
        grid=(n // tm,),
        in_specs=in_specs,
        out_specs=pl.BlockSpec((tm, D_MODEL), lambda i: (i, 0)),
        out_shape=jax.ShapeDtypeStruct((n, D_MODEL), F32),
        compiler_params=_params("parallel"),
        name="mixer_out",
    )(*args)


def _split3(x):
    hi = x.astype(BF16)
    r1 = x - hi.astype(F32)
    mid = r1.astype(BF16)
    lo = (r1 - mid.astype(F32)).astype(BF16)
    return hi, mid, lo


GLA_UNROLL = 16
HGRN_UNROLL = 8
CUMSUM_CHUNKS = 4


def _scan_unroll(n_chunks, want):
    return math.gcd(want, n_chunks)


def _scan_consts(rev, unroll, heads, dk):
    n = math.gcd(unroll, CUMSUM_CHUNKS) * CHUNK
    i = lax.broadcasted_iota(jnp.int32, (n, n), 0)
    j = lax.broadcasted_iota(jnp.int32, (n, n), 1)
    same_chunk = (i // CHUNK) == (j // CHUNK)
    tri = jnp.where(same_chunk & ((j >= i) if rev else (j <= i)), 1.0, 0.0).astype(BF16)
    r = lax.broadcasted_iota(jnp.int32, (CHUNK, heads * CHUNK), 0)
    c = lax.broadcasted_iota(jnp.int32, (CHUNK, heads * CHUNK), 1) & (CHUNK - 1)
    causal = (c >= r) if rev else (c <= r)
    klane_head = lax.broadcasted_iota(jnp.int32, (1, heads * dk), 1) // dk
    return tri, causal, klane_head


def _block_diag(blocks):
    z = jnp.zeros_like(blocks[0])
    return jnp.concatenate([jnp.concatenate([b if c == h else z for c in range(len(blocks))], axis=1)
                            for h, b in enumerate(blocks)], axis=0)


def _scan_local(q, k, g, vb, consts, *, rev, heads, dk, need_out):
    tri, causal, klane_head = consts
    kt = heads * dk
    parts = _split3(g)
    span = tri.shape[0]
    gsum = jnp.concatenate([sum(_dot(tri, part[r:r + span]) for part in parts) for r in range(0, g.shape[0], span)],
                           axis=0)
    staged = []
    for u in range(g.shape[0] // CHUNK):
        rows = slice(u * CHUNK, (u + 1) * CHUNK)
        gs = gsum[rows]
        gt = gs[0:1] if rev else gs[CHUNK - 1:CHUNK]
        ku, vu = k[rows], vb[rows]
        q_in = o_intra = None
        if need_out:
            q_in = (q[rows] * jnp.exp(gs)).astype(BF16)
            k_dec = ku * jnp.exp(-gs)
            k_in = k_dec.astype(BF16)
            k_out = (k_dec * jnp.exp(gt)).astype(BF16)
            if dk % LANE == 0:
                k4 = _block_diag([k_in[:, h * dk:(h + 1) * dk] for h in range(heads)])
            else:
                k4 = jnp.concatenate([jnp.where(klane_head == h, k_in, jnp.zeros_like(k_in)) for h in range(heads)],
                                     axis=0)
            att = jnp.where(causal, _dot_nt(q_in, k4), 0.0).astype(BF16)
            o_intra = _dot(att, _block_diag([vu[:, h * LANE:(h + 1) * LANE] for h in range(heads)]))
        else:
            k_out = (ku * jnp.exp(gt - gs)).astype(BF16)
        if dk % LANE == 0:
            inc = [_dot_tn(k_out[:, h * dk:(h + 1) * dk], vu[:, h * LANE:(h + 1) * LANE]) for h in range(heads)]
        else:
            k_t = jnp.transpose(jnp.concatenate([k_out.astype(F32), jnp.zeros((LANE - CHUNK, kt), F32)], axis=0))
            k_t = k_t.astype(BF16)
            v_pad = jnp.concatenate([vu, jnp.zeros((LANE - CHUNK, heads * LANE), BF16)], axis=0)
            inc = [_dot(k_t[h * dk:(h + 1) * dk, :], v_pad[:, h * LANE:(h + 1) * LANE]) for h in range(heads)]
        dec = jnp.transpose(jnp.broadcast_to(jnp.exp(gt), (LANE, kt)))
        staged.append((q_in, o_intra, inc, [dec[h * dk:(h + 1) * dk] for h in range(heads)]))
    return staged


def _scan_states(s_ref, base, staged):
    heads = len(staged[0][2])
    state = [s_ref[base + h] for h in range(heads)]
    outs = []
    for q_in, o_intra, inc, decay in staged:
        if q_in is not None:
            outs.append(o_intra + _dot(q_in, _block_diag([s.astype(BF16) for s in state])))
        state = [s * d + i for s, d, i in zip(state, decay, inc)]
    for h in range(heads):
        s_ref[base + h] = state[h]
    return outs


def _gla_kernel(qk_ref, v_ref, g_ref, cqk_ref, cv_ref, cg_ref, o_ref, s_ref, ob_ref, *, seq, ctx_len):
    heads, dk = GLA_HEADS, GLA_DK
    kt = heads * dk
    s_ref[...] = jnp.zeros_like(s_ref)

    def make_run(qk, vv, gg, n_chunks, need_out):
        unroll = _scan_unroll(n_chunks, GLA_UNROLL)
        rows = unroll * CHUNK
        consts = [_scan_consts(rev, unroll, heads, dk) for rev in (False, True)]

        def run(n):
            work = []
            for d, rev in enumerate((False, True)):
                start = (n_chunks * CHUNK - rows - n * rows) if rev else n * rows
                sl = pl.ds(pl.multiple_of(start, rows), rows)
                q = qk[sl, 0:kt] * (dk ** -0.5) if need_out else None
                staged = _scan_local(q, qk[sl, kt:2 * kt], gg[sl, d * kt:(d + 1) * kt], vv[sl, :].astype(BF16),
                                     consts[d], rev=rev, heads=heads, dk=dk, need_out=need_out)
                work.append((d, rev, sl, staged[::-1] if rev else staged))
            for d, rev, sl, staged in work:
                outs = _scan_states(s_ref, d * heads, staged)
                if need_out:
                    (ob_ref if rev else o_ref)[sl, :] = jnp.concatenate(outs[::-1] if rev else outs, axis=0)

        return run, n_chunks // unroll

    run, steps = make_run(cqk_ref, cv_ref, cg_ref, ctx_len // CHUNK, False)
    for n in range(steps):
        run(n)
    run, steps = make_run(qk_ref, v_ref, g_ref, seq // CHUNK, True)

    def body(n, carry):
        run(n)
        return carry

    lax.fori_loop(0, steps, body, 0)
    o_ref[...] = o_ref[...] + ob_ref[...]


def _gla(p3, pc3):
    nb, seq, _ = p3.shape
    ctx_len = pc3.shape[1]
    blk = lambda length, j: pl.BlockSpec((None, length, D_MIX), lambda b: (b, 0, j))
    return pl.pallas_call(
        functools.partial(_gla_kernel, seq=seq, ctx_len=ctx_len),
        grid=(nb,),
        in_specs=[blk(seq, PM_QK), blk(seq, PM_V), blk(seq, PM_GATE),
                  blk(ctx_len, P_QK), blk(ctx_len, P_V), blk(ctx_len, P_GATE)],
        out_specs=pl.BlockSpec((None, seq, D_MIX), lambda b: (b, 0, 0)),
        out_shape=jax.ShapeDtypeStruct((nb, seq, D_MIX), F32),
        scratch_shapes=[pltpu.VMEM((2 * GLA_HEADS, GLA_DK, GLA_DV), F32),
                        pltpu.VMEM((seq, D_MIX), F32)],
        compiler_params=_params("parallel"),
        name="gla_scan",
    )(p3, p3, p3, pc3, pc3, pc3)


def _hgrn_kernel(q_ref, ff_ref, fb_ref, v_ref, cff_ref, cfb_ref, cv_ref, o_ref, s_ref, *, seq, ctx_len):
    heads, dk = HG_HEADS, HG_D
    s_ref[...] = jnp.zeros_like(s_ref)

    def make_run(qq, gates, vv, n_chunks, need_out):
        unroll = _scan_unroll(n_chunks, HGRN_UNROLL)
        rows = unroll * CHUNK
        consts = [_scan_consts(rev, unroll, heads, dk) for rev in (False, True)]

        def run(n):
            work = []
            for d, rev in enumerate((False, True)):
                start = (n_chunks * CHUNK - rows - n * rows) if rev else n * rows
                sl = pl.ds(pl.multiple_of(start, rows), rows)
                f = gates[d][sl, :]
                staged = _scan_local(qq[sl, :] if need_out else None, 1.0 - f, jnp.log(f), vv[sl, :].astype(BF16), consts[d],
                                     rev=rev, heads=heads, dk=dk, need_out=need_out)
                work.append((d, rev, sl, staged[::-1] if rev else staged))
            for d, rev, sl, staged in work:
                outs = _scan_states(s_ref, d * heads, staged)
                if need_out:
                    o_ref[sl, :] += jnp.concatenate(outs[::-1] if rev else outs, axis=0)

        return run, n_chunks // unroll

    run, steps = make_run(None, (cff_ref, cfb_ref), cv_ref, ctx_len // CHUNK, False)
    for n in range(steps):
        run(n)
    o_ref[...] = jnp.zeros_like(o_ref)
    run, steps = make_run(q_ref, (ff_ref, fb_ref), v_ref, seq // CHUNK, True)

    def body(n, carry):
        run(n)
        return carry

    lax.fori_loop(0, steps, body, 0)


def _hgrn(ph3, pc3):
    nb, seq, _ = ph3.shape
    ctx_len = pc3.shape[1]
    blk = lambda length, j: pl.BlockSpec((None, length, D_MIX), lambda b: (b, 0, j))
    return pl.pallas_call(
        functools.partial(_hgrn_kernel, seq=seq, ctx_len=ctx_len),
        grid=(nb,),
        in_specs=[blk(seq, PH_QD), blk(seq, PH_FFL), blk(seq, PH_FBL), blk(seq, PH_IDD),
                  blk(ctx_len, P_FFL), blk(ctx_len, P_FBL), blk(ctx_len, P_IDD)],
        out_specs=pl.BlockSpec((None, seq, D_MIX), lambda b: (b, 0, 0)),
        out_shape=jax.ShapeDtypeStruct((nb, seq, D_MIX), F32),
        scratch_shapes=[pltpu.VMEM((2 * HG_HEADS, HG_D, HG_D), F32)],
        compiler_params=_params("parallel"),
        name="hgrn_scan",
    )(ph3, ph3, ph3, ph3, pc3, pc3, pc3)


def _even_mixer(x2, nb, seq, mod, layer, is_ctx, w_in, w_out, conv_w, conv_b, filt, skip, ln_g, ln_b):
    p4 = _proj_deint(x2, nb, seq, mod, layer, w_in, is_ctx=is_ctx)
    te, to, cg = _fourier_tables(seq)
    ya = _fourier(p4, cg, te, to, tl=min(seq // 2, 512))
    tabs = _hyena_tables(seq)
    fwd, bwd = _hyena_filters(seq, *filt)
    coefs = _hyena_spectra(tabs, fwd, bwd)
    cb = 256
    u0 = D_FOURIER // cb
    per = D_HYENA // cb
    y1 = _hyena_order(p4, u0, p4, u0 + per, tabs, coefs, skip, conv_w, conv_b, order=0, first=True, out_dtype=F32,
                      cb=cb)
    y2 = _hyena_order(y1, 0, p4, u0 + 2 * per, tabs, coefs, skip, conv_w, conv_b, order=1, first=False,
                      out_dtype=BF16, cb=cb)
    return _outproj(x2, ya, y2, mod, layer, w_out, ln_g, ln_b, seq=seq, is_ctx=is_ctx)


def _odd_proj_weights(w_in, a_up, a_b):
    kq = GLA_HEADS * GLA_DK
    low0 = 2 * kq + 2 * D_MIX
    low1 = low0 + 2 * GLA_RANK
    w = jnp.concatenate([w_in[:, :low0], w_in[:, low1:], w_in[:, low0:low1],
                         jnp.zeros((D_MODEL, LANE - 2 * GLA_RANK), w_in.dtype)], axis=1).astype(BF16)
    up = jnp.zeros((LANE, 2 * kq), F32)
    up = up.at[0:GLA_RANK, 0:kq].set(a_up[0]).at[GLA_RANK:2 * GLA_RANK, kq:].set(a_up[1]).astype(BF16)
    return w, up, a_b.reshape(1, 2 * kq)


def kernel(x, c, ctx, c_ctx, mod_w, mod_b, ffn_w_in, ffn_w_out, ln_g, ln_b, ev_w_in, ev_w_out, hy_conv_w, hy_conv_b, hy_w1, hy_b1, hy_w2, hy_b2, hy_w3, hy_freq, hy_skip, od_w_in, od_w_out, gla_a_up, gla_a_b, gla_norm_g, hg_lb, hg_norm_g):
    nb, seq, d = x.shape
    ctx_len = ctx.shape[1]
    assert d == D_MODEL and seq % 512 == 0 and ctx_len % CHUNK == 0 and nb + 1 <= MOD_ROWS - 7
    assert mod_w.shape[0] == DEPTH == 2

    c_all = jnp.zeros((MOD_ROWS, D_MODEL), F32).at[:nb].set(c).at[MOD_ROWS - 8].set(c_ctx)
    mod = _mod_table(c_all, mod_w, mod_b)
    w_in_b = ffn_w_in.astype(BF16)
    w_out_b = ffn_w_out.astype(BF16)

    xs = x.reshape(nb * seq, D_MODEL)
    xc = ctx.reshape(nb * ctx_len, D_MODEL)
    streams = [(xs, seq, False), (xc, ctx_len, True)]

    ln_g4 = ln_g.reshape(DEPTH, 3, 1, D_MODEL)
    ln_b4 = ln_b.reshape(DEPTH, 3, 1, D_MODEL)

    def ffn(stream, layer, half):
        arr, length, is_ctx = stream
        return (_ffn(arr, mod, layer, half, w_in_b, w_out_b, ln_g4, ln_b4, seq=length, is_ctx=is_ctx), length, is_ctx)

    streams = [ffn(s, 0, 0) for s in streams]
    filt = (hy_w1[0], hy_b1[0], hy_w2[0], hy_b2[0], hy_w3[0], hy_freq[0])
    ev_in_b, ev_out_b = ev_w_in[0].astype(BF16), ev_w_out[0].astype(BF16)
    streams = [(_even_mixer(arr, nb, length, mod, 0, is_ctx, ev_in_b, ev_out_b, hy_conv_w[0], hy_conv_b[0], filt,
                            hy_skip[0], ln_g4, ln_b4), length, is_ctx) for arr, length, is_ctx in streams]
    streams = [ffn(s, 0, 1) for s in streams]

    streams = [ffn(s, 1, 0) for s in streams]
    (xs, _, _), (xc, _, _) = streams
    w_proj, up, up_b = _odd_proj_weights(od_w_in[0], gla_a_up[0], gla_a_b[0])
    n_gla, n_hg = (P_R + 1) * D_MIX, (PH_GD + 1) * D_MIX
    w_main = jnp.concatenate([w_proj[:, :n_gla], w_proj[:, n_gla + n_hg:]], axis=1)
    pc = _proj(xc, mod, 1, w_proj, seq=ctx_len, is_ctx=True, gates=(up, up_b), forget=(hg_lb, P_FFL * D_MIX),
               want=TOKEN_TILE // 2)
    pm = _proj(xs, mod, 1, w_main, seq=seq, is_ctx=False, gates=(up, up_b))
    ph = _proj_colmajor(xs, nb, seq, mod, 1, w_proj[:, n_gla:n_gla + n_hg], hg_lb)
    pc3 = pc.reshape(nb, ctx_len, -1)
    o_gla = _gla(pm.reshape(nb, seq, -1), pc3).reshape(nb * seq, D_MIX)
    o_hg = _hgrn(ph.reshape(nb, seq, -1), pc3).reshape(ph.shape[:3] + (D_MIX,))
    xs = _outproj(xs, o_gla, o_hg, mod, 1, od_w_out[0].astype(BF16), ln_g4, ln_b4, seq=seq, is_ctx=False,
                  readout=(pm, ph, gla_norm_g[0], hg_norm_g[0]))
    xs, _, _ = ffn((xs, seq, False), 1, 1)
    return xs.reshape(nb, seq, D_MODEL)
```

```python
import functools
import math

import jax
import jax.numpy as jnp
from jax import lax
from jax.experimental import pallas as pl
from jax.experimental.pallas import tpu as pltpu

F32 = jnp.float32
BF16 = jnp.bfloat16

D_MODEL = 1024
DEPTH = 2
N_MOD = 9
D_FF = 2816
LN_EPS = 1e-6
ALPHA = (2.0 * DEPTH) ** 0.25
GRID_W = 64
D_FOURIER = 512
FOURIER_GROUPS = 8
D_HYENA = 512
HY_EMB = 33
HY_HID = 64
HY_FAST_DECAY = 0.3
HY_SLOW_DECAY = 1.5
HY_TARGET = 1e-2
CHUNK = 64
GLA_HEADS = 4
GLA_DK = 64
GLA_DV = 128
GLA_RANK = 16
GLA_TAU = 16.0
HG_HEADS = 4
HG_D = 128
D_MIX = 512

LANE = 128
MOD_ROWS = 24
VMEM_LIMIT = 56 * 1024 * 1024

P_QK, P_V, P_R, P_QD, P_FFL, P_FBL, P_IDD, P_GD, P_GATE = range(9)
PM_QK, PM_V, PM_R, PM_GATE = range(4)
PH_QD, PH_FFL, PH_FBL, PH_IDD, PH_GD = range(5)


def _params(*sem):
    return pltpu.CompilerParams(dimension_semantics=sem, vmem_limit_bytes=VMEM_LIMIT)


def _dot(a, b):
    return jnp.dot(a, b, preferred_element_type=F32)


def _dot_nt(a, b):
    return lax.dot_general(a, b, (((1,), (1,)), ((), ())), preferred_element_type=F32)


def _dot_tn(a, b):
    return lax.dot_general(a, b, (((0,), (0,)), ((), ())), preferred_element_type=F32)


def _ln(x):
    mu = jnp.mean(x, axis=-1, keepdims=True)
    xc = x - mu
    var = jnp.mean(xc * xc, axis=-1, keepdims=True)
    return xc * lax.rsqrt(var + LN_EPS)


def _sigmoid(x):
    return 1.0 / (1.0 + jnp.exp(-x))


def _mod_row(mod_ref, row, k):
    return mod_ref[pl.ds(row, 1), k * D_MODEL:(k + 1) * D_MODEL]


def _mod_row_index(tile, tm, seq, is_ctx):
    if is_ctx:
        return MOD_ROWS - 8
    return (tile * tm) // seq


def _const_spec(shape):
    return pl.BlockSpec(shape, lambda *_: (0,) * len(shape), pipeline_mode=pl.Buffered(1))


def _picked_spec(shape, lead):
    tail = tuple(shape[len(lead):])
    return pl.BlockSpec((None,) * len(lead) + tail, lambda *_: tuple(lead) + (0,) * len(tail),
                        pipeline_mode=pl.Buffered(1))


TOKEN_TILE = 1024
SUB_TILES = 2


def _token_tile(n, seq, is_ctx, want=TOKEN_TILE):
    tm = min(want, n if is_ctx else seq)
    while n % tm or (not is_ctx and seq % tm):
        tm //= 2
    return tm


def _mod_kernel(c_ref, w_ref, b_ref, o_ref):
    c = c_ref[...]
    s = (c * _sigmoid(c)).astype(BF16)
    o_ref[...] = _dot(s, w_ref[...].astype(BF16)) + b_ref[...]


def _mod_table(c_all, mod_w, mod_b):
    return pl.pallas_call(
        _mod_kernel,
        grid=(DEPTH, N_MOD),
        in_specs=[
            _const_spec((MOD_ROWS, D_MODEL)),
            pl.BlockSpec((None, D_MODEL, D_MODEL), lambda l, j: (l, 0, j)),
            pl.BlockSpec((None, 1, D_MODEL), lambda l, j: (l, 0, j)),
        ],
        out_specs=pl.BlockSpec((None, MOD_ROWS, D_MODEL), lambda l, j: (l, 0, j)),
        out_shape=jax.ShapeDtypeStruct((DEPTH, MOD_ROWS, N_MOD * D_MODEL), F32),
        compiler_params=_params("parallel", "parallel"),
        name="mod_table",
    )(c_all, mod_w, mod_b.reshape(DEPTH, 1, N_MOD * D_MODEL))


FFN_CHUNK = 256


def _swiglu(xm, win_ref, wout_ref):
    acc = None
    for j in range(D_FF // FFN_CHUNK):
        lo = j * FFN_CHUNK
        a = _dot(xm, win_ref[:, lo:lo + FFN_CHUNK])
        u = _dot(xm, win_ref[:, D_FF + lo:D_FF + lo + FFN_CHUNK])
        h = (a * _sigmoid(a) * u).astype(BF16)
        y = _dot(h, wout_ref[lo:lo + FFN_CHUNK, :])
        acc = y if acc is None else acc + y
    return acc


def _ffn_kernel(x_ref, mod_ref, win_ref, wout_ref, g_ref, b_ref, o_ref, *, k0, tm, seq, is_ctx):
    row = _mod_row_index(pl.program_id(0), tm, seq, is_ctx)
    shift, scale, gate = (_mod_row(mod_ref, row, k0 + t) for t in range(3))
    subs = [slice(r, r + tm // SUB_TILES) for r in range(0, tm, tm // SUB_TILES)]
    xms = [(_ln(x_ref[s, :]) * (1.0 + scale) + shift).astype(BF16) for s in subs]
    accs = [_swiglu(xm, win_ref, wout_ref) for xm in xms]
    for s, acc in zip(subs, accs):
        z = ALPHA * x_ref[s, :] + (0.5 * gate) * acc
        o_ref[s, :] = _ln(z) * g_ref[...] + b_ref[...]


def _ffn(x2, mod, layer, half, w_in, w_out, ln_g, ln_b, *, seq, is_ctx):
    n = x2.shape[0]
    tm = _token_tile(n, seq, is_ctx)
    kern = functools.partial(_ffn_kernel, k0=6 * half, tm=tm, seq=seq, is_ctx=is_ctx)
    return pl.pallas_call(
        kern,
        grid=(n // tm,),
        in_specs=[
            pl.BlockSpec((tm, D_MODEL), lambda i: (i, 0)),
            _picked_spec(mod.shape, (layer,)),
            _picked_spec(w_in.shape, (layer, half)),
            _picked_spec(w_out.shape, (layer, half)),
            _picked_spec(ln_g.shape, (layer, 2 * half)),
            _picked_spec(ln_b.shape, (layer, 2 * half)),
        ],
        out_specs=pl.BlockSpec((tm, D_MODEL), lambda i: (i, 0)),
        out_shape=jax.ShapeDtypeStruct((n, D_MODEL), F32),
        compiler_params=_params("parallel"),
        name="ffn",
    )(x2, mod, w_in, w_out, ln_g, ln_b)


def _log_sigmoid(x):
    return jnp.minimum(x, 0.0) - jnp.log(1.0 + jnp.exp(-jnp.abs(x)))


def _forget_gates(lb_ref, layer, logits):
    raw = lb_ref[...]
    e = jnp.exp(raw - jnp.max(raw, axis=0, keepdims=True))
    sm = e / jnp.sum(e, axis=0, keepdims=True)
    lb = jnp.sum(sm[0:layer + 1, :], axis=0, keepdims=True) - sm[0:1, :]
    lb = jnp.concatenate([lb, lb], axis=-1)
    return lb + (1.0 - lb) * _sigmoid(logits)


def _proj_kernel(x_ref, mod_ref, w_ref, *rest, tm, seq, is_ctx, gla_gates, forget):
    row = _mod_row_index(pl.program_id(0), tm, seq, is_ctx)
    shift, scale = _mod_row(mod_ref, row, 3), _mod_row(mod_ref, row, 4)
    subs = [slice(r, r + tm // SUB_TILES) for r in range(0, tm, tm // SUB_TILES)]
    xms = [(_ln(x_ref[s, :]) * (1.0 + scale) + shift).astype(BF16) for s in subs]
    ps = [_dot(xm, w_ref[...]) for xm in xms]
    if not gla_gates:
        (o_ref,) = rest
        for s, p in zip(subs, ps):
            o_ref[s, :] = p
        return
    aup_ref, ab_ref, *lb_ref, o_ref = rest
    for s, p in zip(subs, ps):
        main = p.shape[1] - LANE
        o_ref[s, :main] = p[:, :main]
        if forget is not None:
            layer, c0 = forget
            o_ref[s, c0:c0 + 2 * D_MIX] = _forget_gates(lb_ref[0], layer, p[:, c0:c0 + 2 * D_MIX])
        low = p[:, main:].astype(BF16)
        o_ref[s, main:] = _log_sigmoid(_dot(low, aup_ref[...]) + ab_ref[...]) * (1.0 / GLA_TAU)


PERM_COLS = 8


def _proj_colmajor_kernel(x_ref, mod_ref, perm_ref, w_ref, lb_ref, o_ref, *, layer):
    rows, cols, _ = x_ref.shape
    row = pl.program_id(0)
    shift, scale = _mod_row(mod_ref, row, 3), _mod_row(mod_ref, row, 4)
    per_sub = cols // SUB_TILES
    for s in range(SUB_TILES):
        parts = []
        for c in range(s * per_sub, (s + 1) * per_sub, PERM_COLS):
            x = x_ref[:, c:c + PERM_COLS, :].reshape(rows * PERM_COLS, D_MODEL)
            xm = (_ln(x) * (1.0 + scale) + shift).astype(BF16)
            parts.append(_dot(perm_ref[...], xm).astype(BF16))
        p = _dot(jnp.concatenate(parts, axis=0), w_ref[...])
        c0, c1 = PH_FFL * D_MIX, (PH_FBL + 1) * D_MIX
        p = jnp.concatenate([p[:, :c0], _forget_gates(lb_ref, layer, p[:, c0:c1]), p[:, c1:]], axis=-1)
        o_ref[s * per_sub:(s + 1) * per_sub] = p.reshape(per_sub, rows, w_ref.shape[1])


def _proj_colmajor(x2, nb, seq, mod, layer, w, hg_lb, *, cols=32):
    rows = seq // GRID_W
    n_out = w.shape[1]
    n = rows * PERM_COLS
    i = lax.broadcasted_iota(jnp.int32, (n, n), 0)
    j = lax.broadcasted_iota(jnp.int32, (n, n), 1)
    perm = jnp.where(j == (i % rows) * PERM_COLS + i // rows, 1.0, 0.0).astype(BF16)
    return pl.pallas_call(
        functools.partial(_proj_colmajor_kernel, layer=layer),
        grid=(nb, GRID_W // cols),
        in_specs=[pl.BlockSpec((None, rows, cols, D_MODEL), lambda b, c: (b, 0, c, 0)),
                  _picked_spec(mod.shape, (layer,)), _const_spec(perm.shape), _const_spec(w.shape),
                  _const_spec(hg_lb.shape)],
        out_specs=pl.BlockSpec((None, cols, rows, n_out), lambda b, c: (b, c, 0, 0)),
        out_shape=jax.ShapeDtypeStruct((nb, GRID_W, rows, n_out), F32),
        compiler_params=_params("parallel", "parallel"),
        name="mixer_proj_colmajor",
    )(x2.reshape(nb, rows, GRID_W, D_MODEL), mod, perm, w, hg_lb)


def _proj(x2, mod, layer, w, *, seq, is_ctx, gates=None, forget=None, want=TOKEN_TILE):
    n = x2.shape[0]
    tm = _token_tile(n, seq, is_ctx, want)
    n_out = w.shape[1] - LANE + gates[0].shape[1] if gates is not None else w.shape[1]
    kern = functools.partial(_proj_kernel, tm=tm, seq=seq, is_ctx=is_ctx, gla_gates=gates is not None,
                             forget=None if forget is None else (layer, forget[1]))
    extra = [] if gates is None else list(gates)
    if forget is not None:
        extra.append(forget[0])
    return pl.pallas_call(
        kern,
        grid=(n // tm,),
        in_specs=[pl.BlockSpec((tm, D_MODEL), lambda i: (i, 0)), _picked_spec(mod.shape, (layer,)),
                  _const_spec(w.shape)] + [_const_spec(e.shape) for e in extra],
        out_specs=pl.BlockSpec((tm, n_out), lambda i: (i, 0)),
        out_shape=jax.ShapeDtypeStruct((n, n_out), F32),
        compiler_params=_params("parallel"),
        name="mixer_proj",
    )(x2, mod, w, *extra)


DEINT_BLOCK = 256


def _deint_perm(inverse=False):
    i = lax.broadcasted_iota(jnp.int32, (DEINT_BLOCK, DEINT_BLOCK), 0)
    j = lax.broadcasted_iota(jnp.int32, (DEINT_BLOCK, DEINT_BLOCK), 1)
    half = DEINT_BLOCK // 2
    src = 2 * (i % half) + i // half
    hit = (i == 2 * (j % half) + j // half) if inverse else (j == src)
    return jnp.where(hit, 1.0, 0.0).astype(BF16)


def _proj_deint_kernel(x_ref, mod_ref, perm_ref, w_ref, o_ref, *, tm, seq, is_ctx):
    row = _mod_row_index(pl.program_id(0), tm, seq, is_ctx)
    shift, scale = _mod_row(mod_ref, row, 3), _mod_row(mod_ref, row, 4)
    half = DEINT_BLOCK // 2
    n_sub = SUB_TILES if tm % (SUB_TILES * DEINT_BLOCK) == 0 else 1
    sub = tm // n_sub
    for s in range(n_sub):
        parts = []
        for r in range(s * sub, (s + 1) * sub, DEINT_BLOCK):
            xm = (_ln(x_ref[r:r + DEINT_BLOCK, :]) * (1.0 + scale) + shift).astype(BF16)
            parts.append(_dot(perm_ref[...], xm).astype(BF16))
        p = _dot(parts[0] if len(parts) == 1 else jnp.concatenate(parts, axis=0), w_ref[...])
        for k in range(sub // DEINT_BLOCK):
            dst = (s * sub + k * DEINT_BLOCK) // 2
            o_ref[0, dst:dst + half, :] = p[k * DEINT_BLOCK:k * DEINT_BLOCK + half]
            o_ref[1, dst:dst + half, :] = p[k * DEINT_BLOCK + half:(k + 1) * DEINT_BLOCK]


def _proj_deint(x2, nb, seq, mod, layer, w, *, is_ctx):
    n = x2.shape[0]
    tm = _token_tile(n, seq, False)
    assert tm % DEINT_BLOCK == 0
    tiles = seq // tm
    n_out = w.shape[1]
    perm = _deint_perm()
    return pl.pallas_call(
        functools.partial(_proj_deint_kernel, tm=tm, seq=seq, is_ctx=is_ctx),
        grid=(n // tm,),
        in_specs=[pl.BlockSpec((tm, D_MODEL), lambda i: (i, 0)), _picked_spec(mod.shape, (layer,)),
                  _const_spec(perm.shape), _const_spec(w.shape)],
        out_specs=pl.BlockSpec((None, 2, tm // 2, n_out), lambda i: (i // tiles, 0, i % tiles, 0)),
        out_shape=jax.ShapeDtypeStruct((nb, 2, seq // 2, n_out), F32),
        compiler_params=_params("parallel"),
        name="mixer_proj_deint",
    )(x2, mod, perm, w)


def _fourier_kernel(a_ref, cg_ref, te_ref, to_ref, o_ref, stack_ref, *, half):
    @pl.when(pl.program_id(1) == 0)
    def _():
        for par in range(2):
            for c0 in range(0, D_MIX, LANE):
                p = _dot(a_ref[par, :, c0:c0 + LANE].astype(BF16), cg_ref[...])
                stack_ref[par, 0:half, c0:c0 + LANE] = p[:, :LANE].astype(BF16)
                stack_ref[par, half:2 * half, c0:c0 + LANE] = p[:, LANE:].astype(BF16)

    scale = 1.0 / math.sqrt(2 * half * (D_FOURIER // FOURIER_GROUPS))
    ev = _dot(te_ref[...], stack_ref[0])
    od = _dot(to_ref[...], stack_ref[1])
    o_ref[0] = ((ev + od) * scale).astype(o_ref.dtype)
    o_ref[1] = ((ev - od) * scale).astype(o_ref.dtype)


def _fourier(p4, cg, te, to, *, tl):
    nb, _, half, _ = p4.shape
    out = pl.pallas_call(
        functools.partial(_fourier_kernel, half=half),
        grid=(nb, half // tl),
        in_specs=[
            pl.BlockSpec((None, 2, half, D_MIX), lambda b, j: (b, 0, 0, 0)),
            _const_spec(cg.shape),
            pl.BlockSpec((tl, 2 * half), lambda b, j: (j, 0)),
            pl.BlockSpec((tl, 2 * half), lambda b, j: (j, 0)),
        ],
        out_specs=pl.BlockSpec((None, 2, tl, D_MIX), lambda b, j: (b, 0, j, 0)),
        out_shape=jax.ShapeDtypeStruct((nb, 2, half, D_MIX), BF16),
        scratch_shapes=[pltpu.VMEM((2, 2 * half, D_MIX), BF16)],
        compiler_params=_params("parallel", "arbitrary"),
        name="fourier_mix",
    )(p4, cg, te, to)
    return out.reshape(nb * 2 * half, D_MIX)


def _hyena_filter_kernel(z_ref, w1_ref, b1_ref, w2_ref, b2_ref, w3f_ref, w3b_ref, freq_ref, delta_ref,
                         fwd_ref, bwd_ref, hdn_ref):
    hi = lax.Precision.HIGHEST
    z = z_ref[...]

    @pl.when(pl.program_id(0) == 0)
    def _():
        freq = freq_ref[...]
        h1 = jnp.sin(freq * (jnp.dot(z, w1_ref[...], precision=hi, preferred_element_type=F32) + b1_ref[...]))
        hdn_ref[...] = jnp.sin(freq * (jnp.dot(h1, w2_ref[...], precision=hi, preferred_element_type=F32)
                                       + b2_ref[...]))

    hdn = hdn_ref[...]
    decay = jnp.exp(-z[:, 0:1] * delta_ref[...])
    fwd = jnp.dot(hdn, w3f_ref[...], precision=hi, preferred_element_type=F32) * decay
    bwd = jnp.dot(hdn, w3b_ref[...], precision=hi, preferred_element_type=F32) * decay
    row = lax.broadcasted_iota(jnp.int32, bwd.shape, 0)
    bwd = jnp.where(row == 0, 0.0, bwd)
    norm = jnp.sum(jnp.abs(fwd), axis=0, keepdims=True) + jnp.sum(jnp.abs(bwd), axis=0, keepdims=True) + 1e-6
    inv = 1.0 / norm
    fwd_ref[...] = fwd * inv
    bwd_ref[...] = bwd * inv


def _hyena_filters(seq, w1, b1, w2, b2, w3, freq):
    pos = jnp.concatenate([jnp.arange(0, seq, 2), jnp.arange(1, seq, 2)]).astype(F32)[:, None]
    t = pos / (seq - 1)
    bands = (HY_EMB - 1) // 2
    fr = jnp.linspace(1e-4, bands - 1, bands, dtype=F32)[None, :]
    w = 2.0 * math.pi * pos * fr / seq
    z = jnp.concatenate([t, jnp.cos(w), -jnp.sin(w), jnp.zeros((seq, LANE - HY_EMB), F32)], axis=-1)
    pad = LANE - HY_HID
    w1p = jnp.pad(w1, ((0, LANE - HY_EMB), (0, pad)))
    w2p = jnp.pad(w2, ((0, pad), (0, pad)))
    w3p = jnp.pad(w3, ((0, pad), (0, 0)))
    row = lambda v: jnp.pad(v, (0, pad)).reshape(1, LANE)
    delta = jnp.abs(jnp.linspace(math.log(HY_TARGET) / HY_SLOW_DECAY, math.log(HY_TARGET) / HY_FAST_DECAY,
                                 D_HYENA, dtype=F32)).reshape(1, D_HYENA)
    sq = lambda: _const_spec((LANE, LANE))
    vec = lambda: _const_spec((1, LANE))
    n_ord = 2
    return pl.pallas_call(
        _hyena_filter_kernel,
        grid=(n_ord,),
        in_specs=[_const_spec((seq, LANE)), sq(), vec(), sq(), vec(),
                  pl.BlockSpec((LANE, D_MIX), lambda o: (0, o)),
                  pl.BlockSpec((LANE, D_MIX), lambda o: (0, n_ord + o)),
                  vec(), _const_spec((1, D_MIX))],
        out_specs=[pl.BlockSpec((seq, D_MIX), lambda o: (0, o))] * 2,
        out_shape=[jax.ShapeDtypeStruct((seq, n_ord * D_MIX), F32)] * 2,
        scratch_shapes=[pltpu.VMEM((seq, LANE), F32)],
        compiler_params=_params("arbitrary"),
        name="hyena_filters",
    )(z, w1p, row(b1), w2p, row(b2), w3p, w3p, row(freq), delta)


def _alt_sum(x):
    row = lax.broadcasted_iota(jnp.int32, x.shape, 0)
    return jnp.sum(jnp.where((row & 1) == 1, -x, x), axis=0, keepdims=True)


def _alt_rows(v, shape):
    row = lax.broadcasted_iota(jnp.int32, shape, 0)
    return jnp.where((row & 1) == 1, -v, v)


def _half_spectrum(xe, xo, ce, se, co, so):
    ac, bc = _dot(ce, xe), _dot(co, xo)
    a_s, b_s = _dot(se, xe), _dot(so, xo)
    return ac + bc, a_s + b_s, ac - bc, b_s - a_s


def _hyena_spec_kernel(ce_ref, se_ref, co_ref, so_ref, fwd_ref, bwd_ref, alo_ref, blo_ref, ahi_ref, bhi_ref, mid_ref,
                       *, half):
    n = 4 * half
    tabs = (ce_ref[...], se_ref[...], co_ref[...], so_ref[...])
    fwd, bwd = fwd_ref[...], bwd_ref[...]
    fe, fo, be, bo = fwd[:half], fwd[half:], bwd[:half], bwd[half:]
    ce, se, co, so = tabs
    se_, so_ = (fe + be).astype(BF16), (fo + bo).astype(BF16)
    de_, do_ = (be - fe).astype(BF16), (bo - fo).astype(BF16)
    ac, bc = _dot(ce, se_), _dot(co, so_)
    a_s, b_s = _dot(se, de_), _dot(so, do_)
    row = lax.broadcasted_iota(jnp.int32, ac.shape, 0)
    wgt = jnp.where(row == 0, 1.0 / n, 2.0 / n)
    alo_ref[...] = wgt * (ac + bc)
    blo_ref[...] = wgt * (a_s + b_s)
    ahi_ref[...] = wgt * (ac - bc)
    bhi_ref[...] = wgt * (b_s - a_s)
    mid_ref[0:1, :] = (2.0 / n) * (_alt_sum(fe) + _alt_sum(be))
    mid_ref[1:2, :] = (2.0 / n) * (_alt_sum(bo) - _alt_sum(fo))


def _hyena_spectra(tabs, fwd, bwd, *, cb=256):
    seq, width = fwd.shape
    half = seq // 2
    blk = lambda rows: pl.BlockSpec((rows, cb), lambda j: (0, j))
    return pl.pallas_call(
        functools.partial(_hyena_spec_kernel, half=half),
        grid=(width // cb,),
        in_specs=[_const_spec((half, half))] * 4 + [blk(seq), blk(seq)],
        out_specs=[blk(half)] * 4 + [blk(2)],
        out_shape=[jax.ShapeDtypeStruct((half, width), F32)] * 4 + [jax.ShapeDtypeStruct((2, width), F32)],
        compiler_params=_params("parallel"),
        name="hyena_spectra",
    )(*tabs[:4], fwd, bwd)


def _short_conv(e, o, w, b):
    half = e.shape[0]
    row = lax.broadcasted_iota(jnp.int32, e.shape, 0)
    o_prev = jnp.where(row == 0, 0.0, pltpu.roll(o, 1, 0))
    e_next = jnp.where(row == half - 1, 0.0, pltpu.roll(e, half - 1, 0))
    w0, w1, w2 = w[0:1, :], w[1:2, :], w[2:3, :]
    return o_prev * w0 + e * w1 + o * w2 + b, e * w0 + o * w1 + e_next * w2 + b


HYENA_ROW_SPLIT = 2


def _hyena_order_kernel(z_ref, x_ref, ce_ref, se_ref, co_ref, so_ref, cot_ref, sot_ref, alo_ref, blo_ref, ahi_ref,
                        bhi_ref, mid_ref, skip_ref, wx_ref, bx_ref, *rest, order, first):
    if first:
        wz_ref, bz0_ref, o_ref = rest
        ze, zo = _short_conv(z_ref[0], z_ref[1], wz_ref[...], bz0_ref[...])
    else:
        (o_ref,) = rest
        ze, zo = z_ref[0], z_ref[1]
    half = ze.shape[0]
    groups = [slice(r, r + half // HYENA_ROW_SPLIT) for r in range(0, half, half // HYENA_ROW_SPLIT)]
    zeb, zob = ze.astype(BF16), zo.astype(BF16)
    prods = []
    for rs in groups:
        re_lo, im_lo, re_hi, im_hi = _half_spectrum(zeb, zob, ce_ref[rs, :], se_ref[rs, :], co_ref[rs, :],
                                                    so_ref[rs, :])
        alo, blo, ahi, bhi = alo_ref[rs, :], blo_ref[rs, :], ahi_ref[rs, :], bhi_ref[rs, :]
        wr_lo, wi_lo = re_lo * alo + im_lo * blo, re_lo * blo - im_lo * alo
        wr_hi, wi_hi = re_hi * ahi + im_hi * bhi, re_hi * bhi - im_hi * ahi
        prods.append(((wr_lo + wr_hi).astype(BF16), (wi_lo - wi_hi).astype(BF16),
                      (wr_lo - wr_hi).astype(BF16), (wi_lo + wi_hi).astype(BF16)))
    p_even_c, p_even_s, p_odd_c, p_odd_s = (jnp.concatenate(p, axis=0) for p in zip(*prods))
    me, mo = _short_conv(x_ref[0], x_ref[1], wx_ref[...], bx_ref[...])
    re_mid, im_mid = _alt_sum(ze), _alt_sum(zo)
    a_mid, b_mid = mid_ref[0:1, :], mid_ref[1:2, :]
    mid_e, mid_o = re_mid * a_mid + im_mid * b_mid, re_mid * b_mid - im_mid * a_mid
    skip = skip_ref[order:order + 1, :]
    for rs in groups:
        conv_e = _dot(ce_ref[rs, :], p_even_c) - _dot(se_ref[rs, :], p_even_s)
        conv_o = _dot(cot_ref[rs, :], p_odd_c) - _dot(sot_ref[rs, :], p_odd_s)
        conv_e = conv_e + _alt_rows(mid_e, conv_e.shape)
        conv_o = conv_o - _alt_rows(mid_o, conv_o.shape)
        o_ref[0, rs, :] = (me[rs] * (conv_e + ze[rs] * skip)).astype(o_ref.dtype)
        o_ref[1, rs, :] = (mo[rs] * (conv_o + zo[rs] * skip)).astype(o_ref.dtype)


def _hyena_order(z4, z_blk0, x4, x_blk0, tabs, coefs, skip, conv_w, conv_b, *, order, first, out_dtype, cb=256):
    nb, _, half, _ = x4.shape
    ncb = D_MIX // cb
    xw0 = (1 + order) * ncb
    conv_b = conv_b.reshape(1, -1)
    tok = lambda blk0: pl.BlockSpec((None, 2, half, cb), lambda j, b: (b, 0, 0, blk0 + j))
    colblk = lambda rows, blk0: pl.BlockSpec((rows, cb), lambda j, b: (0, blk0 + j), pipeline_mode=pl.Buffered(1))
    in_specs = [tok(z_blk0), tok(x_blk0)] + [_const_spec((half, half))] * 6
    in_specs += [colblk(half, order * ncb)] * 4 + [colblk(2, order * ncb), colblk(2, 0), colblk(3, xw0), colblk(1, xw0)]
    args = [z4, x4, *tabs, *coefs, skip, conv_w, conv_b]
    if first:
        in_specs += [colblk(3, 0), colblk(1, 0)]
        args += [conv_w, conv_b]
    return pl.pallas_call(
        functools.partial(_hyena_order_kernel, order=order, first=first),
        grid=(ncb, nb),
        in_specs=in_specs,
        out_specs=pl.BlockSpec((None, 2, half, cb), lambda j, b: (b, 0, 0, j)),
        out_shape=jax.ShapeDtypeStruct((nb, 2, half, D_MIX), out_dtype),
        compiler_params=_params("parallel", "parallel"),
        name="hyena_order",
    )(*args)


def _dit_tables(half, period):
    split = math.gcd(half, 64)

    def part(rows, mult, col_off):
        r = lax.broadcasted_iota(jnp.int32, (rows, half), 0) * mult
        c = 2 * lax.broadcasted_iota(jnp.int32, (rows, half), 1) + col_off
        ang = ((r * c) % period).astype(F32) * (2.0 * math.pi / period)
        return jnp.cos(ang), jnp.sin(ang)

    out = []
    for col_off in (0, 1):
        ca, sa = (t[:, None, :] for t in part(half // split, split, col_off))
        cb, sb = (t[None, :, :] for t in part(split, 1, col_off))
        out += [(ca * cb - sa * sb).reshape(half, half), (sa * cb + ca * sb).reshape(half, half)]
    return out


def _fourier_tables(seq):
    ce, se, co, so = _dit_tables(seq // 2, seq)
    te = jnp.concatenate([ce, -se], axis=1).astype(BF16)
    to = jnp.concatenate([co, -so], axis=1).astype(BF16)
    gsz = D_FOURIER // FOURIER_GROUPS
    r = lax.broadcasted_iota(jnp.int32, (gsz, gsz), 0)
    c = lax.broadcasted_iota(jnp.int32, (gsz, gsz), 1)
    ang = ((r * c) % gsz).astype(F32) * (2.0 * math.pi / gsz)
    eye = jnp.eye(LANE // gsz, dtype=F32)
    cg = jnp.concatenate([jnp.kron(eye, jnp.cos(ang)), jnp.kron(eye, jnp.sin(ang))], axis=1).astype(BF16)
    return te, to, cg


def _hyena_tables(seq):
    ce, se, co, so = (t.astype(BF16) for t in _dit_tables(seq // 2, 2 * seq))
    return ce, se, co, so, co.T, so.T


def _rms_heads(x, g):
    parts = []
    for h in range(D_MIX // LANE):
        xh = x[:, h * LANE:(h + 1) * LANE]
        parts.append(xh * lax.rsqrt(jnp.mean(xh * xh, axis=-1, keepdims=True) + LN_EPS) * g)
    return jnp.concatenate(parts, axis=-1)


def _outproj_kernel(x_ref, ya_ref, yb_ref, *rest, tm, seq, is_ctx, readout):
    if readout:
        r_ref, gd_ref, gg_ref, hg_ref, unperm_ref, mod_ref, w_ref, g_ref, b_ref, o_ref = rest
        r = r_ref[...]
        ya = (_rms_heads(ya_ref[...], gg_ref[...]) * (r * _sigmoid(r))).astype(BF16)
        yb = _rms_heads(yb_ref[...].reshape(tm, D_MIX) * _sigmoid(gd_ref[...].reshape(tm, D_MIX)), hg_ref[...])
        yb = _dot(unperm_ref[...], yb.astype(BF16)).astype(BF16)
    else:
        unperm_ref, mod_ref, w_ref, g_ref, b_ref, o_ref = rest
        ya = ya_ref[...]
        half = DEINT_BLOCK // 2
        parts = []
        for r in range(0, tm // 2, half):
            blk = jnp.concatenate([yb_ref[0, r:r + half, :], yb_ref[1, r:r + half, :]], axis=0)
            parts.append(_dot(unperm_ref[...], blk).astype(BF16))
        yb = parts[0] if len(parts) == 1 else jnp.concatenate(parts, axis=0)
    row = _mod_row_index(pl.program_id(0), tm, seq, is_ctx)
    gate = _mod_row(mod_ref, row, 5)
    y = _dot(ya, w_ref[:D_MIX, :]) + _dot(yb, w_ref[D_MIX:, :])
    z = ALPHA * x_ref[...] + gate * y
    o_ref[...] = _ln(z) * g_ref[...] + b_ref[...]


def _outproj(x2, ya, yb, mod, layer, w, ln_g, ln_b, *, seq, is_ctx, readout=None):
    n = x2.shape[0]
    tm = _token_tile(n, seq, False, want=TOKEN_TILE if readout is None else TOKEN_TILE // 2)
    tiles = seq // tm
    tok = lambda: pl.BlockSpec((tm, D_MIX), lambda i: (i, 0))
    in_specs = [pl.BlockSpec((tm, D_MODEL), lambda i: (i, 0)), tok()]
    args = [x2, ya, yb]
    if readout is None:
        assert tm % DEINT_BLOCK == 0
        unperm = _deint_perm(inverse=True)
        in_specs += [pl.BlockSpec((None, 2, tm // 2, D_MIX), lambda i: (i // tiles, 0, i % tiles, 0)),
                     _const_spec(unperm.shape)]
        args += [unperm]
    else:
        p_main, p_hg, gla_g, hg_g = readout
        rows = tm // GRID_W
        assert tm % GRID_W == 0 and rows % 8 == 0
        i = lax.broadcasted_iota(jnp.int32, (tm, tm), 0)
        j = lax.broadcasted_iota(jnp.int32, (tm, tm), 1)
        unperm = jnp.where(j == (i % GRID_W) * rows + i // GRID_W, 1.0, 0.0).astype(BF16)
        colmajor = lambda blk: pl.BlockSpec((None, GRID_W, rows, D_MIX), lambda i: (i // tiles, 0, i % tiles, blk))
        in_specs += [colmajor(0), pl.BlockSpec((tm, D_MIX), lambda i: (i, PM_R)), colmajor(PH_GD),
                     _const_spec((1, LANE)), _const_spec((1, LANE)), _const_spec((tm, tm))]
        args += [p_main, p_hg, gla_g.reshape(1, LANE), hg_g.reshape(1, LANE), unperm]
    in_specs += [_picked_spec(mod.shape, (layer,)), _const_spec(w.shape),
                 _picked_spec(ln_g.shape, (layer, 1)), _picked_spec(ln_b.shape, (layer, 1))]
    args += [mod, w, ln_g, ln_b]
    kern = functools.partial(_outproj_kernel, tm=tm, seq=seq, is_ctx=is_ctx, readout=readout is not None)
    return pl.pallas_call(
        kern,
        grid=(n // tm,),
        in_specs=in_specs,
        out_specs=pl.BlockSpec((tm, D_MODEL), lambda i: (i, 0)),
        out_shape=jax.ShapeDtypeStruct((n, D_MODEL), F32),
        compiler_params=_params("parallel"),
        name="mixer_out",
    )(*args)


def _split3(x):
    hi = x.astype(BF16)
    r1 = x - hi.astype(F32)
    mid = r1.astype(BF16)
    lo = (r1 - mid.astype(F32)).astype(BF16)
    return hi, mid, lo


GLA_UNROLL = 32
HGRN_UNROLL = 8
CUMSUM_CHUNKS = 4


def _scan_unroll(n_chunks, want):
    return math.gcd(want, n_chunks)


def _scan_consts(rev, unroll, heads, dk):
    n = math.gcd(unroll, CUMSUM_CHUNKS) * CHUNK
    i = lax.broadcasted_iota(jnp.int32, (n, n), 0)
    j = lax.broadcasted_iota(jnp.int32, (n, n), 1)
    same_chunk = (i // CHUNK) == (j // CHUNK)
    tri = jnp.where(same_chunk & ((j >= i) if rev else (j <= i)), 1.0, 0.0).astype(BF16)
    r = lax.broadcasted_iota(jnp.int32, (CHUNK, heads * CHUNK), 0)
    c = lax.broadcasted_iota(jnp.int32, (CHUNK, heads * CHUNK), 1) & (CHUNK - 1)
    causal = (c >= r) if rev else (c <= r)
    klane_head = lax.broadcasted_iota(jnp.int32, (1, heads * dk), 1) // dk
    return tri, causal, klane_head


def _block_diag(blocks):
    z = jnp.zeros_like(blocks[0])
    return jnp.concatenate([jnp.concatenate([b if c == h else z for c in range(len(blocks))], axis=1)
                            for h, b in enumerate(blocks)], axis=0)


def _scan_local(q, k, g, vb, consts, *, rev, heads, dk, need_out):
    tri, causal, klane_head = consts
    kt = heads * dk
    parts = _split3(g)
    span = tri.shape[0]
    gsum = jnp.concatenate([sum(_dot(tri, part[r:r + span]) for part in parts) for r in range(0, g.shape[0], span)],
                           axis=0)
    staged = []
    for u in range(g.shape[0] // CHUNK):
        rows = slice(u * CHUNK, (u + 1) * CHUNK)
        gs = gsum[rows]
        gt = gs[0:1] if rev else gs[CHUNK - 1:CHUNK]
        ku, vu = k[rows], vb[rows]
        q_in = o_intra = None
        if need_out:
            q_in = (q[rows] * jnp.exp(gs)).astype(BF16)
            k_dec = ku * jnp.exp(-gs)
            k_in = k_dec.astype(BF16)
            k_out = (k_dec * jnp.exp(gt)).astype(BF16)
            if dk % LANE == 0:
                k4 = _block_diag([k_in[:, h * dk:(h + 1) * dk] for h in range(heads)])
            else:
                k4 = jnp.concatenate([jnp.where(klane_head == h, k_in, jnp.zeros_like(k_in)) for h in range(heads)],
                                     axis=0)
            att = jnp.where(causal, _dot_nt(q_in, k4), 0.0).astype(BF16)
            o_intra = _dot(att, _block_diag([vu[:, h * LANE:(h + 1) * LANE] for h in range(heads)]))
        else:
            k_out = (ku * jnp.exp(gt - gs)).astype(BF16)
        if dk % LANE == 0:
            inc = [_dot_tn(k_out[:, h * dk:(h + 1) * dk], vu[:, h * LANE:(h + 1) * LANE]) for h in range(heads)]
        else:
            k_t = jnp.transpose(jnp.concatenate([k_out.astype(F32), jnp.zeros((LANE - CHUNK, kt), F32)], axis=0))
            k_t = k_t.astype(BF16)
            v_pad = jnp.concatenate([vu, jnp.zeros((LANE - CHUNK, heads * LANE), BF16)], axis=0)
            inc = [_dot(k_t[h * dk:(h + 1) * dk, :], v_pad[:, h * LANE:(h + 1) * LANE]) for h in range(heads)]
        dec = jnp.transpose(jnp.broadcast_to(jnp.exp(gt), (LANE, kt)))
        staged.append((q_in, o_intra, inc, [dec[h * dk:(h + 1) * dk] for h in range(heads)]))
    return staged


def _scan_states(s_ref, base, staged):
    heads = len(staged[0][2])
    state = [s_ref[base + h] for h in range(heads)]
    outs = []
    for q_in, o_intra, inc, decay in staged:
        if q_in is not None:
            outs.append(o_intra + _dot(q_in, _block_diag([s.astype(BF16) for s in state])))
        state = [s * d + i for s, d, i in zip(state, decay, inc)]
    for h in range(heads):
        s_ref[base + h] = state[h]
    return outs


def _gla_kernel(qk_ref, v_ref, g_ref, cqk_ref, cv_ref, cg_ref, o_ref, s_ref, ob_ref, *, seq, ctx_len):
    heads, dk = GLA_HEADS, GLA_DK
    kt = heads * dk
    s_ref[...] = jnp.zeros_like(s_ref)

    def make_run(qk, vv, gg, n_chunks, need_out):
        unroll = _scan_unroll(n_chunks, GLA_UNROLL)
        rows = unroll * CHUNK
        consts = [_scan_consts(rev, unroll, heads, dk) for rev in (False, True)]

        def run(n):
            work = []
            for d, rev in enumerate((False, True)):
                start = (n_chunks * CHUNK - rows - n * rows) if rev else n * rows
                sl = pl.ds(pl.multiple_of(start, rows), rows)
                q = qk[sl, 0:kt] * (dk ** -0.5) if need_out else None
                staged = _scan_local(q, qk[sl, kt:2 * kt], gg[sl, d * kt:(d + 1) * kt], vv[sl, :].astype(BF16),
                                     consts[d], rev=rev, heads=heads, dk=dk, need_out=need_out)
                work.append((d, rev, sl, staged[::-1] if rev else staged))
            for d, rev, sl, staged in work:
                outs = _scan_states(s_ref, d * heads, staged)
                if need_out:
                    (ob_ref if rev else o_ref)[sl, :] = jnp.concatenate(outs[::-1] if rev else outs, axis=0)

        return run, n_chunks // unroll

    run, steps = make_run(cqk_ref, cv_ref, cg_ref, ctx_len // CHUNK, False)
    for n in range(steps):
        run(n)
    run, steps = make_run(qk_ref, v_ref, g_ref, seq // CHUNK, True)

    def body(n, carry):
        run(n)
        return carry

    lax.fori_loop(0, steps, body, 0)
    o_ref[...] = o_ref[...] + ob_ref[...]


def _gla(p3, pc3):
    nb, seq, _ = p3.shape
    ctx_len = pc3.shape[1]
    blk = lambda length, j: pl.BlockSpec((None, length, D_MIX), lambda b: (b, 0, j))
    return pl.pallas_call(
        functools.partial(_gla_kernel, seq=seq, ctx_len=ctx_len),
        grid=(nb,),
        in_specs=[blk(seq, PM_QK), blk(seq, PM_V), blk(seq, PM_GATE),
                  blk(ctx_len, P_QK), blk(ctx_len, P_V), blk(ctx_len, P_GATE)],
        out_specs=pl.BlockSpec((None, seq, D_MIX), lambda b: (b, 0, 0)),
        out_shape=jax.ShapeDtypeStruct((nb, seq, D_MIX), F32),
        scratch_shapes=[pltpu.VMEM((2 * GLA_HEADS, GLA_DK, GLA_DV), F32),
                        pltpu.VMEM((seq, D_MIX), F32)],
        compiler_params=_params("parallel"),
        name="gla_scan",
    )(p3, p3, p3, pc3, pc3, pc3)


def _hgrn_kernel(q_ref, ff_ref, fb_ref, v_ref, cff_ref, cfb_ref, cv_ref, o_ref, s_ref, *, seq, ctx_len):
    heads, dk = HG_HEADS, HG_D
    s_ref[...] = jnp.zeros_like(s_ref)

    def make_run(qq, gates, vv, n_chunks, need_out):
        unroll = _scan_unroll(n_chunks, HGRN_UNROLL)
        rows = unroll * CHUNK
        consts = [_scan_consts(rev, unroll, heads, dk) for rev in (False, True)]

        def run(n):
            work = []
            for d, rev in enumerate((False, True)):
                start = (n_chunks * CHUNK - rows - n * rows) if rev else n * rows
                sl = pl.ds(pl.multiple_of(start, rows), rows)
                f = gates[d][sl, :]
                staged = _scan_local(qq[sl, :] if need_out else None, 1.0 - f, jnp.log(f), vv[sl, :].astype(BF16), consts[d],
                                     rev=rev, heads=heads, dk=dk, need_out=need_out)
                work.append((d, rev, sl, staged[::-1] if rev else staged))
            for d, rev, sl, staged in work:
                outs = _scan_states(s_ref, d * heads, staged)
                if need_out:
                    o_ref[sl, :] += jnp.concatenate(outs[::-1] if rev else outs, axis=0)

        return run, n_chunks // unroll

    run, steps = make_run(None, (cff_ref, cfb_ref), cv_ref, ctx_len // CHUNK, False)
    for n in range(steps):
        run(n)
    o_ref[...] = jnp.zeros_like(o_ref)
    run, steps = make_run(q_ref, (ff_ref, fb_ref), v_ref, seq // CHUNK, True)

    def body(n, carry):
        run(n)
        return carry

    lax.fori_loop(0, steps, body, 0)


def _hgrn(ph3, pc3):
    nb, seq, _ = ph3.shape
    ctx_len = pc3.shape[1]
    blk = lambda length, j: pl.BlockSpec((None, length, D_MIX), lambda b: (b, 0, j))
    return pl.pallas_call(
        functools.partial(_hgrn_kernel, seq=seq, ctx_len=ctx_len),
        grid=(nb,),
        in_specs=[blk(seq, PH_QD), blk(seq, PH_FFL), blk(seq, PH_FBL), blk(seq, PH_IDD),
                  blk(ctx_len, P_FFL), blk(ctx_len, P_FBL), blk(ctx_len, P_IDD)],
        out_specs=pl.BlockSpec((None, seq, D_MIX), lambda b: (b, 0, 0)),
        out_shape=jax.ShapeDtypeStruct((nb, seq, D_MIX), F32),
        scratch_shapes=[pltpu.VMEM((2 * HG_HEADS, HG_D, HG_D), F32)],
        compiler_params=_params("parallel"),
        name="hgrn_scan",
    )(ph3, ph3, ph3, ph3, pc3, pc3, pc3)


def _even_mixer(x2, nb, seq, mod, layer, is_ctx, w_in, w_out, conv_w, conv_b, filt, skip, ln_g, ln_b):
    p4 = _proj_deint(x2, nb, seq, mod, layer, w_in, is_ctx=is_ctx)
    te, to, cg = _fourier_tables(seq)
    ya = _fourier(p4, cg, te, to, tl=min(seq // 2, 512))
    tabs = _hyena_tables(seq)
    fwd, bwd = _hyena_filters(seq, *filt)
    coefs = _hyena_spectra(tabs, fwd, bwd)
    cb = 256
    u0 = D_FOURIER // cb
    per = D_HYENA // cb
    y1 = _hyena_order(p4, u0, p4, u0 + per, tabs, coefs, skip, conv_w, conv_b, order=0, first=True, out_dtype=F32,
                      cb=cb)
    y2 = _hyena_order(y1, 0, p4, u0 + 2 * per, tabs, coefs, skip, conv_w, conv_b, order=1, first=False,
                      out_dtype=BF16, cb=cb)
    return _outproj(x2, ya, y2, mod, layer, w_out, ln_g, ln_b, seq=seq, is_ctx=is_ctx)


def _odd_proj_weights(w_in, a_up, a_b):
    kq = GLA_HEADS * GLA_DK
    low0 = 2 * kq + 2 * D_MIX
    low1 = low0 + 2 * GLA_RANK
    w = jnp.concatenate([w_in[:, :low0], w_in[:, low1:], w_in[:, low0:low1],
                         jnp.zeros((D_MODEL, LANE - 2 * GLA_RANK), w_in.dtype)], axis=1).astype(BF16)
    up = jnp.zeros((LANE, 2 * kq), F32)
    up = up.at[0:GLA_RANK, 0:kq].set(a_up[0]).at[GLA_RANK:2 * GLA_RANK, kq:].set(a_up[1]).astype(BF16)
    return w, up, a_b.reshape(1, 2 * kq)


def kernel(x, c, ctx, c_ctx, mod_w, mod_b, ffn_w_in, ffn_w_out, ln_g, ln_b, ev_w_in, ev_w_out, hy_conv_w, hy_conv_b, hy_w1, hy_b1, hy_w2, hy_b2, hy_w3, hy_freq, hy_skip, od_w_in, od_w_out, gla_a_up, gla_a_b, gla_norm_g, hg_lb, hg_norm_g):
    nb, seq, d = x.shape
    ctx_len = ctx.shape[1]
    assert d == D_MODEL and seq % 512 == 0 and ctx_len % CHUNK == 0 and nb + 1 <= MOD_ROWS - 7
    assert mod_w.shape[0] == DEPTH == 2

    c_all = jnp.zeros((MOD_ROWS, D_MODEL), F32).at[:nb].set(c).at[MOD_ROWS - 8].set(c_ctx)
    mod = _mod_table(c_all, mod_w, mod_b)
    w_in_b = ffn_w_in.astype(BF16)
    w_out_b = ffn_w_out.astype(BF16)

    xs = x.reshape(nb * seq, D_MODEL)
    xc = ctx.reshape(nb * ctx_len, D_MODEL)
    streams = [(xs, seq, False), (xc, ctx_len, True)]

    ln_g4 = ln_g.reshape(DEPTH, 3, 1, D_MODEL)
    ln_b4 = ln_b.reshape(DEPTH, 3, 1, D_MODEL)

    def ffn(stream, layer, half):
        arr, length, is_ctx = stream
        return (_ffn(arr, mod, layer, half, w_in_b, w_out_b, ln_g4, ln_b4, seq=length, is_ctx=is_ctx), length, is_ctx)

    streams = [ffn(s, 0, 0) for s in streams]
    filt = (hy_w1[0], hy_b1[0], hy_w2[0], hy_b2[0], hy_w3[0], hy_freq[0])
    ev_in_b, ev_out_b = ev_w_in[0].astype(BF16), ev_w_out[0].astype(BF16)
    streams = [(_even_mixer(arr, nb, length, mod, 0, is_ctx, ev_in_b, ev_out_b, hy_conv_w[0], hy_conv_b[0], filt,
                            hy_skip[0], ln_g4, ln_b4), length, is_ctx) for arr, length, is_ctx in streams]
    streams = [ffn(s, 0, 1) for s in streams]

    streams = [ffn(s, 1, 0) for s in streams]
    (xs, _, _), (xc, _, _) = streams
    w_proj, up, up_b = _odd_proj_weights(od_w_in[0], gla_a_up[0], gla_a_b[0])
    n_gla, n_hg = (P_R + 1) * D_MIX, (PH_GD + 1) * D_MIX
    w_main = jnp.concatenate([w_proj[:, :n_gla], w_proj[:, n_gla + n_hg:]], axis=1)
    pc = _proj(xc, mod, 1, w_proj, seq=ctx_len, is_ctx=True, gates=(up, up_b), forget=(hg_lb, P_FFL * D_MIX),
               want=TOKEN_TILE // 2)
    pm = _proj(xs, mod, 1, w_main, seq=seq, is_ctx=False, gates=(up, up_b))
    ph = _proj_colmajor(xs, nb, seq, mod, 1, w_proj[:, n_gla:n_gla + n_hg], hg_lb)
    pc3 = pc.reshape(nb, ctx_len, -1)
    o_gla = _gla(pm.reshape(nb, seq, -1), pc3).reshape(nb * seq, D_MIX)
    o_hg = _hgrn(ph.reshape(nb, seq, -1), pc3).reshape(ph.shape[:3] + (D_MIX,))
    xs = _outproj(xs, o_gla, o_hg, mod, 1, od_w_out[0].astype(BF16), ln_g4, ln_b4, seq=seq, is_ctx=False,
                  readout=(pm, ph, gla_norm_g[0], hg_norm_g[0]))
    xs, _, _ = ffn((xs, seq, False), 1, 1)
    return xs.reshape(nb, seq, D_MODEL)
```
